```python
import math
import jax, jax.numpy as jnp
from jax import lax
import numpy as np

D_MODEL = 1024
BATCH = 16
SEQ = 2048
DEPTH = 1

HEAD_DIM = 64
BLOCK = 128
A_Q_HEADS = 16
A_KV_HEADS = 2
A_GROUP = A_Q_HEADS // A_KV_HEADS
A_WINDOW = 128
B_PATTERNS = ((128, 1), (512, 4), (2048, 16))
B_HEADS_PER_GROUP = 8
B_HEADS = B_HEADS_PER_GROUP * len(B_PATTERNS)
A_Q_W = A_Q_HEADS * HEAD_DIM
A_KV_W = A_KV_HEADS * HEAD_DIM
B_W = B_HEADS * HEAD_DIM
B_OUT_W = B_HEADS_PER_GROUP * HEAD_DIM
IN_W = A_Q_W + 2 * A_KV_W + 3 * B_W + 2 * D_MODEL
D_FF = -(-8 * D_MODEL // (3 * 256)) * 256
ROPE_THETA = 10000.0
LN_EPS = 1e-5
DEEPNORM_ALPHA = (2 * DEPTH) ** 0.25
DEEPNORM_BETA = (8 * DEPTH) ** -0.25
NEG_INF = -1e30

kernel_name = "hybrid_swa_sink_dilated_gated_deepnorm_adaln"


def layer_norm(x, g, b):
    xf = x.astype(jnp.float32)
    mu = jnp.mean(xf, axis=-1, keepdims=True)
    xc = xf - mu
    var = jnp.mean(xc * xc, axis=-1, keepdims=True)
    y = xc * lax.rsqrt(var + LN_EPS)
    return (y * g.astype(jnp.float32) + b.astype(jnp.float32)).astype(x.dtype)


def rope(x, positions):
    half = HEAD_DIM // 2
    inv = ROPE_THETA ** (-jnp.arange(half, dtype=jnp.float32) / half)
    ang = positions.astype(jnp.float32)[..., None] * inv
    cos = jnp.cos(ang)[:, :, None, :]
    sin = jnp.sin(ang)[:, :, None, :]
    xf = x.astype(jnp.float32)
    x1, x2 = xf[..., :half], xf[..., half:]
    out = jnp.concatenate([x1 * cos - x2 * sin, x2 * cos + x1 * sin], axis=-1)
    return out.astype(x.dtype)


def banded_window_attention(q, k, v, n_back, sink=None):
    N, T, Hkv, G, dh = q.shape
    nblk = -(-T // BLOCK)
    pad = nblk * BLOCK - T
    qp = jnp.pad(q, ((0, 0), (0, pad), (0, 0), (0, 0), (0, 0)))
    kp = jnp.pad(k, ((0, 0), (BLOCK, pad), (0, 0), (0, 0)))
    vp = jnp.pad(v, ((0, 0), (BLOCK, pad), (0, 0), (0, 0)))
    qb = qp.reshape(N, nblk, BLOCK, Hkv, G, dh)
    kb = kp.reshape(N, nblk + 1, BLOCK, Hkv, dh)
    vb = vp.reshape(N, nblk + 1, BLOCK, Hkv, dh)
    kw = jnp.concatenate([kb[:, :-1], kb[:, 1:]], axis=2)
    vw = jnp.concatenate([vb[:, :-1], vb[:, 1:]], axis=2)
    s = jnp.einsum('nbqhgd,nbkhd->nbhgqk', qb, kw,
                   preferred_element_type=jnp.float32) * (dh ** -0.5)
    qi = jnp.arange(BLOCK)[:, None]
    ki = jnp.arange(2 * BLOCK)[None, :]
    dist = qi + BLOCK - ki
    kpos = jnp.arange(nblk)[:, None, None] * BLOCK + ki[None] - BLOCK
    mask = (dist >= 0) & (dist <= n_back) & (kpos >= 0)
    s = jnp.where(mask[None, :, None, None], s, NEG_INF)
    m = jnp.max(s, axis=-1)
    if sink is not None:
        sk = sink.astype(jnp.float32).reshape(Hkv, G)[None, None, :, :, None]
        m = jnp.maximum(m, sk)
    p = jnp.exp(s - m[..., None])
    denom = jnp.sum(p, axis=-1)
    if sink is not None:
        denom = denom + jnp.exp(sk - m)
    o = jnp.einsum('nbhgqk,nbkhd->nbqhgd', p, vw.astype(jnp.float32))
    denom_t = jnp.moveaxis(denom, -1, 2)
    o = o / denom_t[..., None]
    lse = jnp.moveaxis(m, -1, 2) + jnp.log(denom_t)
    o = o.reshape(N, nblk * BLOCK, Hkv, G, dh)[:, :T]
    lse = lse.reshape(N, nblk * BLOCK, Hkv, G)[:, :T]
    return o.astype(q.dtype), lse


def dilated_attention(q, k, v, window, dilation):
    Bn, T, H, dh = q.shape
    r = dilation
    tsub = -(-T // r)
    pad = tsub * r - T

    def to_strided(t):
        t = jnp.pad(t, ((0, 0), (0, pad), (0, 0), (0, 0)))
        return t.reshape(Bn, tsub, r, H, dh).transpose(0, 2, 1, 3, 4).reshape(Bn * r, tsub, H, dh)

    o, lse = banded_window_attention(to_strided(q)[:, :, :, None], to_strided(k), to_strided(v),
                                     window // r)
    o = o[:, :, :, 0].reshape(Bn, r, tsub, H, dh).transpose(0, 2, 1, 3, 4).reshape(Bn, tsub * r, H, dh)[:, :T]
    lse = lse[:, :, :, 0].reshape(Bn, r, tsub, H).transpose(0, 2, 1, 3).reshape(Bn, tsub * r, H)[:, :T]
    return o, lse


def mixer(u, positions, w_in, sinks, w_branch_a, w_branch_b, w_o):
    Bn, T, _ = u.shape
    proj = jnp.einsum('btd,de->bte', u, w_in)
    sizes = [A_Q_W, A_KV_W, A_KV_W, B_W, B_W, B_W, D_MODEL]
    offs, acc = [], 0
    for s_ in sizes:
        acc += s_
        offs.append(acc)
    qa, ka, va, qb, kb, vb, ga, gb = jnp.split(proj, offs, axis=-1)

    qa = rope(qa.reshape(Bn, T, A_Q_HEADS, HEAD_DIM), positions).reshape(Bn, T, A_KV_HEADS, A_GROUP, HEAD_DIM)
    ka = rope(ka.reshape(Bn, T, A_KV_HEADS, HEAD_DIM), positions)
    va = va.reshape(Bn, T, A_KV_HEADS, HEAD_DIM)
    oa, _ = banded_window_attention(qa, ka, va, A_WINDOW - 1, sinks)
    ya = jnp.einsum('bte,ed->btd', oa.reshape(Bn, T, A_Q_W), w_branch_a)

    qb = rope(qb.reshape(Bn, T, B_HEADS, HEAD_DIM), positions)
    kb = rope(kb.reshape(Bn, T, B_HEADS, HEAD_DIM), positions)
    vb = vb.reshape(Bn, T, B_HEADS, HEAD_DIM)
    outs, lses = [], []
    for g, (window, dil) in enumerate(B_PATTERNS):
        sl = slice(g * B_HEADS_PER_GROUP, (g + 1) * B_HEADS_PER_GROUP)
        o_g, l_g = dilated_attention(qb[:, :, sl], kb[:, :, sl], vb[:, :, sl], window, dil)
        outs.append(o_g)
        lses.append(l_g)
    o_all = jnp.stack(outs).astype(jnp.float32)
    wts = jax.nn.softmax(jnp.stack(lses), axis=0)
    ob = jnp.sum(wts[..., None] * o_all, axis=0).astype(u.dtype)
    yb = jnp.einsum('bte,ed->btd', ob.reshape(Bn, T, B_OUT_W), w_branch_b)

    merged = jax.nn.sigmoid(ga) * ya + jax.nn.sigmoid(gb) * yb
    return jnp.einsum('btd,de->bte', merged, w_o)


def swiglu(u, w_gate_up, w_down):
    h = jnp.einsum('btd,df->btf', u, w_gate_up)
    hg, hu = jnp.split(h, 2, axis=-1)
    return jnp.einsum('btf,fd->btd', jax.nn.silu(hg) * hu, w_down)


def setup_inputs(seed: int = 0) -> dict:
    key = jax.random.key(seed)
    ks = jax.random.split(key, 20)
    f32 = jnp.float32
    nrm = lambda k, shape, scale: jax.random.normal(k, shape, f32) * scale
    x = jax.random.normal(ks[0], (BATCH, SEQ, D_MODEL), f32)
    c = jax.random.normal(ks[1], (BATCH, D_MODEL), f32)
    offset = jax.random.randint(ks[2], (BATCH, 1), 0, 1024, dtype=jnp.int32)
    positions = offset + jnp.arange(SEQ, dtype=jnp.int32)[None, :]
    return {
        "x": x,
        "c": c,
        "positions": positions,
        "w_ada": nrm(ks[3], (DEPTH, D_MODEL, 6 * D_MODEL), 0.1 * D_MODEL ** -0.5),
        "b_ada": nrm(ks[4], (DEPTH, 6 * D_MODEL), 0.01),
        "w_in": nrm(ks[5], (DEPTH, D_MODEL, IN_W), D_MODEL ** -0.5),
        "sinks": nrm(ks[6], (DEPTH, A_Q_HEADS), 1.0),
        "w_branch_a": nrm(ks[7], (DEPTH, A_Q_W, D_MODEL), A_Q_W ** -0.5),
        "w_branch_b": nrm(ks[8], (DEPTH, B_OUT_W, D_MODEL), B_OUT_W ** -0.5),
        "w_o": nrm(ks[9], (DEPTH, D_MODEL, D_MODEL), DEEPNORM_BETA * D_MODEL ** -0.5),
        "ln1_g": 1.0 + nrm(ks[10], (DEPTH, D_MODEL), 0.02),
        "ln1_b": nrm(ks[11], (DEPTH, D_MODEL), 0.02),
        "w_gate_up": nrm(ks[12], (DEPTH, D_MODEL, 2 * D_FF), D_MODEL ** -0.5),
        "w_down": nrm(ks[13], (DEPTH, D_FF, D_MODEL), DEEPNORM_BETA * D_FF ** -0.5),
        "ln2_g": 1.0 + nrm(ks[14], (DEPTH, D_MODEL), 0.02),
        "ln2_b": nrm(ks[15], (DEPTH, D_MODEL), 0.02),
    }


def reference(x, c, positions, w_ada, b_ada, w_in, sinks, w_branch_a, w_branch_b, w_o,
              ln1_g, ln1_b, w_gate_up, w_down, ln2_g, ln2_b):
    c_act = jax.nn.silu(c)
    for l in range(DEPTH):
        mod = (jnp.einsum('bd,de->be', c_act, w_ada[l]) + b_ada[l])[:, None, :]
        shift_m, scale_m, gate_m, shift_f, scale_f, gate_f = jnp.split(mod, 6, axis=-1)
        u = x * (1.0 + scale_m) + shift_m
        y = mixer(u, positions, w_in[l], sinks[l], w_branch_a[l], w_branch_b[l], w_o[l])
        x = layer_norm(DEEPNORM_ALPHA * x + (1.0 + gate_m) * y, ln1_g[l], ln1_b[l])
        u = x * (1.0 + scale_f) + shift_f
        y = swiglu(u, w_gate_up[l], w_down[l])
        x = layer_norm(DEEPNORM_ALPHA * x + (1.0 + gate_f) * y, ln2_g[l], ln2_b[l])
    return x
```

```python
import functools

import jax
import jax.numpy as jnp
from jax import lax
from jax.experimental import pallas as pl
from jax.experimental.pallas import tpu as pltpu

F32 = jnp.float32
BF16 = jnp.bfloat16

HEAD_DIM = 64
HALF = HEAD_DIM // 2
BLOCK = 128
LANES = 128
A_Q_HEADS = 16
A_KV_HEADS = 2
A_GROUP = A_Q_HEADS // A_KV_HEADS
A_WINDOW = 128
B_PATTERNS = ((128, 1), (512, 4), (2048, 16))
B_HEADS_PER_GROUP = 8
ROPE_THETA = 10000.0
LN_EPS = 1e-5
NEG_INF = -1e30

TM_IN = 512
TM_POST = 256
FF_CHUNK = 256
VMEM_LIMIT = 56 * 1024 * 1024

_NT = (((1,), (1,)), ((), ()))


def _resident(shape):
    nd = len(shape)
    return pl.BlockSpec(shape, lambda *_: (0,) * nd, pipeline_mode=pl.Buffered(1))


def _ada_kernel(c_ref, w_ref, b_ref, o_ref):
    c = c_ref[...]
    act = (c * jax.nn.sigmoid(c)).astype(BF16)
    o_ref[...] = jnp.dot(act, w_ref[...].astype(BF16), preferred_element_type=F32) + b_ref[...]


def _ada(c, w, b):
    bsz, d = c.shape
    e = w.shape[1]
    tn = d
    return pl.pallas_call(
        _ada_kernel,
        out_shape=jax.ShapeDtypeStruct((bsz, e), F32),
        grid=(e // tn,),
        in_specs=[pl.BlockSpec((bsz, d), lambda j: (0, 0)),
                  pl.BlockSpec((d, tn), lambda j: (0, j)),
                  pl.BlockSpec((1, tn), lambda j: (0, j))],
        out_specs=pl.BlockSpec((bsz, tn), lambda j: (0, j)),
        compiler_params=pltpu.CompilerParams(dimension_semantics=("arbitrary",)),
        name="ada",
    )(c, w, b.reshape(1, e))


def _rope_table_kernel(pos_ref, inv_ref, cos_ref, sin_ref):
    ang = pos_ref[...] * inv_ref[...]
    cos_ref[...] = jnp.cos(ang)
    sin_ref[...] = jnp.sin(ang)


def _rope_tables(positions):
    bsz, t = positions.shape
    n = bsz * t
    per_row = LANES // HALF
    inv = ROPE_THETA ** (-jnp.arange(HALF, dtype=F32) / HALF)
    pos_dense = jnp.repeat(positions.astype(F32).reshape(n // per_row, per_row), HALF, axis=1)
    inv_dense = jnp.tile(inv, per_row).reshape(1, LANES)
    rows = n // per_row
    tr = 1024
    cos_d, sin_d = pl.pallas_call(
        _rope_table_kernel,
        out_shape=[jax.ShapeDtypeStruct((rows, LANES), F32)] * 2,
        grid=(rows // tr,),
        in_specs=[pl.BlockSpec((tr, LANES), lambda i: (i, 0)),
                  pl.BlockSpec((1, LANES), lambda i: (0, 0))],
        out_specs=[pl.BlockSpec((tr, LANES), lambda i: (i, 0))] * 2,
        compiler_params=pltpu.CompilerParams(dimension_semantics=("arbitrary",)),
        name="rope_tab",
    )(pos_dense, inv_dense)
    cos32 = cos_d.reshape(n, HALF)
    sin32 = sin_d.reshape(n, HALF)
    ctab = jnp.tile(cos32, (1, per_row))
    stab = jnp.concatenate([-sin32, sin32] * (per_row // 2), axis=-1)
    return ctab, stab


def _rope_cols(y, ctab, stab, first_half, scale):
    outs = []
    for j in range(y.shape[1] // LANES):
        slab = y[:, j * LANES:(j + 1) * LANES]
        swapped = jnp.where(first_half,
                            pltpu.roll(slab, LANES - HALF, axis=1),
                            pltpu.roll(slab, HALF, axis=1))
        r = slab * ctab + swapped * stab
        if scale != 1.0:
            r = r * scale
        outs.append(r.astype(BF16))
    return outs[0] if len(outs) == 1 else jnp.concatenate(outs, axis=1)


def _inproj_kernel(xn_ref, x4_ref, x16_ref, sc_ref, sh_ref,
                   cn_ref, sn_ref, c4_ref, s4_ref, c16_ref, s16_ref, w_ref,
                   qa_ref, ka_ref, va_ref, b1_ref, b2_ref, b3_ref, *, cols):
    scale1 = 1.0 + sc_ref[...]
    shift = sh_ref[...]
    tm = xn_ref.shape[0]
    d = xn_ref.shape[1]
    lane = lax.broadcasted_iota(jnp.int32, (tm, LANES), 1)
    first_half = (lane & HALF) == 0
    qscale = HEAD_DIM ** -0.5

    def modulate(xv):
        return (xv * scale1 + shift).astype(BF16)

    def proj(u, c0, width):
        return jnp.dot(u, w_ref[:, c0:c0 + width], preferred_element_type=F32)

    un = modulate(xn_ref[...])
    cn, sn = cn_ref[...], sn_ref[...]
    half_q = cols["qa_w"] // 2
    for h in range(2):
        y = proj(un, cols["qa"] + h * half_q, half_q)
        qa_ref[:, h * half_q:(h + 1) * half_q] = _rope_cols(y, cn, sn, first_half, qscale)
    kvw = cols["kva_w"]
    y = proj(un, cols["ka"], 2 * kvw)
    ka_ref[...] = _rope_cols(y[:, :kvw], cn, sn, first_half, 1.0)
    va_ref[...] = y[:, kvw:].astype(BF16)

    gw = cols["bg_w"]

    def group(u, g, ctab, stab, out_ref):
        y = proj(u, cols["qb"] + g * gw, gw)
        out_ref[:, 0:gw] = _rope_cols(y, ctab, stab, first_half, qscale)
        y = proj(u, cols["kb"] + g * gw, gw)
        out_ref[:, gw:2 * gw] = _rope_cols(y, ctab, stab, first_half, 1.0)
        y = proj(u, cols["vb"] + g * gw, gw)
        out_ref[:, 2 * gw:3 * gw] = y.astype(BF16)

    group(un, 0, cn, sn, b1_ref)
    group(modulate(x4_ref[...]), 1, c4_ref[...], s4_ref[...], b2_ref)
    x16 = x16_ref[...]
    nres = x16.shape[1] // d
    u16 = jnp.concatenate([modulate(x16[:, k * d:(k + 1) * d]) for k in range(nres)], axis=0)
    c16 = c16_ref[...]
    s16 = s16_ref[...]
    c16 = jnp.concatenate([c16[:, k * LANES:(k + 1) * LANES] for k in range(nres)], axis=0)
    s16 = jnp.concatenate([s16[:, k * LANES:(k + 1) * LANES] for k in range(nres)], axis=0)
    group(u16, 2, c16, s16, b3_ref)


def _inproj(x, mod3, ctab, stab, w_bf, cols):
    bsz, t, d = x.shape
    tm = TM_IN
    tiles = t // tm
    r4, r16 = B_PATTERNS[1][1], B_PATTERNS[2][1]
    assert tiles == r4 and t // r16 * (r16 // tiles) == tm
    res16 = r16 // tiles
    t16 = t // r16
    x4 = x.reshape(bsz, t // r4, r4 * d)
    x16 = x.reshape(bsz, t16, r16 * d)
    c3 = ctab.reshape(bsz, t, LANES)
    s3 = stab.reshape(bsz, t, LANES)
    c4 = ctab.reshape(bsz, t // r4, r4 * LANES)
    s4 = stab.reshape(bsz, t // r4, r4 * LANES)
    c16 = ctab.reshape(bsz, t16, r16 * LANES)
    s16 = stab.reshape(bsz, t16, r16 * LANES)
    dm = d // d

    def bq(i):
        return i // tiles, i % tiles

    nat = lambda width: pl.BlockSpec((None, tm, width), lambda i: (bq(i)[0], bq(i)[1], 0))
    in_specs = [
        nat(d),
        pl.BlockSpec((None, t // r4, d), lambda i: (bq(i)[0], 0, bq(i)[1])),
        pl.BlockSpec((None, t16, res16 * d), lambda i: (bq(i)[0], 0, bq(i)[1])),
        pl.BlockSpec((None, 1, d), lambda i: (bq(i)[0], 0, 1 * dm)),
        pl.BlockSpec((None, 1, d), lambda i: (bq(i)[0], 0, 0)),
        nat(LANES), nat(LANES),
        pl.BlockSpec((None, t // r4, LANES), lambda i: (bq(i)[0], 0, bq(i)[1])),
        pl.BlockSpec((None, t // r4, LANES), lambda i: (bq(i)[0], 0, bq(i)[1])),
        pl.BlockSpec((None, t16, res16 * LANES), lambda i: (bq(i)[0], 0, bq(i)[1])),
        pl.BlockSpec((None, t16, res16 * LANES), lambda i: (bq(i)[0], 0, bq(i)[1])),
        _resident(w_bf.shape),
    ]
    gw3 = 3 * cols["bg_w"]
    out_shape = [jax.ShapeDtypeStruct((bsz, t, cols["qa_w"]), BF16),
                 jax.ShapeDtypeStruct((bsz, t, cols["kva_w"]), BF16),
                 jax.ShapeDtypeStruct((bsz, t, cols["kva_w"]), BF16),
                 jax.ShapeDtypeStruct((bsz, t, gw3), BF16),
                 jax.ShapeDtypeStruct((bsz, t, gw3), BF16),
                 jax.ShapeDtypeStruct((bsz, t, gw3), BF16)]
    out_specs = [nat(cols["qa_w"]), nat(cols["kva_w"]), nat(cols["kva_w"]),
                 nat(gw3), nat(gw3), nat(gw3)]
    return pl.pallas_call(
        functools.partial(_inproj_kernel, cols=cols),
        out_shape=out_shape,
        grid=(bsz * tiles,),
        in_specs=in_specs,
        out_specs=out_specs,
        compiler_params=pltpu.CompilerParams(dimension_semantics=("arbitrary",),
                                             vmem_limit_bytes=VMEM_LIMIT),
        name="inproj",
    )(x.reshape(bsz, t, d), x4, x16, mod3, mod3, c3, s3, c4, s4, c16, s16, w_bf)


def _softmax_head(s, mask, sink):
    s = jnp.where(mask, s, NEG_INF)
    m = jnp.max(s, axis=-1, keepdims=True)
    if sink is not None:
        m = jnp.maximum(m, sink)
    p = jnp.exp(s - m)
    l = jnp.sum(p, axis=-1, keepdims=True)
    if sink is not None:
        l = l + jnp.exp(sink - m)
    return p.astype(BF16), m, l


def _band_mask(rows, keys, n_back, col_lo):
    qi = lax.broadcasted_iota(jnp.int32, (rows, keys), 0)
    c = lax.broadcasted_iota(jnp.int32, (rows, keys), 1)
    dist = qi + (keys - rows) - c
    return (dist >= 0) & (dist <= n_back) & (c >= col_lo)


def _attn_a_kernel(sink_ref, q_ref, k_ref, v_ref, o_ref, klo, khi, vlo, vhi):
    t = q_ref.shape[0]
    nblk = t // BLOCK
    lane = lax.broadcasted_iota(jnp.int32, (1, LANES), 1)
    mlo = (lane < HEAD_DIM).astype(F32)
    mhi = 1.0 - mlo
    lane_lo = lax.broadcasted_iota(jnp.int32, (BLOCK, LANES), 1) < HEAD_DIM

    for src, lo_ref, hi_ref in ((k_ref, klo, khi), (v_ref, vlo, vhi)):
        val = src[...].astype(F32)
        swp = pltpu.roll(val, HEAD_DIM, axis=1)
        lo_ref[0] = (val * mlo).astype(BF16)
        hi_ref[0] = (swp * mhi).astype(BF16)
        lo_ref[1] = (swp * mlo).astype(BF16)
        hi_ref[1] = (val * mhi).astype(BF16)

    def block(i, carry):
        r0 = pl.multiple_of(i * BLOCK, BLOCK)
        p0 = pl.multiple_of(jnp.maximum(i - 1, 0) * BLOCK, BLOCK)
        col_lo = jnp.where(i > 0, 0, BLOCK)
        mask = _band_mask(BLOCK, 2 * BLOCK, A_WINDOW - 1, col_lo)
        for hk in range(A_KV_HEADS):
            def cat(ref):
                return jnp.concatenate([ref[hk, pl.ds(p0, BLOCK), :], ref[hk, pl.ds(r0, BLOCK), :]], axis=0)
            k_lo, k_hi, v_lo, v_hi = cat(klo), cat(khi), cat(vlo), cat(vhi)
            for pr in range(A_GROUP // 2):
                slab = hk * (A_GROUP // 2) + pr
                q = q_ref[pl.ds(r0, BLOCK), slab * LANES:(slab + 1) * LANES]
                h0 = 2 * slab
                s_e = lax.dot_general(q, k_lo, _NT, preferred_element_type=F32)
                s_o = lax.dot_general(q, k_hi, _NT, preferred_element_type=F32)
                p_e, _, l_e = _softmax_head(s_e, mask, sink_ref[h0])
                p_o, _, l_o = _softmax_head(s_o, mask, sink_ref[h0 + 1])
                acc = (jnp.dot(p_e, v_lo, preferred_element_type=F32)
                       + jnp.dot(p_o, v_hi, preferred_element_type=F32))
                inv_l = jnp.where(lane_lo, 1.0 / l_e, 1.0 / l_o)
                o_ref[pl.ds(r0, BLOCK), slab * LANES:(slab + 1) * LANES] = (acc * inv_l).astype(BF16)
        return carry

    lax.fori_loop(0, nblk, block, 0)


def _attn_a(sinks, qa, ka, va):
    bsz, t, qw = qa.shape
    kw = ka.shape[2]
    seq = lambda width: pl.BlockSpec((None, t, width), lambda b: (b, 0, 0))
    return pl.pallas_call(
        _attn_a_kernel,
        out_shape=jax.ShapeDtypeStruct((bsz, t, qw), BF16),
        grid=(bsz,),
        in_specs=[pl.BlockSpec(memory_space=pltpu.SMEM), seq(qw), seq(kw), seq(kw)],
        out_specs=seq(qw),
        scratch_shapes=[pltpu.VMEM((A_KV_HEADS, t, LANES), BF16)] * 4,
        compiler_params=pltpu.CompilerParams(dimension_semantics=("arbitrary",),
                                             vmem_limit_bytes=VMEM_LIMIT),
        name="attn_a",
    )(sinks, qa, ka, va)


def _pair_attention(q, kc, vc, mask, mlo, mhi, lane_lo):
    s_e = lax.dot_general(q * mlo, kc, _NT, preferred_element_type=F32)
    s_o = lax.dot_general(q * mhi, kc, _NT, preferred_element_type=F32)
    p_e, m_e, l_e = _softmax_head(s_e, mask, None)
    p_o, m_o, l_o = _softmax_head(s_o, mask, None)
    acc = (jnp.dot(p_e, vc * mlo, preferred_element_type=F32)
           + jnp.dot(p_o, vc * mhi, preferred_element_type=F32))
    out = acc * jnp.where(lane_lo, 1.0 / l_e, 1.0 / l_o)
    lse = jnp.where(lane_lo, m_e + jnp.log(l_e), m_o + jnp.log(l_o))
    return out, lse


def _attn_b_kernel(q1_ref, k1_ref, v1_ref, q2_ref, k2_ref, v2_ref, q3_ref, k3_ref, v3_ref,
                   o_ref, o2_s, l2_s, o3_s, l3_s):
    t = q1_ref.shape[0]
    lane = lax.broadcasted_iota(jnp.int32, (1, LANES), 1)
    mlo = (lane < HEAD_DIM).astype(BF16)
    mhi = (lane >= HEAD_DIM).astype(BF16)
    lane_lo = lax.broadcasted_iota(jnp.int32, (BLOCK, LANES), 1) < HEAD_DIM
    pair = functools.partial(_pair_attention, mlo=mlo, mhi=mhi, lane_lo=lane_lo)

    def banded(q_ref, k_ref, v_ref, r0, p0, has_prev, n_back):
        col_lo = jnp.where(has_prev, 0, BLOCK)
        mask = _band_mask(BLOCK, 2 * BLOCK, n_back, col_lo)
        kc = jnp.concatenate([k_ref[pl.ds(p0, BLOCK), :], k_ref[pl.ds(r0, BLOCK), :]], axis=0)
        vc = jnp.concatenate([v_ref[pl.ds(p0, BLOCK), :], v_ref[pl.ds(r0, BLOCK), :]], axis=0)
        return pair(q_ref[pl.ds(r0, BLOCK), :], kc, vc, mask)

    (_, r16), (w4, r4), (w1, _) = B_PATTERNS[2], B_PATTERNS[1], B_PATTERNS[0]
    t16 = t // r16
    assert t16 == BLOCK

    def g3(rho, carry):
        r0 = pl.multiple_of(rho * BLOCK, BLOCK)
        mask = _band_mask(BLOCK, BLOCK, B_PATTERNS[2][0] // r16, 0)
        out, lse = pair(q3_ref[pl.ds(r0, BLOCK), :], k3_ref[pl.ds(r0, BLOCK), :],
                        v3_ref[pl.ds(r0, BLOCK), :], mask)
        o3_s[pl.ds(rho, BLOCK, stride=r16), :] = out
        l3_s[pl.ds(rho, BLOCK, stride=r16), :] = lse
        return carry

    lax.fori_loop(0, r16, g3, 0)

    nb4 = t // r4 // BLOCK

    def g2(idx, carry):
        rho = idx // nb4
        sb = idx % nb4
        r0 = pl.multiple_of(idx * BLOCK, BLOCK)
        p0 = pl.multiple_of(jnp.maximum(idx - 1, 0) * BLOCK, BLOCK)
        out, lse = banded(q2_ref, k2_ref, v2_ref, r0, p0, sb > 0, w4 // r4)
        start = sb * (BLOCK * r4) + rho
        o2_s[pl.ds(start, BLOCK, stride=r4), :] = out
        l2_s[pl.ds(start, BLOCK, stride=r4), :] = lse
        return carry

    lax.fori_loop(0, r4 * nb4, g2, 0)

    def g1(i, carry):
        r0 = pl.multiple_of(i * BLOCK, BLOCK)
        p0 = pl.multiple_of(jnp.maximum(i - 1, 0) * BLOCK, BLOCK)
        o1, l1 = banded(q1_ref, k1_ref, v1_ref, r0, p0, i > 0, w1)
        o2, l2 = o2_s[pl.ds(r0, BLOCK), :], l2_s[pl.ds(r0, BLOCK), :]
        o3, l3 = o3_s[pl.ds(r0, BLOCK), :], l3_s[pl.ds(r0, BLOCK), :]
        mx = jnp.maximum(jnp.maximum(l1, l2), l3)
        e1, e2, e3 = jnp.exp(l1 - mx), jnp.exp(l2 - mx), jnp.exp(l3 - mx)
        merged = (e1 * o1 + e2 * o2 + e3 * o3) / (e1 + e2 + e3)
        o_ref[pl.ds(r0, BLOCK), :] = merged.astype(BF16)
        return carry

    lax.fori_loop(0, t // BLOCK, g1, 0)


def _attn_b(b1, b2, b3):
    bsz, t, w3 = b1.shape
    gw = w3 // 3
    npair = gw // LANES

    def spec(part):
        return pl.BlockSpec((None, t, LANES), lambda b, j: (b, 0, part * npair + j))

    return pl.pallas_call(
        _attn_b_kernel,
        out_shape=jax.ShapeDtypeStruct((bsz, t, gw), BF16),
        grid=(bsz, npair),
        in_specs=[spec(0), spec(1), spec(2)] * 3,
        out_specs=pl.BlockSpec((None, t, LANES), lambda b, j: (b, 0, j)),
        scratch_shapes=[pltpu.VMEM((t, LANES), F32)] * 4,
        compiler_params=pltpu.CompilerParams(dimension_semantics=("arbitrary", "arbitrary"),
                                             vmem_limit_bytes=VMEM_LIMIT),
        name="attn_b",
    )(b1, b1, b1, b2, b2, b2, b3, b3, b3)


def _layer_norm(h, g, b):
    mu = jnp.mean(h, axis=-1, keepdims=True)
    xc = h - mu
    var = jnp.mean(xc * xc, axis=-1, keepdims=True)
    return xc * lax.rsqrt(var + LN_EPS) * g + b


def _post_kernel(x_ref, oa_ref, ob_ref, scm_ref, shm_ref, gm_ref, shf_ref, scf_ref, gf_ref,
                 g1_ref, b1_ref, g2_ref, b2_ref,
                 wg_ref, wa_ref, wb_ref, wo_ref, wgu_ref, wd_ref, o_ref, *, alpha, d_ff):
    x = x_ref[...]
    d = x.shape[1]
    u = (x * (1.0 + scm_ref[...]) + shm_ref[...]).astype(BF16)
    ga = jax.nn.sigmoid(jnp.dot(u, wg_ref[:, :d], preferred_element_type=F32))
    ya = jnp.dot(oa_ref[...], wa_ref[...], preferred_element_type=F32)
    merged = ga * ya
    gb = jax.nn.sigmoid(jnp.dot(u, wg_ref[:, d:], preferred_element_type=F32))
    yb = jnp.dot(ob_ref[...], wb_ref[...], preferred_element_type=F32)
    merged = (merged + gb * yb).astype(BF16)
    y = jnp.dot(merged, wo_ref[...], preferred_element_type=F32)
    x1 = _layer_norm(alpha * x + (1.0 + gm_ref[...]) * y, g1_ref[...], b1_ref[...])

    u2 = (x1 * (1.0 + scf_ref[...]) + shf_ref[...]).astype(BF16)
    acc = jnp.zeros_like(x1)
    for c in range(d_ff // FF_CHUNK):
        c0 = c * FF_CHUNK
        hg = jnp.dot(u2, wgu_ref[:, c0:c0 + FF_CHUNK], preferred_element_type=F32)
        hu = jnp.dot(u2, wgu_ref[:, d_ff + c0:d_ff + c0 + FF_CHUNK], preferred_element_type=F32)
        act = (hg * jax.nn.sigmoid(hg) * hu).astype(BF16)
        acc = acc + jnp.dot(act, wd_ref[c0:c0 + FF_CHUNK, :], preferred_element_type=F32)
    o_ref[...] = _layer_norm(alpha * x1 + (1.0 + gf_ref[...]) * acc, g2_ref[...], b2_ref[...])


def _post(x, oa, ob, mod3, ln1_g, ln1_b, ln2_g, ln2_b, wg, wa, wb, wo, wgu, wd, alpha):
    bsz, t, d = x.shape
    tm = TM_POST
    tiles = t // tm
    d_ff = wd.shape[0]

    def bq(i):
        return i // tiles, i % tiles

    nat = lambda width: pl.BlockSpec((None, tm, width), lambda i: (bq(i)[0], bq(i)[1], 0))
    modspec = lambda k: pl.BlockSpec((None, 1, d), lambda i: (bq(i)[0], 0, k))
    vec = pl.BlockSpec((1, d), lambda i: (0, 0))
    in_specs = [nat(d), nat(oa.shape[2]), nat(ob.shape[2]),
                modspec(1), modspec(0), modspec(2), modspec(3), modspec(4), modspec(5),
                vec, vec, vec, vec,
                _resident(wg.shape), _resident(wa.shape), _resident(wb.shape),
                _resident(wo.shape), _resident(wgu.shape), _resident(wd.shape)]
    row = lambda v: v.reshape(1, d)
    return pl.pallas_call(
        functools.partial(_post_kernel, alpha=alpha, d_ff=d_ff),
        out_shape=jax.ShapeDtypeStruct((bsz, t, d), F32),
        grid=(bsz * tiles,),
        in_specs=in_specs,
        out_specs=nat(d),
        compiler_params=pltpu.CompilerParams(dimension_semantics=("arbitrary",),
                                             vmem_limit_bytes=VMEM_LIMIT),
        name="post",
    )(x, oa, ob, mod3, mod3, mod3, mod3, mod3, mod3,
      row(ln1_g), row(ln1_b), row(ln2_g), row(ln2_b), wg, wa, wb, wo, wgu, wd)


def kernel(x, c, positions, w_ada, b_ada, w_in, sinks, w_branch_a, w_branch_b, w_o,
           ln1_g, ln1_b, w_gate_up, w_down, ln2_g, ln2_b):
    depth = w_ada.shape[0]
    bsz, t, d = x.shape
    alpha = (2 * depth) ** 0.25
    a_q_w = A_Q_HEADS * HEAD_DIM
    a_kv_w = A_KV_HEADS * HEAD_DIM
    b_w = B_HEADS_PER_GROUP * len(B_PATTERNS) * HEAD_DIM
    cols = {"qa": 0, "qa_w": a_q_w, "ka": a_q_w, "kva_w": a_kv_w,
            "qb": a_q_w + 2 * a_kv_w, "kb": a_q_w + 2 * a_kv_w + b_w,
            "vb": a_q_w + 2 * a_kv_w + 2 * b_w, "bg_w": B_HEADS_PER_GROUP * HEAD_DIM}
    gate0 = a_q_w + 2 * a_kv_w + 3 * b_w

    c_act_in = c
    ctab, stab = _rope_tables(positions)
    for l in range(depth):
        mod = _ada(c_act_in, w_ada[l], b_ada[l])
        mod3 = mod.reshape(bsz, 1, 6 * d)
        w_bf = w_in[l].astype(BF16)
        qa, ka, va, b1, b2, b3 = _inproj(x, mod3, ctab, stab, w_bf[:, :gate0], cols)
        oa = _attn_a(sinks[l], qa, ka, va)
        ob = _attn_b(b1, b2, b3)
        x = _post(x, oa, ob, mod3, ln1_g[l], ln1_b[l], ln2_g[l], ln2_b[l],
                  w_bf[:, gate0:], w_branch_a[l].astype(BF16), w_branch_b[l].astype(BF16),
                  w_o[l].astype(BF16), w_gate_up[l].astype(BF16), w_down[l].astype(BF16), alpha)
    return x
```

```python
import functools

import jax
import jax.numpy as jnp
from jax import lax
from jax.experimental import pallas as pl
from jax.experimental.pallas import tpu as pltpu

F32 = jnp.float32
BF16 = jnp.bfloat16

HEAD_DIM = 64
HALF = HEAD_DIM // 2
BLOCK = 128
LANES = 128
A_Q_HEADS = 16
A_KV_HEADS = 2
A_GROUP = A_Q_HEADS // A_KV_HEADS
A_WINDOW = 128
B_PATTERNS = ((128, 1), (512, 4), (2048, 16))
B_HEADS_PER_GROUP = 8
ROPE_THETA = 10000.0
LN_EPS = 1e-5
NEG_INF = -1e30

TM_IN = 512
TM_POST = 256
FF_CHUNK = 256
BLOCKS_PER_ITER = 8
VMEM_LIMIT = 56 * 1024 * 1024

_NT = (((1,), (1,)), ((), ()))


def _resident(shape):
    nd = len(shape)
    return pl.BlockSpec(shape, lambda *_: (0,) * nd, pipeline_mode=pl.Buffered(1))


def _ada_kernel(c_ref, w_ref, b_ref, o_ref):
    c = c_ref[...]
    act = (c * jax.nn.sigmoid(c)).astype(BF16)
    o_ref[...] = jnp.dot(act, w_ref[...].astype(BF16), preferred_element_type=F32) + b_ref[...]


def _ada(c, w, b):
    bsz, d = c.shape
    e = w.shape[1]
    tn = d
    return pl.pallas_call(
        _ada_kernel,
        out_shape=jax.ShapeDtypeStruct((bsz, e), F32),
        grid=(e // tn,),
        in_specs=[pl.BlockSpec((bsz, d), lambda j: (0, 0)),
                  pl.BlockSpec((d, tn), lambda j: (0, j)),
                  pl.BlockSpec((1, tn), lambda j: (0, j))],
        out_specs=pl.BlockSpec((bsz, tn), lambda j: (0, j)),
        compiler_params=pltpu.CompilerParams(dimension_semantics=("arbitrary",)),
        name="ada",
    )(c, w, b.reshape(1, e))


def _rope_table_kernel(pos_ref, inv_ref, c_ref, s_ref):
    ang = pos_ref[...] * inv_ref[...]
    cos = jnp.cos(ang)
    sin = jnp.sin(ang)
    reps = LANES // HEAD_DIM
    c_ref[...] = jnp.concatenate([cos, cos] * reps, axis=0).T
    s_ref[...] = jnp.concatenate([-sin, sin] * reps, axis=0).T


def _rope_tables(positions):
    bsz, t = positions.shape
    inv = ROPE_THETA ** (-jnp.arange(HALF, dtype=F32) / HALF)
    tab = pl.BlockSpec((None, t, LANES), lambda b: (b, 0, 0))
    return pl.pallas_call(
        _rope_table_kernel,
        out_shape=[jax.ShapeDtypeStruct((bsz, t, LANES), F32)] * 2,
        grid=(bsz,),
        in_specs=[pl.BlockSpec((None, 1, t), lambda b: (b, 0, 0)),
                  pl.BlockSpec((HALF, 1), lambda b: (0, 0))],
        out_specs=[tab, tab],
        compiler_params=pltpu.CompilerParams(dimension_semantics=("arbitrary",)),
        name="rope_tab",
    )(positions.astype(F32).reshape(bsz, 1, t), inv.reshape(HALF, 1))


def _rope_cols(y, ctab, stab, first_half, scale):
    outs = []
    for j in range(y.shape[1] // LANES):
        slab = y[:, j * LANES:(j + 1) * LANES]
        swapped = jnp.where(first_half,
                            pltpu.roll(slab, LANES - HALF, axis=1),
                            pltpu.roll(slab, HALF, axis=1))
        r = slab * ctab + swapped * stab
        if scale != 1.0:
            r = r * scale
        outs.append(r.astype(BF16))
    return outs[0] if len(outs) == 1 else jnp.concatenate(outs, axis=1)


def _inproj_kernel(x_ref, sc_ref, sh_ref, cn_ref, sn_ref, w_ref,
                   qa_ref, ka_ref, va_ref, b1_ref, b2_ref, b3_ref, us_ref, *, cols):
    tm, d = x_ref.shape
    nslab = d // LANES
    lane = lax.broadcasted_iota(jnp.int32, (tm, LANES), 1)
    first_half = (lane & HALF) == 0
    qscale = HEAD_DIM ** -0.5

    uf = x_ref[...] * (1.0 + sc_ref[...]) + sh_ref[...]
    un = uf.astype(BF16)
    for j in range(nslab):
        us_ref[j] = uf[:, j * LANES:(j + 1) * LANES]

    def strided_rows(load, r):
        per = tm // r
        return jnp.concatenate([load(rho, per, r) for rho in range(r)], axis=0)

    def u_dilated(r):
        def load(rho, per, stride):
            return jnp.concatenate(
                [us_ref[j, pl.ds(rho, per, stride=stride), :] for j in range(nslab)], axis=1)
        return strided_rows(load, r).astype(BF16)

    def tab_dilated(ref, r):
        return strided_rows(lambda rho, per, stride: ref[pl.ds(rho, per, stride=stride), :], r)

    def proj(u, c0, width):
        return jnp.dot(u, w_ref[:, c0:c0 + width], preferred_element_type=F32)

    cn, sn = cn_ref[...], sn_ref[...]
    half_q = cols["qa_w"] // 2
    for h in range(2):
        y = proj(un, cols["qa"] + h * half_q, half_q)
        qa_ref[:, h * half_q:(h + 1) * half_q] = _rope_cols(y, cn, sn, first_half, qscale)
    kvw = cols["kva_w"]
    y = proj(un, cols["ka"], 2 * kvw)
    ka_ref[...] = _rope_cols(y[:, :kvw], cn, sn, first_half, 1.0)
    va_ref[...] = y[:, kvw:].astype(BF16)

    gw = cols["bg_w"]

    def group(u, g, ctab, stab, store):
        y = proj(u, cols["qb"] + g * gw, gw)
        store(0, _rope_cols(y, ctab, stab, first_half, qscale))
        y = proj(u, cols["kb"] + g * gw, gw)
        store(gw, _rope_cols(y, ctab, stab, first_half, 1.0))
        y = proj(u, cols["vb"] + g * gw, gw)
        store(2 * gw, y.astype(BF16))

    def store_nat(c0, val):
        b1_ref[:, c0:c0 + gw] = val

    def store_res(ref):
        nres, per = ref.shape[0], ref.shape[1]

        def store(c0, val):
            for rho in range(nres):
                ref[rho, :, c0:c0 + gw] = val[rho * per:(rho + 1) * per]
        return store

    group(un, 0, cn, sn, store_nat)
    for g, out_ref in ((1, b2_ref), (2, b3_ref)):
        r = B_PATTERNS[g][1]
        group(u_dilated(r), g, tab_dilated(cn_ref, r), tab_dilated(sn_ref, r), store_res(out_ref))


def _inproj(x, mod3, ctab, stab, w_bf, cols):
    bsz, t, d = x.shape
    tm = TM_IN
    tiles = t // tm
    r4, r16 = B_PATTERNS[1][1], B_PATTERNS[2][1]

    def bq(i):
        return i // tiles, i % tiles

    nat = lambda width: pl.BlockSpec((None, tm, width), lambda i: (bq(i)[0], bq(i)[1], 0))
    res = lambda r, width: pl.BlockSpec((None, r, tm // r, width), lambda i: (bq(i)[0], 0, bq(i)[1], 0))
    in_specs = [
        nat(d),
        pl.BlockSpec((None, 1, d), lambda i: (bq(i)[0], 0, 1)),
        pl.BlockSpec((None, 1, d), lambda i: (bq(i)[0], 0, 0)),
        nat(LANES), nat(LANES),
        _resident(w_bf.shape),
    ]
    gw3 = 3 * cols["bg_w"]
    out_shape = [jax.ShapeDtypeStruct((bsz, t, cols["qa_w"]), BF16),
                 jax.ShapeDtypeStruct((bsz, t, cols["kva_w"]), BF16),
                 jax.ShapeDtypeStruct((bsz, t, cols["kva_w"]), BF16),
                 jax.ShapeDtypeStruct((bsz, t, gw3), BF16),
                 jax.ShapeDtypeStruct((bsz, r4, t // r4, gw3), BF16),
                 jax.ShapeDtypeStruct((bsz, r16, t // r16, gw3), BF16)]
    out_specs = [nat(cols["qa_w"]), nat(cols["kva_w"]), nat(cols["kva_w"]),
                 nat(gw3), res(r4, gw3), res(r16, gw3)]
    qa, ka, va, b1, b2, b3 = pl.pallas_call(
        functools.partial(_inproj_kernel, cols=cols),
        out_shape=out_shape,
        grid=(bsz * tiles,),
        in_specs=in_specs,
        out_specs=out_specs,
        scratch_shapes=[pltpu.VMEM((d // LANES, tm, LANES), F32)],
        compiler_params=pltpu.CompilerParams(dimension_semantics=("arbitrary",),
                                             vmem_limit_bytes=VMEM_LIMIT),
        name="inproj",
    )(x, mod3, mod3, ctab, stab, w_bf)
    return qa, ka, va, b1, b2.reshape(bsz, t, gw3), b3.reshape(bsz, t, gw3)


def _band_bias(keys, n_back, prev_valid):
    qi = lax.broadcasted_iota(jnp.int32, (BLOCK, keys), 0)
    c = lax.broadcasted_iota(jnp.int32, (BLOCK, keys), 1)
    dist = qi + (keys - BLOCK) - c
    ok = (dist >= 0) & (dist <= n_back)
    if not prev_valid:
        ok = ok & (c >= keys - BLOCK)
    return jnp.where(ok, 0.0, NEG_INF).astype(F32)


def _lane_consts():
    lane = lax.broadcasted_iota(jnp.int32, (1, LANES), 1)
    mlo = (lane < HEAD_DIM).astype(BF16)
    mhi = (lane >= HEAD_DIM).astype(BF16)
    lane_lo = lax.broadcasted_iota(jnp.int32, (BLOCK, LANES), 1) < HEAD_DIM
    return mlo, mhi, lane_lo


def _pair_block(q, kc, vc, bias, mlo, mhi, lane_lo, sinks=None, want_lse=True):
    q2 = jnp.concatenate([q * mlo, q * mhi], axis=0)
    s = lax.dot_general(q2, kc, _NT, preferred_element_type=F32) + bias
    halves = (s[:BLOCK], s[BLOCK:])
    ms = [jnp.max(h, axis=-1, keepdims=True) for h in halves]
    if sinks is not None:
        ms = [jnp.maximum(m, sk) for m, sk in zip(ms, sinks)]
    p = jnp.concatenate([jnp.exp(h - m) for h, m in zip(halves, ms)], axis=0).astype(BF16)
    ones = jnp.ones(vc.shape, BF16)
    r = jnp.dot(p, jnp.concatenate([vc, ones], axis=1), preferred_element_type=F32)
    num = jnp.where(lane_lo, r[:BLOCK, :LANES], r[BLOCK:, :LANES])
    den = jnp.where(lane_lo, r[:BLOCK, LANES:], r[BLOCK:, LANES:])
    if sinks is not None:
        den = den + jnp.where(lane_lo, jnp.exp(sinks[0] - ms[0]), jnp.exp(sinks[1] - ms[1]))
    out = num / den
    if not want_lse:
        return out
    return out, jnp.where(lane_lo, ms[0], ms[1]) + jnp.log(den)


def _attn_a_kernel(sink_ref, q_ref, k_ref, v_ref, o_ref, kd_ref, vd_ref, bias_ref):
    t = q_ref.shape[0]
    nblk = t // BLOCK
    nslab = A_GROUP // 2
    mlo, mhi, lane_lo = _lane_consts()
    lane_lo_t = lax.broadcasted_iota(jnp.int32, (t, LANES), 1) < HEAD_DIM

    for src, dup in ((k_ref, kd_ref), (v_ref, vd_ref)):
        val = src[...].astype(F32)
        swp = pltpu.roll(val, HEAD_DIM, axis=1)
        dup[0] = jnp.where(lane_lo_t, val, swp).astype(BF16)
        dup[1] = jnp.where(lane_lo_t, swp, val).astype(BF16)
    twice = lambda b: jnp.concatenate([b, b], axis=0)
    bias_ref[0] = twice(_band_bias(2 * BLOCK, A_WINDOW - 1, False))
    bias_ref[1] = twice(_band_bias(2 * BLOCK, A_WINDOW - 1, True))

    def block(i, carry):
        r0 = pl.multiple_of(i * BLOCK, BLOCK)
        p0 = pl.multiple_of(jnp.maximum(i - 1, 0) * BLOCK, BLOCK)
        bias = bias_ref[jnp.minimum(i, 1)]
        for hk in range(A_KV_HEADS):
            kc = jnp.concatenate([kd_ref[hk, pl.ds(p0, BLOCK), :], kd_ref[hk, pl.ds(r0, BLOCK), :]], axis=0)
            vc = jnp.concatenate([vd_ref[hk, pl.ds(p0, BLOCK), :], vd_ref[hk, pl.ds(r0, BLOCK), :]], axis=0)
            for sl in range(nslab):
                slab = hk * nslab + sl
                cols = slice(slab * LANES, (slab + 1) * LANES)
                out = _pair_block(q_ref[pl.ds(r0, BLOCK), cols], kc, vc, bias, mlo, mhi, lane_lo,
                                  sinks=(sink_ref[2 * slab], sink_ref[2 * slab + 1]), want_lse=False)
                o_ref[pl.ds(r0, BLOCK), cols] = out.astype(BF16)
        return carry

    lax.fori_loop(0, nblk, block, 0)


def _attn_a(sinks, qa, ka, va):
    bsz, t, qw = qa.shape
    kw = ka.shape[2]
    seq = lambda width: pl.BlockSpec((None, t, width), lambda b: (b, 0, 0))
    return pl.pallas_call(
        _attn_a_kernel,
        out_shape=jax.ShapeDtypeStruct((bsz, t, qw), BF16),
        grid=(bsz,),
        in_specs=[pl.BlockSpec(memory_space=pltpu.SMEM), seq(qw), seq(kw), seq(kw)],
        out_specs=seq(qw),
        scratch_shapes=[pltpu.VMEM((A_KV_HEADS, t, LANES), BF16),
                        pltpu.VMEM((A_KV_HEADS, t, LANES), BF16),
                        pltpu.VMEM((2, 2 * BLOCK, 2 * BLOCK), F32)],
        compiler_params=pltpu.CompilerParams(dimension_semantics=("arbitrary",),
                                             vmem_limit_bytes=VMEM_LIMIT),
        name="attn_a",
    )(sinks, qa, ka, va)


def _attn_b_kernel(q1_ref, k1_ref, v1_ref, q2_ref, k2_ref, v2_ref, q3_ref, k3_ref, v3_ref,
                   o_ref, o2_s, l2_s, o3_s, l3_s, bias_ref, bias1_ref):
    t = q1_ref.shape[0]
    mlo, mhi, lane_lo = _lane_consts()
    pair = functools.partial(_pair_block, mlo=mlo, mhi=mhi, lane_lo=lane_lo)
    (w1, _), (w4, r4), (w16, r16) = B_PATTERNS
    nb = BLOCKS_PER_ITER
    nb4 = t // r4 // BLOCK
    assert w1 == w4 // r4 == w16 // r16 and t // r16 == BLOCK and nb % nb4 == 0

    twice = lambda b: jnp.concatenate([b, b], axis=0)
    bias_ref[0] = twice(_band_bias(2 * BLOCK, w1, False))
    bias_ref[1] = twice(_band_bias(2 * BLOCK, w1, True))
    bias1_ref[...] = twice(_band_bias(BLOCK, w1, True))

    def first_block(q_ref, k_ref, v_ref, r0):
        return pair(q_ref[pl.ds(r0, BLOCK), :], k_ref[pl.ds(r0, BLOCK), :],
                    v_ref[pl.ds(r0, BLOCK), :], bias1_ref[...])

    def banded(q_ref, k_ref, v_ref, r0, bias):
        p0 = pl.multiple_of(r0 - BLOCK, BLOCK)
        return pair(q_ref[pl.ds(r0, BLOCK), :], k_ref[pl.ds(p0, 2 * BLOCK), :],
                    v_ref[pl.ds(p0, 2 * BLOCK), :], bias)

    def g3(it, carry):
        for u in range(nb):
            rho = it * nb + u
            out, lse = first_block(q3_ref, k3_ref, v3_ref, pl.multiple_of(rho * BLOCK, BLOCK))
            o3_s[pl.ds(rho, BLOCK, stride=r16), :] = out
            l3_s[pl.ds(rho, BLOCK, stride=r16), :] = lse
        return carry

    lax.fori_loop(0, r16 // nb, g3, 0)

    def g2(it, carry):
        base = pl.multiple_of(it * (nb * BLOCK), nb * BLOCK)
        for u in range(nb):
            rho = it * (nb // nb4) + u // nb4
            sb = u % nb4
            r0 = pl.multiple_of(base + u * BLOCK, BLOCK)
            if sb == 0:
                out, lse = first_block(q2_ref, k2_ref, v2_ref, r0)
            else:
                out, lse = banded(q2_ref, k2_ref, v2_ref, r0, bias_ref[1])
            start = sb * (BLOCK * r4) + rho
            o2_s[pl.ds(start, BLOCK, stride=r4), :] = out
            l2_s[pl.ds(start, BLOCK, stride=r4), :] = lse
        return carry

    lax.fori_loop(0, r4 * nb4 // nb, g2, 0)

    def g1(it, carry):
        base = pl.multiple_of(it * (nb * BLOCK), nb * BLOCK)
        for u in range(nb):
            r0 = pl.multiple_of(base + u * BLOCK, BLOCK)
            if u == 0:
                p0 = pl.multiple_of(jnp.maximum(r0 - BLOCK, 0), BLOCK)
                o1, l1 = pair(q1_ref[pl.ds(r0, BLOCK), :],
                              jnp.concatenate([k1_ref[pl.ds(p0, BLOCK), :], k1_ref[pl.ds(r0, BLOCK), :]], axis=0),
                              jnp.concatenate([v1_ref[pl.ds(p0, BLOCK), :], v1_ref[pl.ds(r0, BLOCK), :]], axis=0),
                              bias_ref[jnp.minimum(it, 1)])
            else:
                o1, l1 = banded(q1_ref, k1_ref, v1_ref, r0, bias_ref[1])
            o2, l2 = o2_s[pl.ds(r0, BLOCK), :], l2_s[pl.ds(r0, BLOCK), :]
            o3, l3 = o3_s[pl.ds(r0, BLOCK), :], l3_s[pl.ds(r0, BLOCK), :]
            mx = jnp.maximum(jnp.maximum(l1, l2), l3)
            e1, e2, e3 = jnp.exp(l1 - mx), jnp.exp(l2 - mx), jnp.exp(l3 - mx)
            merged = (e1 * o1 + e2 * o2 + e3 * o3) / (e1 + e2 + e3)
            o_ref[pl.ds(r0, BLOCK), :] = merged.astype(BF16)
        return carry

    lax.fori_loop(0, t // (nb * BLOCK), g1, 0)


def _attn_b(b1, b2, b3):
    bsz, t, w3 = b1.shape
    gw = w3 // 3
    npair = gw // LANES

    def spec(part):
        return pl.BlockSpec((None, t, LANES), lambda b, j: (b, 0, part * npair + j))

    return pl.pallas_call(
        _attn_b_kernel,
        out_shape=jax.ShapeDtypeStruct((bsz, t, gw), BF16),
        grid=(bsz, npair),
        in_specs=[spec(0), spec(1), spec(2)] * 3,
        out_specs=pl.BlockSpec((None, t, LANES), lambda b, j: (b, 0, j)),
        scratch_shapes=[pltpu.VMEM((t, LANES), F32)] * 4
                       + [pltpu.VMEM((2, 2 * BLOCK, 2 * BLOCK), F32),
                          pltpu.VMEM((2 * BLOCK, BLOCK), F32)],
        compiler_params=pltpu.CompilerParams(dimension_semantics=("arbitrary", "arbitrary"),
                                             vmem_limit_bytes=VMEM_LIMIT),
        name="attn_b",
    )(b1, b1, b1, b2, b2, b2, b3, b3, b3)


def _layer_norm(h, g, b):
    mu = jnp.mean(h, axis=-1, keepdims=True)
    xc = h - mu
    var = jnp.mean(xc * xc, axis=-1, keepdims=True)
    return xc * lax.rsqrt(var + LN_EPS) * g + b


def _post_kernel(x_ref, oa_ref, ob_ref, scm_ref, shm_ref, gm_ref, shf_ref, scf_ref, gf_ref,
                 g1_ref, b1_ref, g2_ref, b2_ref,
                 wg_ref, wa_ref, wb_ref, wo_ref, wgu_ref, wd_ref, o_ref, *, alpha, d_ff):
    x = x_ref[...]
    d = x.shape[1]
    u = (x * (1.0 + scm_ref[...]) + shm_ref[...]).astype(BF16)
    ga = jax.nn.sigmoid(jnp.dot(u, wg_ref[:, :d], preferred_element_type=F32))
    ya = jnp.dot(oa_ref[...], wa_ref[...], preferred_element_type=F32)
    merged = ga * ya
    gb = jax.nn.sigmoid(jnp.dot(u, wg_ref[:, d:], preferred_element_type=F32))
    yb = jnp.dot(ob_ref[...], wb_ref[...], preferred_element_type=F32)
    merged = (merged + gb * yb).astype(BF16)
    y = jnp.dot(merged, wo_ref[...], preferred_element_type=F32)
    x1 = _layer_norm(alpha * x + (1.0 + gm_ref[...]) * y, g1_ref[...], b1_ref[...])

    u2 = (x1 * (1.0 + scf_ref[...]) + shf_ref[...]).astype(BF16)
    acc = jnp.zeros_like(x1)
    for c in range(d_ff // FF_CHUNK):
        c0 = c * FF_CHUNK
        hg = jnp.dot(u2, wgu_ref[:, c0:c0 + FF_CHUNK], preferred_element_type=F32)
        hu = jnp.dot(u2, wgu_ref[:, d_ff + c0:d_ff + c0 + FF_CHUNK], preferred_element_type=F32)
        act = (hg * jax.nn.sigmoid(hg) * hu).astype(BF16)
        acc = acc + jnp.dot(act, wd_ref[c0:c0 + FF_CHUNK, :], preferred_element_type=F32)
    o_ref[...] = _layer_norm(alpha * x1 + (1.0 + gf_ref[...]) * acc, g2_ref[...], b2_ref[...])


def _post(x, oa, ob, mod3, ln1_g, ln1_b, ln2_g, ln2_b, wg, wa, wb, wo, wgu, wd, alpha):
    bsz, t, d = x.shape
    tm = TM_POST
    tiles = t // tm
    d_ff = wd.shape[0]

    def bq(i):
        return i // tiles, i % tiles

    nat = lambda width: pl.BlockSpec((None, tm, width), lambda i: (bq(i)[0], bq(i)[1], 0))
    modspec = lambda k: pl.BlockSpec((None, 1, d), lambda i: (bq(i)[0], 0, k))
    vec = pl.BlockSpec((1, d), lambda i: (0, 0))
    in_specs = [nat(d), nat(oa.shape[2]), nat(ob.shape[2]),
                modspec(1), modspec(0), modspec(2), modspec(3), modspec(4), modspec(5),
                vec, vec, vec, vec,
                _resident(wg.shape), _resident(wa.shape), _resident(wb.shape),
                _resident(wo.shape), _resident(wgu.shape), _resident(wd.shape)]
    row = lambda v: v.reshape(1, d)
    return pl.pallas_call(
        functools.partial(_post_kernel, alpha=alpha, d_ff=d_ff),
        out_shape=jax.ShapeDtypeStruct((bsz, t, d), F32),
        grid=(bsz * tiles,),
        in_specs=in_specs,
        out_specs=nat(d),
        compiler_params=pltpu.CompilerParams(dimension_semantics=("arbitrary",),
                                             vmem_limit_bytes=VMEM_LIMIT),
        name="post",
    )(x, oa, ob, mod3, mod3, mod3, mod3, mod3, mod3,
      row(ln1_g), row(ln1_b), row(ln2_g), row(ln2_b), wg, wa, wb, wo, wgu, wd)


def kernel(x, c, positions, w_ada, b_ada, w_in, sinks, w_branch_a, w_branch_b, w_o,
           ln1_g, ln1_b, w_gate_up, w_down, ln2_g, ln2_b):
    depth = w_ada.shape[0]
    bsz, t, d = x.shape
    alpha = (2 * depth) ** 0.25
    a_q_w = A_Q_HEADS * HEAD_DIM
    a_kv_w = A_KV_HEADS * HEAD_DIM
    b_w = B_HEADS_PER_GROUP * len(B_PATTERNS) * HEAD_DIM
    cols = {"qa": 0, "qa_w": a_q_w, "ka": a_q_w, "kva_w": a_kv_w,
            "qb": a_q_w + 2 * a_kv_w, "kb": a_q_w + 2 * a_kv_w + b_w,
            "vb": a_q_w + 2 * a_kv_w + 2 * b_w, "bg_w": B_HEADS_PER_GROUP * HEAD_DIM}
    gate0 = a_q_w + 2 * a_kv_w + 3 * b_w

    ctab, stab = _rope_tables(positions)
    for l in range(depth):
        mod3 = _ada(c, w_ada[l], b_ada[l]).reshape(bsz, 1, 6 * d)
        w_bf = w_in[l].astype(BF16)
        qa, ka, va, b1, b2, b3 = _inproj(x, mod3, ctab, stab, w_bf[:, :gate0], cols)
        oa = _attn_a(sinks[l], qa, ka, va)
        ob = _attn_b(b1, b2, b3)
        x = _post(x, oa, ob, mod3, ln1_g[l], ln1_b[l], ln2_g[l], ln2_b[l],
                  w_bf[:, gate0:], w_branch_a[l].astype(BF16), w_branch_b[l].astype(BF16),
                  w_o[l].astype(BF16), w_gate_up[l].astype(BF16), w_down[l].astype(BF16), alpha)
    return x
```

```python
import functools

import jax
import jax.numpy as jnp
from jax import lax
from jax.experimental import pallas as pl
from jax.experimental.pallas import tpu as pltpu

F32 = jnp.float32
BF16 = jnp.bfloat16

HEAD_DIM = 64
HALF = HEAD_DIM // 2
BLOCK = 128
LANES = 128
A_Q_HEADS = 16
A_KV_HEADS = 2
A_GROUP = A_Q_HEADS // A_KV_HEADS
A_WINDOW = 128
B_PATTERNS = ((128, 1), (512, 4), (2048, 16))
B_HEADS_PER_GROUP = 8
ROPE_THETA = 10000.0
LN_EPS = 1e-5
NEG_INF = -1e30

TM_IN = 512
TM_POST = 512
SUB_POST = 512
FF_CHUNK = 256
BLOCKS_PER_ITER = 16
VMEM_LIMIT = 56 * 1024 * 1024

_NT = (((1,), (1,)), ((), ()))


def _resident(shape):
    nd = len(shape)
    return pl.BlockSpec(shape, lambda *_: (0,) * nd, pipeline_mode=pl.Buffered(1))


def _ada_kernel(c_ref, w_ref, b_ref, o_ref):
    c = c_ref[...]
    act = (c * jax.nn.sigmoid(c)).astype(BF16)
    o_ref[...] = jnp.dot(act, w_ref[...].astype(BF16), preferred_element_type=F32) + b_ref[...]


def _ada(c, w, b):
    bsz, d = c.shape
    e = w.shape[1]
    tn = d
    return pl.pallas_call(
        _ada_kernel,
        out_shape=jax.ShapeDtypeStruct((bsz, e), F32),
        grid=(e // tn,),
        in_specs=[pl.BlockSpec((bsz, d), lambda j: (0, 0)),
                  pl.BlockSpec((d, tn), lambda j: (0, j)),
                  pl.BlockSpec((1, tn), lambda j: (0, j))],
        out_specs=pl.BlockSpec((bsz, tn), lambda j: (0, j)),
        compiler_params=pltpu.CompilerParams(dimension_semantics=("arbitrary",)),
        name="ada",
    )(c, w, b.reshape(1, e))


def _rope_table_kernel(pos_ref, inv_ref, c_ref, s_ref):
    ang = pos_ref[...] * inv_ref[...]
    cos = jnp.cos(ang)
    sin = jnp.sin(ang)
    reps = LANES // HEAD_DIM
    c_ref[...] = jnp.concatenate([cos, cos] * reps, axis=0).T
    s_ref[...] = jnp.concatenate([-sin, sin] * reps, axis=0).T


def _rope_tables(positions):
    bsz, t = positions.shape
    inv = ROPE_THETA ** (-jnp.arange(HALF, dtype=F32) / HALF)
    tab = pl.BlockSpec((None, t, LANES), lambda b: (b, 0, 0))
    return pl.pallas_call(
        _rope_table_kernel,
        out_shape=[jax.ShapeDtypeStruct((bsz, t, LANES), F32)] * 2,
        grid=(bsz,),
        in_specs=[pl.BlockSpec((None, 1, t), lambda b: (b, 0, 0)),
                  pl.BlockSpec((HALF, 1), lambda b: (0, 0))],
        out_specs=[tab, tab],
        compiler_params=pltpu.CompilerParams(dimension_semantics=("arbitrary",)),
        name="rope_tab",
    )(positions.astype(F32).reshape(bsz, 1, t), inv.reshape(HALF, 1))


def _rope_cols(y, ctab, stab, first_half, scale):
    outs = []
    for j in range(y.shape[1] // LANES):
        slab = y[:, j * LANES:(j + 1) * LANES]
        swapped = jnp.where(first_half,
                            pltpu.roll(slab, LANES - HALF, axis=1),
                            pltpu.roll(slab, HALF, axis=1))
        r = slab * ctab + swapped * stab
        if scale != 1.0:
            r = r * scale
        outs.append(r.astype(BF16))
    return outs[0] if len(outs) == 1 else jnp.concatenate(outs, axis=1)


def _inproj_kernel(x_ref, sc_ref, sh_ref, cn_ref, sn_ref, w_ref,
                   qa_ref, ka_ref, va_ref, b1_ref, b2_ref, b3_ref, us_ref, *, cols):
    tm, d = x_ref.shape
    nslab = d // LANES
    lane = lax.broadcasted_iota(jnp.int32, (tm, LANES), 1)
    first_half = (lane & HALF) == 0
    qscale = HEAD_DIM ** -0.5

    uf = x_ref[...] * (1.0 + sc_ref[...]) + sh_ref[...]
    un = uf.astype(BF16)
    for j in range(nslab):
        us_ref[j] = uf[:, j * LANES:(j + 1) * LANES]

    def strided_rows(load, r):
        per = tm // r
        return jnp.concatenate([load(rho, per, r) for rho in range(r)], axis=0)

    def u_dilated(r):
        def load(rho, per, stride):
            return jnp.concatenate(
                [us_ref[j, pl.ds(rho, per, stride=stride), :] for j in range(nslab)], axis=1)
        return strided_rows(load, r).astype(BF16)

    def tab_dilated(ref, r):
        return strided_rows(lambda rho, per, stride: ref[pl.ds(rho, per, stride=stride), :], r)

    def proj(u, c0, width):
        return jnp.dot(u, w_ref[:, c0:c0 + width], preferred_element_type=F32)

    cn, sn = cn_ref[...], sn_ref[...]
    half_q = cols["qa_w"] // 2
    for h in range(2):
        y = proj(un, cols["qa"] + h * half_q, half_q)
        qa_ref[:, h * half_q:(h + 1) * half_q] = _rope_cols(y, cn, sn, first_half, qscale)
    kvw = cols["kva_w"]
    y = proj(un, cols["ka"], 2 * kvw)
    ka_ref[...] = _rope_cols(y[:, :kvw], cn, sn, first_half, 1.0)
    va_ref[...] = y[:, kvw:].astype(BF16)

    gw = cols["bg_w"]

    def group(u, g, ctab, stab, store):
        y = proj(u, cols["qb"] + g * gw, gw)
        store(0, _rope_cols(y, ctab, stab, first_half, qscale))
        y = proj(u, cols["kb"] + g * gw, gw)
        store(gw, _rope_cols(y, ctab, stab, first_half, 1.0))
        y = proj(u, cols["vb"] + g * gw, gw)
        store(2 * gw, y.astype(BF16))

    def store_nat(c0, val):
        b1_ref[:, c0:c0 + gw] = val

    def store_res(ref):
        nres, per = ref.shape[0], ref.shape[1]

        def store(c0, val):
            for rho in range(nres):
                ref[rho, :, c0:c0 + gw] = val[rho * per:(rho + 1) * per]
        return store

    group(un, 0, cn, sn, store_nat)
    for g, out_ref in ((1, b2_ref), (2, b3_ref)):
        r = B_PATTERNS[g][1]
        group(u_dilated(r), g, tab_dilated(cn_ref, r), tab_dilated(sn_ref, r), store_res(out_ref))


def _inproj(x, mod3, ctab, stab, w_bf, cols):
    bsz, t, d = x.shape
    tm = TM_IN
    tiles = t // tm
    r4, r16 = B_PATTERNS[1][1], B_PATTERNS[2][1]

    def bq(i):
        return i // tiles, i % tiles

    nat = lambda width: pl.BlockSpec((None, tm, width), lambda i: (bq(i)[0], bq(i)[1], 0))
    res = lambda r, width: pl.BlockSpec((None, r, tm // r, width), lambda i: (bq(i)[0], 0, bq(i)[1], 0))
    in_specs = [
        nat(d),
        pl.BlockSpec((None, 1, d), lambda i: (bq(i)[0], 0, 1)),
        pl.BlockSpec((None, 1, d), lambda i: (bq(i)[0], 0, 0)),
        nat(LANES), nat(LANES),
        _resident(w_bf.shape),
    ]
    gw3 = 3 * cols["bg_w"]
    out_shape = [jax.ShapeDtypeStruct((bsz, t, cols["qa_w"]), BF16),
                 jax.ShapeDtypeStruct((bsz, t, cols["kva_w"]), BF16),
                 jax.ShapeDtypeStruct((bsz, t, cols["kva_w"]), BF16),
                 jax.ShapeDtypeStruct((bsz, t, gw3), BF16),
                 jax.ShapeDtypeStruct((bsz, r4, t // r4, gw3), BF16),
                 jax.ShapeDtypeStruct((bsz, r16, t // r16, gw3), BF16)]
    out_specs = [nat(cols["qa_w"]), nat(cols["kva_w"]), nat(cols["kva_w"]),
                 nat(gw3), res(r4, gw3), res(r16, gw3)]
    qa, ka, va, b1, b2, b3 = pl.pallas_call(
        functools.partial(_inproj_kernel, cols=cols),
        out_shape=out_shape,
        grid=(bsz * tiles,),
        in_specs=in_specs,
        out_specs=out_specs,
        scratch_shapes=[pltpu.VMEM((d // LANES, tm, LANES), F32)],
        compiler_params=pltpu.CompilerParams(dimension_semantics=("arbitrary",),
                                             vmem_limit_bytes=VMEM_LIMIT),
        name="inproj",
    )(x, mod3, mod3, ctab, stab, w_bf)
    return qa, ka, va, b1, b2.reshape(bsz, t, gw3), b3.reshape(bsz, t, gw3)


def _band_bias(keys, n_back, prev_valid):
    qi = lax.broadcasted_iota(jnp.int32, (BLOCK, keys), 0)
    c = lax.broadcasted_iota(jnp.int32, (BLOCK, keys), 1)
    dist = qi + (keys - BLOCK) - c
    ok = (dist >= 0) & (dist <= n_back)
    if not prev_valid:
        ok = ok & (c >= keys - BLOCK)
    return jnp.where(ok, 0.0, NEG_INF).astype(F32)


def _lane_consts():
    lane = lax.broadcasted_iota(jnp.int32, (1, LANES), 1)
    mlo = (lane < HEAD_DIM).astype(BF16)
    mhi = (lane >= HEAD_DIM).astype(BF16)
    lane_lo = lax.broadcasted_iota(jnp.int32, (BLOCK, LANES), 1) < HEAD_DIM
    return mlo, mhi, lane_lo


def _pair_block(q, kc, vc, bias, mlo, mhi, lane_lo, sinks=None, want_lse=True):
    q2 = jnp.concatenate([q * mlo, q * mhi], axis=0)
    s = lax.dot_general(q2, kc, _NT, preferred_element_type=F32) + bias
    halves = (s[:BLOCK], s[BLOCK:])
    ms = [jnp.max(h, axis=-1, keepdims=True) for h in halves]
    if sinks is not None:
        ms = [jnp.maximum(m, sk) for m, sk in zip(ms, sinks)]
    p = jnp.concatenate([jnp.exp(h - m) for h, m in zip(halves, ms)], axis=0).astype(BF16)
    ones = jnp.ones(vc.shape, BF16)
    r = jnp.dot(p, jnp.concatenate([vc, ones], axis=1), preferred_element_type=F32)
    num = jnp.where(lane_lo, r[:BLOCK, :LANES], r[BLOCK:, :LANES])
    den = jnp.where(lane_lo, r[:BLOCK, LANES:], r[BLOCK:, LANES:])
    if sinks is not None:
        den = den + jnp.where(lane_lo, jnp.exp(sinks[0] - ms[0]), jnp.exp(sinks[1] - ms[1]))
    out = num / den
    if not want_lse:
        return out
    return out, jnp.where(lane_lo, ms[0], ms[1]) + jnp.log(den)


def _attn_a_kernel(sink_ref, q_ref, k_ref, v_ref, o_ref, kd_ref, vd_ref, bias_ref):
    t = q_ref.shape[0]
    nblk = t // BLOCK
    nslab = A_GROUP // 2
    mlo, mhi, lane_lo = _lane_consts()
    lane_lo_t = lax.broadcasted_iota(jnp.int32, (t, LANES), 1) < HEAD_DIM

    for src, dup in ((k_ref, kd_ref), (v_ref, vd_ref)):
        val = src[...].astype(F32)
        swp = pltpu.roll(val, HEAD_DIM, axis=1)
        dup[0] = jnp.where(lane_lo_t, val, swp).astype(BF16)
        dup[1] = jnp.where(lane_lo_t, swp, val).astype(BF16)
    twice = lambda b: jnp.concatenate([b, b], axis=0)
    bias_ref[0] = twice(_band_bias(2 * BLOCK, A_WINDOW - 1, False))
    bias_ref[1] = twice(_band_bias(2 * BLOCK, A_WINDOW - 1, True))

    def block(i, carry):
        r0 = pl.multiple_of(i * BLOCK, BLOCK)
        p0 = pl.multiple_of(jnp.maximum(i - 1, 0) * BLOCK, BLOCK)
        bias = bias_ref[jnp.minimum(i, 1)]
        for hk in range(A_KV_HEADS):
            kc = jnp.concatenate([kd_ref[hk, pl.ds(p0, BLOCK), :], kd_ref[hk, pl.ds(r0, BLOCK), :]], axis=0)
            vc = jnp.concatenate([vd_ref[hk, pl.ds(p0, BLOCK), :], vd_ref[hk, pl.ds(r0, BLOCK), :]], axis=0)
            for sl in range(nslab):
                slab = hk * nslab + sl
                cols = slice(slab * LANES, (slab + 1) * LANES)
                out = _pair_block(q_ref[pl.ds(r0, BLOCK), cols], kc, vc, bias, mlo, mhi, lane_lo,
                                  sinks=(sink_ref[2 * slab], sink_ref[2 * slab + 1]), want_lse=False)
                o_ref[pl.ds(r0, BLOCK), cols] = out.astype(BF16)
        return carry

    lax.fori_loop(0, nblk, block, 0, unroll=2)


def _attn_a(sinks, qa, ka, va):
    bsz, t, qw = qa.shape
    kw = ka.shape[2]
    seq = lambda width: pl.BlockSpec((None, t, width), lambda b: (b, 0, 0))
    return pl.pallas_call(
        _attn_a_kernel,
        out_shape=jax.ShapeDtypeStruct((bsz, t, qw), BF16),
        grid=(bsz,),
        in_specs=[pl.BlockSpec(memory_space=pltpu.SMEM), seq(qw), seq(kw), seq(kw)],
        out_specs=seq(qw),
        scratch_shapes=[pltpu.VMEM((A_KV_HEADS, t, LANES), BF16),
                        pltpu.VMEM((A_KV_HEADS, t, LANES), BF16),
                        pltpu.VMEM((2, 2 * BLOCK, 2 * BLOCK), F32)],
        compiler_params=pltpu.CompilerParams(dimension_semantics=("arbitrary",),
                                             vmem_limit_bytes=VMEM_LIMIT),
        name="attn_a",
    )(sinks, qa, ka, va)


def _attn_b_kernel(q1_ref, k1_ref, v1_ref, q2_ref, k2_ref, v2_ref, q3_ref, k3_ref, v3_ref,
                   o_ref, o2_s, l2_s, o3_s, l3_s, bias_ref, bias1_ref):
    t = q1_ref.shape[0]
    mlo, mhi, lane_lo = _lane_consts()
    pair = functools.partial(_pair_block, mlo=mlo, mhi=mhi, lane_lo=lane_lo)
    (w1, _), (w4, r4), (w16, r16) = B_PATTERNS
    nb = BLOCKS_PER_ITER
    nb4 = t // r4 // BLOCK
    assert w1 == w4 // r4 == w16 // r16 and t // r16 == BLOCK and nb % nb4 == 0

    twice = lambda b: jnp.concatenate([b, b], axis=0)
    bias_ref[0] = twice(_band_bias(2 * BLOCK, w1, False))
    bias_ref[1] = twice(_band_bias(2 * BLOCK, w1, True))
    bias1_ref[...] = twice(_band_bias(BLOCK, w1, True))

    def first_block(q_ref, k_ref, v_ref, r0):
        return pair(q_ref[pl.ds(r0, BLOCK), :], k_ref[pl.ds(r0, BLOCK), :],
                    v_ref[pl.ds(r0, BLOCK), :], bias1_ref[...])

    def banded(q_ref, k_ref, v_ref, r0, bias):
        p0 = pl.multiple_of(r0 - BLOCK, BLOCK)
        return pair(q_ref[pl.ds(r0, BLOCK), :], k_ref[pl.ds(p0, 2 * BLOCK), :],
                    v_ref[pl.ds(p0, 2 * BLOCK), :], bias)

    def g3(it, carry):
        for u in range(nb):
            rho = it * nb + u
            out, lse = first_block(q3_ref, k3_ref, v3_ref, pl.multiple_of(rho * BLOCK, BLOCK))
            o3_s[pl.ds(rho, BLOCK, stride=r16), :] = out
            l3_s[pl.ds(rho, BLOCK, stride=r16), :] = lse
        return carry

    lax.fori_loop(0, r16 // nb, g3, 0)

    def g2(it, carry):
        base = pl.multiple_of(it * (nb * BLOCK), nb * BLOCK)
        for u in range(nb):
            rho = it * (nb // nb4) + u // nb4
            sb = u % nb4
            r0 = pl.multiple_of(base + u * BLOCK, BLOCK)
            if sb == 0:
                out, lse = first_block(q2_ref, k2_ref, v2_ref, r0)
            else:
                out, lse = banded(q2_ref, k2_ref, v2_ref, r0, bias_ref[1])
            start = sb * (BLOCK * r4) + rho
            o2_s[pl.ds(start, BLOCK, stride=r4), :] = out
            l2_s[pl.ds(start, BLOCK, stride=r4), :] = lse
        return carry

    lax.fori_loop(0, r4 * nb4 // nb, g2, 0)

    def g1(it, carry):
        base = pl.multiple_of(it * (nb * BLOCK), nb * BLOCK)
        for u in range(nb):
            r0 = pl.multiple_of(base + u * BLOCK, BLOCK)
            if u == 0:
                p0 = pl.multiple_of(jnp.maximum(r0 - BLOCK, 0), BLOCK)
                o1, l1 = pair(q1_ref[pl.ds(r0, BLOCK), :],
                              jnp.concatenate([k1_ref[pl.ds(p0, BLOCK), :], k1_ref[pl.ds(r0, BLOCK), :]], axis=0),
                              jnp.concatenate([v1_ref[pl.ds(p0, BLOCK), :], v1_ref[pl.ds(r0, BLOCK), :]], axis=0),
                              bias_ref[jnp.minimum(it, 1)])
            else:
                o1, l1 = banded(q1_ref, k1_ref, v1_ref, r0, bias_ref[1])
            o2, l2 = o2_s[pl.ds(r0, BLOCK), :], l2_s[pl.ds(r0, BLOCK), :]
            o3, l3 = o3_s[pl.ds(r0, BLOCK), :], l3_s[pl.ds(r0, BLOCK), :]
            mx = jnp.maximum(jnp.maximum(l1, l2), l3)
            e1, e2, e3 = jnp.exp(l1 - mx), jnp.exp(l2 - mx), jnp.exp(l3 - mx)
            merged = (e1 * o1 + e2 * o2 + e3 * o3) / (e1 + e2 + e3)
            o_ref[pl.ds(r0, BLOCK), :] = merged.astype(BF16)
        return carry

    lax.fori_loop(0, t // (nb * BLOCK), g1, 0)


def _attn_b(b1, b2, b3):
    bsz, t, w3 = b1.shape
    gw = w3 // 3
    npair = gw // LANES

    def spec(part):
        return pl.BlockSpec((None, t, LANES), lambda b, j: (b, 0, part * npair + j))

    return pl.pallas_call(
        _attn_b_kernel,
        out_shape=jax.ShapeDtypeStruct((bsz, t, gw), BF16),
        grid=(bsz, npair),
        in_specs=[spec(0), spec(1), spec(2)] * 3,
        out_specs=pl.BlockSpec((None, t, LANES), lambda b, j: (b, 0, j)),
        scratch_shapes=[pltpu.VMEM((t, LANES), F32)] * 4
                       + [pltpu.VMEM((2, 2 * BLOCK, 2 * BLOCK), F32),
                          pltpu.VMEM((2 * BLOCK, BLOCK), F32)],
        compiler_params=pltpu.CompilerParams(dimension_semantics=("arbitrary", "arbitrary"),
                                             vmem_limit_bytes=VMEM_LIMIT),
        name="attn_b",
    )(b1, b1, b1, b2, b2, b2, b3, b3, b3)


def _layer_norm(h, g, b):
    mu = jnp.mean(h, axis=-1, keepdims=True)
    xc = h - mu
    var = jnp.mean(xc * xc, axis=-1, keepdims=True)
    return xc * lax.rsqrt(var + LN_EPS) * g + b


def _post_kernel(x_ref, oa_ref, ob_ref, scm_ref, shm_ref, gm_ref, shf_ref, scf_ref, gf_ref,
                 g1_ref, b1_ref, g2_ref, b2_ref,
                 wg_ref, wa_ref, wb_ref, wo_ref, wgu_ref, wd_ref, o_ref, *, alpha, d_ff):
    d = x_ref.shape[1]

    def mix(rows):
        x = x_ref[rows, :]
        u = (x * (1.0 + scm_ref[...]) + shm_ref[...]).astype(BF16)
        ga = jax.nn.sigmoid(jnp.dot(u, wg_ref[:, :d], preferred_element_type=F32))
        ya = jnp.dot(oa_ref[rows, :], wa_ref[...], preferred_element_type=F32)
        merged = ga * ya
        gb = jax.nn.sigmoid(jnp.dot(u, wg_ref[:, d:], preferred_element_type=F32))
        yb = jnp.dot(ob_ref[rows, :], wb_ref[...], preferred_element_type=F32)
        merged = (merged + gb * yb).astype(BF16)
        return jnp.dot(merged, wo_ref[...], preferred_element_type=F32)

    def norm1(rows, y):
        return _layer_norm(alpha * x_ref[rows, :] + (1.0 + gm_ref[...]) * y, g1_ref[...], b1_ref[...])

    def ffn(x1):
        u2 = (x1 * (1.0 + scf_ref[...]) + shf_ref[...]).astype(BF16)
        acc = jnp.zeros_like(x1)
        for c in range(d_ff // FF_CHUNK):
            c0 = c * FF_CHUNK
            hg = jnp.dot(u2, wgu_ref[:, c0:c0 + FF_CHUNK], preferred_element_type=F32)
            hu = jnp.dot(u2, wgu_ref[:, d_ff + c0:d_ff + c0 + FF_CHUNK], preferred_element_type=F32)
            act = (hg * jax.nn.sigmoid(hg) * hu).astype(BF16)
            acc = acc + jnp.dot(act, wd_ref[c0:c0 + FF_CHUNK, :], preferred_element_type=F32)
        return acc

    def norm2(rows, x1, acc):
        o_ref[rows, :] = _layer_norm(alpha * x1 + (1.0 + gf_ref[...]) * acc, g2_ref[...], b2_ref[...])

    subs = [slice(r0, r0 + SUB_POST) for r0 in range(0, x_ref.shape[0], SUB_POST)]
    ys = [mix(rows) for rows in subs]
    x1s, accs = [], []
    for i, rows in enumerate(subs):
        x1s.append(norm1(rows, ys[i]))
        accs.append(ffn(x1s[i]))
        if i > 0:
            norm2(subs[i - 1], x1s[i - 1], accs[i - 1])
    norm2(subs[-1], x1s[-1], accs[-1])


def _post(x, oa, ob, mod3, ln1_g, ln1_b, ln2_g, ln2_b, wg, wa, wb, wo, wgu, wd, alpha):
    bsz, t, d = x.shape
    tm = TM_POST
    tiles = t // tm
    d_ff = wd.shape[0]

    def bq(i):
        return i // tiles, i % tiles

    nat = lambda width: pl.BlockSpec((None, tm, width), lambda i: (bq(i)[0], bq(i)[1], 0))
    modspec = lambda k: pl.BlockSpec((None, 1, d), lambda i: (bq(i)[0], 0, k))
    vec = pl.BlockSpec((1, d), lambda i: (0, 0))
    in_specs = [nat(d), nat(oa.shape[2]), nat(ob.shape[2]),
                modspec(1), modspec(0), modspec(2), modspec(3), modspec(4), modspec(5),
                vec, vec, vec, vec,
                _resident(wg.shape), _resident(wa.shape), _resident(wb.shape),
                _resident(wo.shape), _resident(wgu.shape), _resident(wd.shape)]
    row = lambda v: v.reshape(1, d)
    return pl.pallas_call(
        functools.partial(_post_kernel, alpha=alpha, d_ff=d_ff),
        out_shape=jax.ShapeDtypeStruct((bsz, t, d), F32),
        grid=(bsz * tiles,),
        in_specs=in_specs,
        out_specs=nat(d),
        compiler_params=pltpu.CompilerParams(dimension_semantics=("arbitrary",),
                                             vmem_limit_bytes=VMEM_LIMIT),
        name="post",
    )(x, oa, ob, mod3, mod3, mod3, mod3, mod3, mod3,
      row(ln1_g), row(ln1_b), row(ln2_g), row(ln2_b), wg, wa, wb, wo, wgu, wd)


def kernel(x, c, positions, w_ada, b_ada, w_in, sinks, w_branch_a, w_branch_b, w_o,
           ln1_g, ln1_b, w_gate_up, w_down, ln2_g, ln2_b):
    depth = w_ada.shape[0]
    bsz, t, d = x.shape
    alpha = (2 * depth) ** 0.25
    a_q_w = A_Q_HEADS * HEAD_DIM
    a_kv_w = A_KV_HEADS * HEAD_DIM
    b_w = B_HEADS_PER_GROUP * len(B_PATTERNS) * HEAD_DIM
    cols = {"qa": 0, "qa_w": a_q_w, "ka": a_q_w, "kva_w": a_kv_w,
            "qb": a_q_w + 2 * a_kv_w, "kb": a_q_w + 2 * a_kv_w + b_w,
            "vb": a_q_w + 2 * a_kv_w + 2 * b_w, "bg_w": B_HEADS_PER_GROUP * HEAD_DIM}
    gate0 = a_q_w + 2 * a_kv_w + 3 * b_w

    ctab, stab = _rope_tables(positions)
    for l in range(depth):
        mod3 = _ada(c, w_ada[l], b_ada[l]).reshape(bsz, 1, 6 * d)
        w_bf = w_in[l].astype(BF16)
        qa, ka, va, b1, b2, b3 = _inproj(x, mod3, ctab, stab, w_bf[:, :gate0], cols)
        oa = _attn_a(sinks[l], qa, ka, va)
        ob = _attn_b(b1, b2, b3)
        x = _post(x, oa, ob, mod3, ln1_g[l], ln1_b[l], ln2_g[l], ln2_b[l],
                  w_bf[:, gate0:], w_branch_a[l].astype(BF16), w_branch_b[l].astype(BF16),
                  w_o[l].astype(BF16), w_gate_up[l].astype(BF16), w_down[l].astype(BF16), alpha)
    return x
```

```python
import functools

import jax
import jax.numpy as jnp
from jax import lax
from jax.experimental import pallas as pl
from jax.experimental.pallas import tpu as pltpu

F32 = jnp.float32
BF16 = jnp.bfloat16

HEAD_DIM = 64
HALF = HEAD_DIM // 2
BLOCK = 128
LANES = 128
A_Q_HEADS = 16
A_KV_HEADS = 2
A_GROUP = A_Q_HEADS // A_KV_HEADS
A_WINDOW = 128
B_PATTERNS = ((128, 1), (512, 4), (2048, 16))
B_HEADS_PER_GROUP = 8
ROPE_THETA = 10000.0
LN_EPS = 1e-5
NEG_INF = -1e30
LOG2E = 1.4426950408889634

TM_IN = 512
TM_POST = 512
SUB_POST = 512
FF_CHUNK = 256
BLOCKS_PER_ITER = 16
VMEM_LIMIT = 56 * 1024 * 1024

_NT = (((1,), (1,)), ((), ()))


def _resident(shape):
    nd = len(shape)
    return pl.BlockSpec(shape, lambda *_: (0,) * nd, pipeline_mode=pl.Buffered(1))


def _ada_kernel(c_ref, w_ref, b_ref, o_ref):
    c = c_ref[...]
    act = (c * jax.nn.sigmoid(c)).astype(BF16)
    o_ref[...] = jnp.dot(act, w_ref[...].astype(BF16), preferred_element_type=F32) + b_ref[...]


def _ada(c, w, b):
    bsz, d = c.shape
    e = w.shape[1]
    tn = d
    return pl.pallas_call(
        _ada_kernel,
        out_shape=jax.ShapeDtypeStruct((bsz, e), F32),
        grid=(e // tn,),
        in_specs=[pl.BlockSpec((bsz, d), lambda j: (0, 0)),
                  pl.BlockSpec((d, tn), lambda j: (0, j)),
                  pl.BlockSpec((1, tn), lambda j: (0, j))],
        out_specs=pl.BlockSpec((bsz, tn), lambda j: (0, j)),
        compiler_params=pltpu.CompilerParams(dimension_semantics=("arbitrary",)),
        name="ada",
    )(c, w, b.reshape(1, e))


def _rope_table_kernel(pos_ref, inv_ref, c_ref, s_ref):
    ang = pos_ref[...] * inv_ref[...]
    cos = jnp.cos(ang)
    sin = jnp.sin(ang)
    reps = LANES // HEAD_DIM
    c_ref[...] = jnp.concatenate([cos, cos] * reps, axis=0).T
    s_ref[...] = jnp.concatenate([-sin, sin] * reps, axis=0).T


def _rope_tables(positions):
    bsz, t = positions.shape
    inv = ROPE_THETA ** (-jnp.arange(HALF, dtype=F32) / HALF)
    tab = pl.BlockSpec((None, t, LANES), lambda b: (b, 0, 0))
    return pl.pallas_call(
        _rope_table_kernel,
        out_shape=[jax.ShapeDtypeStruct((bsz, t, LANES), F32)] * 2,
        grid=(bsz,),
        in_specs=[pl.BlockSpec((None, 1, t), lambda b: (b, 0, 0)),
                  pl.BlockSpec((HALF, 1), lambda b: (0, 0))],
        out_specs=[tab, tab],
        compiler_params=pltpu.CompilerParams(dimension_semantics=("arbitrary",)),
        name="rope_tab",
    )(positions.astype(F32).reshape(bsz, 1, t), inv.reshape(HALF, 1))


def _rope_cols(y, ctab, stab, first_half, scale):
    outs = []
    for j in range(y.shape[1] // LANES):
        slab = y[:, j * LANES:(j + 1) * LANES]
        swapped = jnp.where(first_half,
                            pltpu.roll(slab, LANES - HALF, axis=1),
                            pltpu.roll(slab, HALF, axis=1))
        r = slab * ctab + swapped * stab
        if scale != 1.0:
            r = r * scale
        outs.append(r.astype(BF16))
    return outs[0] if len(outs) == 1 else jnp.concatenate(outs, axis=1)


def _inproj_kernel(x_ref, sc_ref, sh_ref, cn_ref, sn_ref, w_ref,
                   qa_ref, ka_ref, va_ref, b1_ref, b2_ref, b3_ref, us_ref, *, cols):
    tm, d = x_ref.shape
    nslab = d // LANES
    lane = lax.broadcasted_iota(jnp.int32, (tm, LANES), 1)
    first_half = (lane & HALF) == 0
    qscale = HEAD_DIM ** -0.5 * LOG2E

    uf = x_ref[...] * (1.0 + sc_ref[...]) + sh_ref[...]
    un = uf.astype(BF16)
    for j in range(nslab):
        us_ref[j] = uf[:, j * LANES:(j + 1) * LANES]

    def strided_rows(load, r):
        per = tm // r
        return jnp.concatenate([load(rho, per, r) for rho in range(r)], axis=0)

    def u_dilated(r):
        def load(rho, per, stride):
            return jnp.concatenate(
                [us_ref[j, pl.ds(rho, per, stride=stride), :] for j in range(nslab)], axis=1)
        return strided_rows(load, r).astype(BF16)

    def tab_dilated(ref, r):
        return strided_rows(lambda rho, per, stride: ref[pl.ds(rho, per, stride=stride), :], r)

    def proj(u, c0, width):
        return jnp.dot(u, w_ref[:, c0:c0 + width], preferred_element_type=F32)

    cn, sn = cn_ref[...], sn_ref[...]
    half_q = cols["qa_w"] // 2
    for h in range(2):
        y = proj(un, cols["qa"] + h * half_q, half_q)
        qa_ref[:, h * half_q:(h + 1) * half_q] = _rope_cols(y, cn, sn, first_half, qscale)
    kvw = cols["kva_w"]
    y = proj(un, cols["ka"], 2 * kvw)
    ka_ref[...] = _rope_cols(y[:, :kvw], cn, sn, first_half, 1.0)
    va_ref[...] = y[:, kvw:].astype(BF16)

    gw = cols["bg_w"]

    def group(u, g, ctab, stab, store):
        y = proj(u, cols["qb"] + g * gw, gw)
        store(0, _rope_cols(y, ctab, stab, first_half, qscale))
        y = proj(u, cols["kb"] + g * gw, gw)
        store(gw, _rope_cols(y, ctab, stab, first_half, 1.0))
        y = proj(u, cols["vb"] + g * gw, gw)
        store(2 * gw, y.astype(BF16))

    def store_nat(c0, val):
        b1_ref[:, c0:c0 + gw] = val

    def store_res(ref):
        nres, per = ref.shape[0], ref.shape[1]

        def store(c0, val):
            for rho in range(nres):
                ref[rho, :, c0:c0 + gw] = val[rho * per:(rho + 1) * per]
        return store

    group(un, 0, cn, sn, store_nat)
    for g, out_ref in ((1, b2_ref), (2, b3_ref)):
        r = B_PATTERNS[g][1]
        group(u_dilated(r), g, tab_dilated(cn_ref, r), tab_dilated(sn_ref, r), store_res(out_ref))


def _inproj(x, mod3, ctab, stab, w_bf, cols):
    bsz, t, d = x.shape
    tm = TM_IN
    tiles = t // tm
    r4, r16 = B_PATTERNS[1][1], B_PATTERNS[2][1]

    def bq(i):
        return i // tiles, i % tiles

    nat = lambda width: pl.BlockSpec((None, tm, width), lambda i: (bq(i)[0], bq(i)[1], 0))
    res = lambda r, width: pl.BlockSpec((None, r, tm // r, width), lambda i: (bq(i)[0], 0, bq(i)[1], 0))
    in_specs = [
        nat(d),
        pl.BlockSpec((None, 1, d), lambda i: (bq(i)[0], 0, 1)),
        pl.BlockSpec((None, 1, d), lambda i: (bq(i)[0], 0, 0)),
        nat(LANES), nat(LANES),
        _resident(w_bf.shape),
    ]
    gw3 = 3 * cols["bg_w"]
    out_shape = [jax.ShapeDtypeStruct((bsz, t, cols["qa_w"]), BF16),
                 jax.ShapeDtypeStruct((bsz, t, cols["kva_w"]), BF16),
                 jax.ShapeDtypeStruct((bsz, t, cols["kva_w"]), BF16),
                 jax.ShapeDtypeStruct((bsz, t, gw3), BF16),
                 jax.ShapeDtypeStruct((bsz, r4, t // r4, gw3), BF16),
                 jax.ShapeDtypeStruct((bsz, r16, t // r16, gw3), BF16)]
    out_specs = [nat(cols["qa_w"]), nat(cols["kva_w"]), nat(cols["kva_w"]),
                 nat(gw3), res(r4, gw3), res(r16, gw3)]
    qa, ka, va, b1, b2, b3 = pl.pallas_call(
        functools.partial(_inproj_kernel, cols=cols),
        out_shape=out_shape,
        grid=(bsz * tiles,),
        in_specs=in_specs,
        out_specs=out_specs,
        scratch_shapes=[pltpu.VMEM((d // LANES, tm, LANES), F32)],
        compiler_params=pltpu.CompilerParams(dimension_semantics=("arbitrary",),
                                             vmem_limit_bytes=VMEM_LIMIT),
        name="inproj",
    )(x, mod3, mod3, ctab, stab, w_bf)
    return qa, ka, va, b1, b2.reshape(bsz, t, gw3), b3.reshape(bsz, t, gw3)


def _band_bias(keys, n_back, prev_valid):
    qi = lax.broadcasted_iota(jnp.int32, (BLOCK, keys), 0)
    c = lax.broadcasted_iota(jnp.int32, (BLOCK, keys), 1)
    dist = qi + (keys - BLOCK) - c
    ok = (dist >= 0) & (dist <= n_back)
    if not prev_valid:
        ok = ok & (c >= keys - BLOCK)
    return jnp.where(ok, 0.0, NEG_INF).astype(F32)


def _lane_consts():
    lane = lax.broadcasted_iota(jnp.int32, (1, LANES), 1)
    mlo = (lane < HEAD_DIM).astype(BF16)
    mhi = (lane >= HEAD_DIM).astype(BF16)
    lane_lo = lax.broadcasted_iota(jnp.int32, (BLOCK, LANES), 1) < HEAD_DIM
    return mlo, mhi, lane_lo


def _pair_scores(q, kc, mlo, mhi):
    q2 = jnp.concatenate([q * mlo, q * mhi], axis=0)
    return lax.dot_general(q2, kc, _NT, preferred_element_type=F32)


def _pair_softmax(s, sinks2):
    halves = (s[:BLOCK], s[BLOCK:])
    ms = [jnp.max(h, axis=-1, keepdims=True) for h in halves]
    if sinks2 is not None:
        ms = [jnp.maximum(m, sk) for m, sk in zip(ms, sinks2)]
    p = jnp.concatenate([jnp.exp2(h - m) for h, m in zip(halves, ms)], axis=0).astype(BF16)
    return p, ms


def _pair_values(p, vc, ms, lane_lo, sinks2, want_lse):
    ones = jnp.ones(vc.shape, BF16)
    r = jnp.dot(p, jnp.concatenate([vc, ones], axis=1), preferred_element_type=F32)
    num = jnp.where(lane_lo, r[:BLOCK, :LANES], r[BLOCK:, :LANES])
    den = jnp.where(lane_lo, r[:BLOCK, LANES:], r[BLOCK:, LANES:])
    if sinks2 is not None:
        den = den + jnp.where(lane_lo, jnp.exp2(sinks2[0] - ms[0]), jnp.exp2(sinks2[1] - ms[1]))
    out = num / den
    if not want_lse:
        return out
    return out, jnp.where(lane_lo, ms[0], ms[1]) + jnp.log2(den)


def _pair_block(q, kc, vc, bias, mlo, mhi, lane_lo):
    p, ms = _pair_softmax(_pair_scores(q, kc, mlo, mhi) + bias, None)
    return _pair_values(p, vc, ms, lane_lo, None, True)


def _pair_block_window(q, kc, vc, prev_bias, tri, tri_bf, ntri_bf, mlo, mhi, lane_lo, sinks2):
    s = _pair_scores(q, kc, mlo, mhi)
    merged = jnp.where(tri, s[:, BLOCK:], s[:, :BLOCK] + prev_bias)
    p, ms = _pair_softmax(merged, sinks2)
    p2 = jnp.concatenate([p * ntri_bf, p * tri_bf], axis=1)
    return _pair_values(p2, vc, ms, lane_lo, sinks2, False)


def _attn_a_kernel(sink_ref, q_ref, k_ref, v_ref, o_ref, kd_ref, vd_ref):
    t = q_ref.shape[0]
    nblk = t // BLOCK
    nslab = A_GROUP // 2
    assert A_WINDOW == BLOCK
    mlo, mhi, lane_lo = _lane_consts()
    lane_lo_t = lax.broadcasted_iota(jnp.int32, (t, LANES), 1) < HEAD_DIM

    for src, dup in ((k_ref, kd_ref), (v_ref, vd_ref)):
        val = src[...].astype(F32)
        swp = pltpu.roll(val, HEAD_DIM, axis=1)
        dup[0] = jnp.where(lane_lo_t, val, swp).astype(BF16)
        dup[1] = jnp.where(lane_lo_t, swp, val).astype(BF16)
    qi = lax.broadcasted_iota(jnp.int32, (2 * BLOCK, BLOCK), 0) & (BLOCK - 1)
    tri = lax.broadcasted_iota(jnp.int32, (2 * BLOCK, BLOCK), 1) <= qi
    tri_bf = tri.astype(F32).astype(BF16)
    ntri_bf = 1.0 - tri_bf

    def block(i, carry):
        r0 = pl.multiple_of(i * BLOCK, BLOCK)
        p0 = pl.multiple_of(jnp.maximum(i - 1, 0) * BLOCK, BLOCK)
        prev_bias = jnp.where(i > 0, 0.0, NEG_INF).astype(F32)
        for hk in range(A_KV_HEADS):
            kc = jnp.concatenate([kd_ref[hk, pl.ds(p0, BLOCK), :], kd_ref[hk, pl.ds(r0, BLOCK), :]], axis=0)
            vc = jnp.concatenate([vd_ref[hk, pl.ds(p0, BLOCK), :], vd_ref[hk, pl.ds(r0, BLOCK), :]], axis=0)
            for sl in range(nslab):
                slab = hk * nslab + sl
                cols = slice(slab * LANES, (slab + 1) * LANES)
                sinks2 = (sink_ref[2 * slab] * LOG2E, sink_ref[2 * slab + 1] * LOG2E)
                out = _pair_block_window(q_ref[pl.ds(r0, BLOCK), cols], kc, vc, prev_bias,
                                         tri, tri_bf, ntri_bf, mlo, mhi, lane_lo, sinks2)
                o_ref[pl.ds(r0, BLOCK), cols] = out.astype(BF16)
        return carry

    lax.fori_loop(0, nblk, block, 0, unroll=2)


def _attn_a(sinks, qa, ka, va):
    bsz, t, qw = qa.shape
    kw = ka.shape[2]
    seq = lambda width: pl.BlockSpec((None, t, width), lambda b: (b, 0, 0))
    return pl.pallas_call(
        _attn_a_kernel,
        out_shape=jax.ShapeDtypeStruct((bsz, t, qw), BF16),
        grid=(bsz,),
        in_specs=[pl.BlockSpec(memory_space=pltpu.SMEM), seq(qw), seq(kw), seq(kw)],
        out_specs=seq(qw),
        scratch_shapes=[pltpu.VMEM((A_KV_HEADS, t, LANES), BF16),
                        pltpu.VMEM((A_KV_HEADS, t, LANES), BF16)],
        compiler_params=pltpu.CompilerParams(dimension_semantics=("arbitrary",),
                                             vmem_limit_bytes=VMEM_LIMIT),
        name="attn_a",
    )(sinks, qa, ka, va)


def _attn_b_kernel(q1_ref, k1_ref, v1_ref, q2_ref, k2_ref, v2_ref, q3_ref, k3_ref, v3_ref,
                   o_ref, o2_s, l2_s, o3_s, l3_s, bias_ref, bias1_ref):
    t = q1_ref.shape[0]
    mlo, mhi, lane_lo = _lane_consts()
    pair = functools.partial(_pair_block, mlo=mlo, mhi=mhi, lane_lo=lane_lo)
    (w1, _), (w4, r4), (w16, r16) = B_PATTERNS
    nb = BLOCKS_PER_ITER
    nb4 = t // r4 // BLOCK
    assert w1 == w4 // r4 == w16 // r16 and t // r16 == BLOCK and nb % nb4 == 0

    twice = lambda b: jnp.concatenate([b, b], axis=0)
    bias_ref[0] = twice(_band_bias(2 * BLOCK, w1, False))
    bias_ref[1] = twice(_band_bias(2 * BLOCK, w1, True))
    bias1_ref[...] = twice(_band_bias(BLOCK, w1, True))

    def first_block(q_ref, k_ref, v_ref, r0):
        return pair(q_ref[pl.ds(r0, BLOCK), :], k_ref[pl.ds(r0, BLOCK), :],
                    v_ref[pl.ds(r0, BLOCK), :], bias1_ref[...])

    def banded(q_ref, k_ref, v_ref, r0, bias):
        p0 = pl.multiple_of(r0 - BLOCK, BLOCK)
        return pair(q_ref[pl.ds(r0, BLOCK), :], k_ref[pl.ds(p0, 2 * BLOCK), :],
                    v_ref[pl.ds(p0, 2 * BLOCK), :], bias)

    def g3(it, carry):
        for u in range(nb):
            rho = it * nb + u
            out, lse = first_block(q3_ref, k3_ref, v3_ref, pl.multiple_of(rho * BLOCK, BLOCK))
            o3_s[pl.ds(rho, BLOCK, stride=r16), :] = out
            l3_s[pl.ds(rho, BLOCK, stride=r16), :] = lse
        return carry

    lax.fori_loop(0, r16 // nb, g3, 0)

    def g2(it, carry):
        base = pl.multiple_of(it * (nb * BLOCK), nb * BLOCK)
        for u in range(nb):
            rho = it * (nb // nb4) + u // nb4
            sb = u % nb4
            r0 = pl.multiple_of(base + u * BLOCK, BLOCK)
            if sb == 0:
                out, lse = first_block(q2_ref, k2_ref, v2_ref, r0)
            else:
                out, lse = banded(q2_ref, k2_ref, v2_ref, r0, bias_ref[1])
            start = sb * (BLOCK * r4) + rho
            o2_s[pl.ds(start, BLOCK, stride=r4), :] = out
            l2_s[pl.ds(start, BLOCK, stride=r4), :] = lse
        return carry

    lax.fori_loop(0, r4 * nb4 // nb, g2, 0)

    def g1(it, carry):
        base = pl.multiple_of(it * (nb * BLOCK), nb * BLOCK)
        for u in range(nb):
            r0 = pl.multiple_of(base + u * BLOCK, BLOCK)
            if u == 0:
                p0 = pl.multiple_of(jnp.maximum(r0 - BLOCK, 0), BLOCK)
                o1, l1 = pair(q1_ref[pl.ds(r0, BLOCK), :],
                              jnp.concatenate([k1_ref[pl.ds(p0, BLOCK), :], k1_ref[pl.ds(r0, BLOCK), :]], axis=0),
                              jnp.concatenate([v1_ref[pl.ds(p0, BLOCK), :], v1_ref[pl.ds(r0, BLOCK), :]], axis=0),
                              bias_ref[jnp.minimum(it, 1)])
            else:
                o1, l1 = banded(q1_ref, k1_ref, v1_ref, r0, bias_ref[1])
            o2, l2 = o2_s[pl.ds(r0, BLOCK), :], l2_s[pl.ds(r0, BLOCK), :]
            o3, l3 = o3_s[pl.ds(r0, BLOCK), :], l3_s[pl.ds(r0, BLOCK), :]
            mx = jnp.maximum(jnp.maximum(l1, l2), l3)
            e1, e2, e3 = jnp.exp2(l1 - mx), jnp.exp2(l2 - mx), jnp.exp2(l3 - mx)
            merged = (e1 * o1 + e2 * o2 + e3 * o3) / (e1 + e2 + e3)
            o_ref[pl.ds(r0, BLOCK), :] = merged.astype(BF16)
        return carry

    lax.fori_loop(0, t // (nb * BLOCK), g1, 0)


def _attn_b(b1, b2, b3):
    bsz, t, w3 = b1.shape
    gw = w3 // 3
    npair = gw // LANES

    def spec(part):
        return pl.BlockSpec((None, t, LANES), lambda b, j: (b, 0, part * npair + j))

    return pl.pallas_call(
        _attn_b_kernel,
        out_shape=jax.ShapeDtypeStruct((bsz, t, gw), BF16),
        grid=(bsz, npair),
        in_specs=[spec(0), spec(1), spec(2)] * 3,
        out_specs=pl.BlockSpec((None, t, LANES), lambda b, j: (b, 0, j)),
        scratch_shapes=[pltpu.VMEM((t, LANES), F32)] * 4
                       + [pltpu.VMEM((2, 2 * BLOCK, 2 * BLOCK), F32),
                          pltpu.VMEM((2 * BLOCK, BLOCK), F32)],
        compiler_params=pltpu.CompilerParams(dimension_semantics=("arbitrary", "arbitrary"),
                                             vmem_limit_bytes=VMEM_LIMIT),
        name="attn_b",
    )(b1, b1, b1, b2, b2, b2, b3, b3, b3)


def _layer_norm(h, g, b):
    mu = jnp.mean(h, axis=-1, keepdims=True)
    xc = h - mu
    var = jnp.mean(xc * xc, axis=-1, keepdims=True)
    return xc * lax.rsqrt(var + LN_EPS) * g + b


def _post_kernel(x_ref, oa_ref, ob_ref, scm_ref, shm_ref, gm_ref, shf_ref, scf_ref, gf_ref,
                 g1_ref, b1_ref, g2_ref, b2_ref,
                 wg_ref, wa_ref, wb_ref, wo_ref, wgu_ref, wd_ref, o_ref, *, alpha, d_ff):
    d = x_ref.shape[1]

    def mix(rows):
        x = x_ref[rows, :]
        u = (x * (1.0 + scm_ref[...]) + shm_ref[...]).astype(BF16)
        ga = jax.nn.sigmoid(jnp.dot(u, wg_ref[:, :d], preferred_element_type=F32))
        ya = jnp.dot(oa_ref[rows, :], wa_ref[...], preferred_element_type=F32)
        merged = ga * ya
        gb = jax.nn.sigmoid(jnp.dot(u, wg_ref[:, d:], preferred_element_type=F32))
        yb = jnp.dot(ob_ref[rows, :], wb_ref[...], preferred_element_type=F32)
        merged = (merged + gb * yb).astype(BF16)
        return jnp.dot(merged, wo_ref[...], preferred_element_type=F32)

    def norm1(rows, y):
        return _layer_norm(alpha * x_ref[rows, :] + (1.0 + gm_ref[...]) * y, g1_ref[...], b1_ref[...])

    def ffn(x1):
        u2 = (x1 * (1.0 + scf_ref[...]) + shf_ref[...]).astype(BF16)
        acc = jnp.zeros_like(x1)
        for c in range(d_ff // FF_CHUNK):
            c0 = c * FF_CHUNK
            hg = jnp.dot(u2, wgu_ref[:, c0:c0 + FF_CHUNK], preferred_element_type=F32)
            hu = jnp.dot(u2, wgu_ref[:, d_ff + c0:d_ff + c0 + FF_CHUNK], preferred_element_type=F32)
            act = (hg * jax.nn.sigmoid(hg) * hu).astype(BF16)
            acc = acc + jnp.dot(act, wd_ref[c0:c0 + FF_CHUNK, :], preferred_element_type=F32)
        return acc

    def norm2(rows, x1, acc):
        o_ref[rows, :] = _layer_norm(alpha * x1 + (1.0 + gf_ref[...]) * acc, g2_ref[...], b2_ref[...])

    subs = [slice(r0, r0 + SUB_POST) for r0 in range(0, x_ref.shape[0], SUB_POST)]
    ys = [mix(rows) for rows in subs]
    x1s, accs = [], []
    for i, rows in enumerate(subs):
        x1s.append(norm1(rows, ys[i]))
        accs.append(ffn(x1s[i]))
        if i > 0:
            norm2(subs[i - 1], x1s[i - 1], accs[i - 1])
    norm2(subs[-1], x1s[-1], accs[-1])


def _post(x, oa, ob, mod3, ln1_g, ln1_b, ln2_g, ln2_b, wg, wa, wb, wo, wgu, wd, alpha):
    bsz, t, d = x.shape
    tm = TM_POST
    tiles = t // tm
    d_ff = wd.shape[0]

    def bq(i):
        return i // tiles, i % tiles

    nat = lambda width: pl.BlockSpec((None, tm, width), lambda i: (bq(i)[0], bq(i)[1], 0))
    modspec = lambda k: pl.BlockSpec((None, 1, d), lambda i: (bq(i)[0], 0, k))
    vec = pl.BlockSpec((1, d), lambda i: (0, 0))
    in_specs = [nat(d), nat(oa.shape[2]), nat(ob.shape[2]),
                modspec(1), modspec(0), modspec(2), modspec(3), modspec(4), modspec(5),
                vec, vec, vec, vec,
                _resident(wg.shape), _resident(wa.shape), _resident(wb.shape),
                _resident(wo.shape), _resident(wgu.shape), _resident(wd.shape)]
    row = lambda v: v.reshape(1, d)
    return pl.pallas_call(
        functools.partial(_post_kernel, alpha=alpha, d_ff=d_ff),
        out_shape=jax.ShapeDtypeStruct((bsz, t, d), F32),
        grid=(bsz * tiles,),
        in_specs=in_specs,
        out_specs=nat(d),
        compiler_params=pltpu.CompilerParams(dimension_semantics=("arbitrary",),
                                             vmem_limit_bytes=VMEM_LIMIT),
        name="post",
    )(x, oa, ob, mod3, mod3, mod3, mod3, mod3, mod3,
      row(ln1_g), row(ln1_b), row(ln2_g), row(ln2_b), wg, wa, wb, wo, wgu, wd)


def kernel(x, c, positions, w_ada, b_ada, w_in, sinks, w_branch_a, w_branch_b, w_o,
           ln1_g, ln1_b, w_gate_up, w_down, ln2_g, ln2_b):
    depth = w_ada.shape[0]
    bsz, t, d = x.shape
    alpha = (2 * depth) ** 0.25
    a_q_w = A_Q_HEADS * HEAD_DIM
    a_kv_w = A_KV_HEADS * HEAD_DIM
    b_w = B_HEADS_PER_GROUP * len(B_PATTERNS) * HEAD_DIM
    cols = {"qa": 0, "qa_w": a_q_w, "ka": a_q_w, "kva_w": a_kv_w,
            "qb": a_q_w + 2 * a_kv_w, "kb": a_q_w + 2 * a_kv_w + b_w,
            "vb": a_q_w + 2 * a_kv_w + 2 * b_w, "bg_w": B_HEADS_PER_GROUP * HEAD_DIM}
    gate0 = a_q_w + 2 * a_kv_w + 3 * b_w

    ctab, stab = _rope_tables(positions)
    for l in range(depth):
        mod3 = _ada(c, w_ada[l], b_ada[l]).reshape(bsz, 1, 6 * d)
        w_bf = w_in[l].astype(BF16)
        qa, ka, va, b1, b2, b3 = _inproj(x, mod3, ctab, stab, w_bf[:, :gate0], cols)
        oa = _attn_a(sinks[l], qa, ka, va)
        ob = _attn_b(b1, b2, b3)
        x = _post(x, oa, ob, mod3, ln1_g[l], ln1_b[l], ln2_g[l], ln2_b[l],
                  w_bf[:, gate0:], w_branch_a[l].astype(BF16), w_branch_b[l].astype(BF16),
                  w_o[l].astype(BF16), w_gate_up[l].astype(BF16), w_down[l].astype(BF16), alpha)
    return x
```

```python
import functools

import jax
import jax.numpy as jnp
from jax import lax
from jax.experimental import pallas as pl
from jax.experimental.pallas import tpu as pltpu

F32 = jnp.float32
BF16 = jnp.bfloat16

HEAD_DIM = 64
HALF = HEAD_DIM // 2
BLOCK = 128
LANES = 128
A_Q_HEADS = 16
A_KV_HEADS = 2
A_GROUP = A_Q_HEADS // A_KV_HEADS
A_WINDOW = 128
B_PATTERNS = ((128, 1), (512, 4), (2048, 16))
B_HEADS_PER_GROUP = 8
ROPE_THETA = 10000.0
LN_EPS = 1e-5
NEG_INF = -1e30
LOG2E = 1.4426950408889634

TM_IN = 512
TM_POST = 512
SUB_POST = 512
FF_CHUNK = 256
BLOCKS_PER_ITER = 16
A_SLABS_PER_DOT = 2
VMEM_LIMIT = 56 * 1024 * 1024

_NT = (((1,), (1,)), ((), ()))


def _resident(shape):
    nd = len(shape)
    return pl.BlockSpec(shape, lambda *_: (0,) * nd, pipeline_mode=pl.Buffered(1))


def _ada_kernel(c_ref, w_ref, b_ref, o_ref):
    c = c_ref[...]
    act = (c * jax.nn.sigmoid(c)).astype(BF16)
    o_ref[...] = jnp.dot(act, w_ref[...].astype(BF16), preferred_element_type=F32) + b_ref[...]


def _ada(c, w, b):
    bsz, d = c.shape
    e = w.shape[1]
    tn = d
    return pl.pallas_call(
        _ada_kernel,
        out_shape=jax.ShapeDtypeStruct((bsz, e), F32),
        grid=(e // tn,),
        in_specs=[pl.BlockSpec((bsz, d), lambda j: (0, 0)),
                  pl.BlockSpec((d, tn), lambda j: (0, j)),
                  pl.BlockSpec((1, tn), lambda j: (0, j))],
        out_specs=pl.BlockSpec((bsz, tn), lambda j: (0, j)),
        compiler_params=pltpu.CompilerParams(dimension_semantics=("arbitrary",)),
        name="ada",
    )(c, w, b.reshape(1, e))


def _rope_table_kernel(pos_ref, inv_ref, c_ref, s_ref):
    ang = pos_ref[...] * inv_ref[...]
    cos = jnp.cos(ang)
    sin = jnp.sin(ang)
    reps = LANES // HEAD_DIM
    c_ref[...] = jnp.concatenate([cos, cos] * reps, axis=0).T
    s_ref[...] = jnp.concatenate([-sin, sin] * reps, axis=0).T


def _rope_tables(positions):
    bsz, t = positions.shape
    inv = ROPE_THETA ** (-jnp.arange(HALF, dtype=F32) / HALF)
    tab = pl.BlockSpec((None, t, LANES), lambda b: (b, 0, 0))
    return pl.pallas_call(
        _rope_table_kernel,
        out_shape=[jax.ShapeDtypeStruct((bsz, t, LANES), F32)] * 2,
        grid=(bsz,),
        in_specs=[pl.BlockSpec((None, 1, t), lambda b: (b, 0, 0)),
                  pl.BlockSpec((HALF, 1), lambda b: (0, 0))],
        out_specs=[tab, tab],
        compiler_params=pltpu.CompilerParams(dimension_semantics=("arbitrary",)),
        name="rope_tab",
    )(positions.astype(F32).reshape(bsz, 1, t), inv.reshape(HALF, 1))


def _rope_cols(y, ctab, stab, first_half, scale):
    outs = []
    for j in range(y.shape[1] // LANES):
        slab = y[:, j * LANES:(j + 1) * LANES]
        swapped = jnp.where(first_half,
                            pltpu.roll(slab, LANES - HALF, axis=1),
                            pltpu.roll(slab, HALF, axis=1))
        r = slab * ctab + swapped * stab
        if scale != 1.0:
            r = r * scale
        outs.append(r.astype(BF16))
    return outs[0] if len(outs) == 1 else jnp.concatenate(outs, axis=1)


def _inproj_kernel(x_ref, sc_ref, sh_ref, cn_ref, sn_ref, w_ref,
                   qa_ref, ka_ref, va_ref, b1_ref, b2_ref, b3_ref, us_ref, *, cols):
    tm, d = x_ref.shape
    nslab = d // LANES
    lane = lax.broadcasted_iota(jnp.int32, (tm, LANES), 1)
    first_half = (lane & HALF) == 0
    qscale = HEAD_DIM ** -0.5 * LOG2E

    uf = x_ref[...] * (1.0 + sc_ref[...]) + sh_ref[...]
    un = uf.astype(BF16)
    for j in range(nslab):
        us_ref[j] = uf[:, j * LANES:(j + 1) * LANES]

    def strided_rows(load, r):
        per = tm // r
        return jnp.concatenate([load(rho, per, r) for rho in range(r)], axis=0)

    def u_dilated(r):
        def load(rho, per, stride):
            return jnp.concatenate(
                [us_ref[j, pl.ds(rho, per, stride=stride), :] for j in range(nslab)], axis=1)
        return strided_rows(load, r).astype(BF16)

    def tab_dilated(ref, r):
        return strided_rows(lambda rho, per, stride: ref[pl.ds(rho, per, stride=stride), :], r)

    def proj(u, c0, width):
        return jnp.dot(u, w_ref[:, c0:c0 + width], preferred_element_type=F32)

    cn, sn = cn_ref[...], sn_ref[...]
    half_q = cols["qa_w"] // 2
    for h in range(2):
        y = proj(un, cols["qa"] + h * half_q, half_q)
        qa_ref[:, h * half_q:(h + 1) * half_q] = _rope_cols(y, cn, sn, first_half, qscale)
    kvw = cols["kva_w"]
    y = proj(un, cols["ka"], 2 * kvw)
    ka_ref[...] = _rope_cols(y[:, :kvw], cn, sn, first_half, 1.0)
    va_ref[...] = y[:, kvw:].astype(BF16)

    gw = cols["bg_w"]

    def group(u, g, ctab, stab, store):
        y = proj(u, cols["qb"] + g * gw, gw)
        store(0, _rope_cols(y, ctab, stab, first_half, qscale))
        y = proj(u, cols["kb"] + g * gw, gw)
        store(gw, _rope_cols(y, ctab, stab, first_half, 1.0))
        y = proj(u, cols["vb"] + g * gw, gw)
        store(2 * gw, y.astype(BF16))

    def store_nat(c0, val):
        b1_ref[:, c0:c0 + gw] = val

    def store_res(ref):
        nres, per = ref.shape[0], ref.shape[1]

        def store(c0, val):
            for rho in range(nres):
                ref[rho, :, c0:c0 + gw] = val[rho * per:(rho + 1) * per]
        return store

    group(un, 0, cn, sn, store_nat)
    for g, out_ref in ((1, b2_ref), (2, b3_ref)):
        r = B_PATTERNS[g][1]
        group(u_dilated(r), g, tab_dilated(cn_ref, r), tab_dilated(sn_ref, r), store_res(out_ref))


def _inproj(x, mod3, ctab, stab, w_bf, cols):
    bsz, t, d = x.shape
    tm = TM_IN
    tiles = t // tm
    r4, r16 = B_PATTERNS[1][1], B_PATTERNS[2][1]

    def bq(i):
        return i // tiles, i % tiles

    nat = lambda width: pl.BlockSpec((None, tm, width), lambda i: (bq(i)[0], bq(i)[1], 0))
    res = lambda r, width: pl.BlockSpec((None, r, tm // r, width), lambda i: (bq(i)[0], 0, bq(i)[1], 0))
    in_specs = [
        nat(d),
        pl.BlockSpec((None, 1, d), lambda i: (bq(i)[0], 0, 1)),
        pl.BlockSpec((None, 1, d), lambda i: (bq(i)[0], 0, 0)),
        nat(LANES), nat(LANES),
        _resident(w_bf.shape),
    ]
    gw3 = 3 * cols["bg_w"]
    out_shape = [jax.ShapeDtypeStruct((bsz, t, cols["qa_w"]), BF16),
                 jax.ShapeDtypeStruct((bsz, t, cols["kva_w"]), BF16),
                 jax.ShapeDtypeStruct((bsz, t, cols["kva_w"]), BF16),
                 jax.ShapeDtypeStruct((bsz, t, gw3), BF16),
                 jax.ShapeDtypeStruct((bsz, r4, t // r4, gw3), BF16),
                 jax.ShapeDtypeStruct((bsz, r16, t // r16, gw3), BF16)]
    out_specs = [nat(cols["qa_w"]), nat(cols["kva_w"]), nat(cols["kva_w"]),
                 nat(gw3), res(r4, gw3), res(r16, gw3)]
    qa, ka, va, b1, b2, b3 = pl.pallas_call(
        functools.partial(_inproj_kernel, cols=cols),
        out_shape=out_shape,
        grid=(bsz * tiles,),
        in_specs=in_specs,
        out_specs=out_specs,
        scratch_shapes=[pltpu.VMEM((d // LANES, tm, LANES), F32)],
        compiler_params=pltpu.CompilerParams(dimension_semantics=("arbitrary",),
                                             vmem_limit_bytes=VMEM_LIMIT),
        name="inproj",
    )(x, mod3, mod3, ctab, stab, w_bf)
    return qa, ka, va, b1, b2.reshape(bsz, t, gw3), b3.reshape(bsz, t, gw3)


def _band_bias(keys, n_back, prev_valid):
    qi = lax.broadcasted_iota(jnp.int32, (BLOCK, keys), 0)
    c = lax.broadcasted_iota(jnp.int32, (BLOCK, keys), 1)
    dist = qi + (keys - BLOCK) - c
    ok = (dist >= 0) & (dist <= n_back)
    if not prev_valid:
        ok = ok & (c >= keys - BLOCK)
    return jnp.where(ok, 0.0, NEG_INF).astype(F32)


def _lane_consts():
    lane = lax.broadcasted_iota(jnp.int32, (1, LANES), 1)
    mlo = (lane < HEAD_DIM).astype(BF16)
    mhi = (lane >= HEAD_DIM).astype(BF16)
    lane_lo = lax.broadcasted_iota(jnp.int32, (BLOCK, LANES), 1) < HEAD_DIM
    return mlo, mhi, lane_lo


def _stack_scores(qs, kc, mlo, mhi):
    parts = []
    for q in qs:
        parts += [q * mlo, q * mhi]
    return lax.dot_general(jnp.concatenate(parts, axis=0), kc, _NT, preferred_element_type=F32)


def _stack_softmax(s, sinks2):
    heads = [s[h * BLOCK:(h + 1) * BLOCK] for h in range(s.shape[0] // BLOCK)]
    ms = [jnp.max(h, axis=-1, keepdims=True) for h in heads]
    if sinks2 is not None:
        ms = [jnp.maximum(m, sk) for m, sk in zip(ms, sinks2)]
    p = jnp.concatenate([jnp.exp2(h - m) for h, m in zip(heads, ms)], axis=0).astype(BF16)
    return p, ms


def _stack_values(p, vc, ms, lane_lo, sinks2, want_lse):
    ones = jnp.ones(vc.shape, BF16)
    r = jnp.dot(p, jnp.concatenate([vc, ones], axis=1), preferred_element_type=F32)
    outs = []
    for j in range(len(ms) // 2):
        e, o = r[2 * j * BLOCK:(2 * j + 1) * BLOCK], r[(2 * j + 1) * BLOCK:(2 * j + 2) * BLOCK]
        me, mo = ms[2 * j], ms[2 * j + 1]
        num = jnp.where(lane_lo, e[:, :LANES], o[:, :LANES])
        den = jnp.where(lane_lo, e[:, LANES:], o[:, LANES:])
        if sinks2 is not None:
            den = den + jnp.where(lane_lo, jnp.exp2(sinks2[2 * j] - me), jnp.exp2(sinks2[2 * j + 1] - mo))
        out = num / den
        outs.append((out, jnp.where(lane_lo, me, mo) + jnp.log2(den)) if want_lse else out)
    return outs


def _pair_block(q, kc, vc, bias, mlo, mhi, lane_lo):
    p, ms = _stack_softmax(_stack_scores([q], kc, mlo, mhi) + bias, None)
    return _stack_values(p, vc, ms, lane_lo, None, True)[0]


def _window_block(qs, kc, vc, prev_bias, tri, tri_bf, ntri_bf, mlo, mhi, lane_lo, sinks2):
    s = _stack_scores(qs, kc, mlo, mhi)
    merged = jnp.where(tri, s[:, BLOCK:], s[:, :BLOCK] + prev_bias)
    p, ms = _stack_softmax(merged, sinks2)
    p2 = jnp.concatenate([p * ntri_bf, p * tri_bf], axis=1)
    return _stack_values(p2, vc, ms, lane_lo, sinks2, False)


def _attn_a_kernel(sink_ref, q_ref, k_ref, v_ref, o_ref, kd_ref, vd_ref):
    t = q_ref.shape[0]
    nblk = t // BLOCK
    nslab = A_GROUP // 2
    assert A_WINDOW == BLOCK
    mlo, mhi, lane_lo = _lane_consts()
    lane_lo_t = lax.broadcasted_iota(jnp.int32, (t, LANES), 1) < HEAD_DIM

    for src, dup in ((k_ref, kd_ref), (v_ref, vd_ref)):
        val = src[...].astype(F32)
        swp = pltpu.roll(val, HEAD_DIM, axis=1)
        dup[0] = jnp.where(lane_lo_t, val, swp).astype(BF16)
        dup[1] = jnp.where(lane_lo_t, swp, val).astype(BF16)
    stack = 2 * A_SLABS_PER_DOT * BLOCK
    qi = lax.broadcasted_iota(jnp.int32, (stack, BLOCK), 0) & (BLOCK - 1)
    tri = lax.broadcasted_iota(jnp.int32, (stack, BLOCK), 1) <= qi
    tri_bf = tri.astype(F32).astype(BF16)
    ntri_bf = 1.0 - tri_bf

    def block(i, carry):
        r0 = pl.multiple_of(i * BLOCK, BLOCK)
        p0 = pl.multiple_of(jnp.maximum(i - 1, 0) * BLOCK, BLOCK)
        prev_bias = jnp.where(i > 0, 0.0, NEG_INF).astype(F32)
        for hk in range(A_KV_HEADS):
            kc = jnp.concatenate([kd_ref[hk, pl.ds(p0, BLOCK), :], kd_ref[hk, pl.ds(r0, BLOCK), :]], axis=0)
            vc = jnp.concatenate([vd_ref[hk, pl.ds(p0, BLOCK), :], vd_ref[hk, pl.ds(r0, BLOCK), :]], axis=0)
            for s0 in range(0, nslab, A_SLABS_PER_DOT):
                slabs = [hk * nslab + s0 + u for u in range(A_SLABS_PER_DOT)]
                cols = [slice(sl * LANES, (sl + 1) * LANES) for sl in slabs]
                sinks2 = [sink_ref[2 * sl + h] * LOG2E for sl in slabs for h in range(2)]
                outs = _window_block([q_ref[pl.ds(r0, BLOCK), c] for c in cols], kc, vc, prev_bias,
                                     tri, tri_bf, ntri_bf, mlo, mhi, lane_lo, sinks2)
                for c, out in zip(cols, outs):
                    o_ref[pl.ds(r0, BLOCK), c] = out.astype(BF16)
        return carry

    lax.fori_loop(0, nblk, block, 0, unroll=8)


def _attn_a(sinks, qa, ka, va):
    bsz, t, qw = qa.shape
    kw = ka.shape[2]
    seq = lambda width: pl.BlockSpec((None, t, width), lambda b: (b, 0, 0))
    return pl.pallas_call(
        _attn_a_kernel,
        out_shape=jax.ShapeDtypeStruct((bsz, t, qw), BF16),
        grid=(bsz,),
        in_specs=[pl.BlockSpec(memory_space=pltpu.SMEM), seq(qw), seq(kw), seq(kw)],
        out_specs=seq(qw),
        scratch_shapes=[pltpu.VMEM((A_KV_HEADS, t, LANES), BF16),
                        pltpu.VMEM((A_KV_HEADS, t, LANES), BF16)],
        compiler_params=pltpu.CompilerParams(dimension_semantics=("arbitrary",),
                                             vmem_limit_bytes=VMEM_LIMIT),
        name="attn_a",
    )(sinks, qa, ka, va)


def _attn_b_kernel(q1_ref, k1_ref, v1_ref, q2_ref, k2_ref, v2_ref, q3_ref, k3_ref, v3_ref,
                   o_ref, o1_s, l1_s, o3_s, l3_s, fin_s, bias_ref, bias1_ref):
    t = q1_ref.shape[0]
    mlo, mhi, lane_lo = _lane_consts()
    pair = functools.partial(_pair_block, mlo=mlo, mhi=mhi, lane_lo=lane_lo)
    (w1, _), (w4, r4), (w16, r16) = B_PATTERNS
    nb4 = t // r4 // BLOCK
    assert w1 == w4 // r4 == w16 // r16 and t // r16 == BLOCK and r16 % r4 == 0

    twice = lambda b: jnp.concatenate([b, b], axis=0)
    bias_ref[...] = twice(_band_bias(2 * BLOCK, w1, True))
    bias1_ref[...] = twice(_band_bias(BLOCK, w1, True))

    def first_block(q_ref, k_ref, v_ref, r0):
        k = k_ref[pl.ds(r0, BLOCK), :]
        s = _stack_scores([q_ref[pl.ds(r0, BLOCK), :]], jnp.concatenate([k, k], axis=0), mlo, mhi)
        p, ms = _stack_softmax(s[:, :BLOCK] + bias1_ref[...], None)
        return _stack_values(p, v_ref[pl.ds(r0, BLOCK), :], ms, lane_lo, None, True)[0]

    def banded(q_ref, k_ref, v_ref, r0, bias):
        return pair(q_ref[pl.ds(r0, BLOCK), :], k_ref[pl.ds(r0 - BLOCK, 2 * BLOCK), :],
                    v_ref[pl.ds(r0 - BLOCK, 2 * BLOCK), :], bias)

    rows4 = t // r4

    for rho in range(r16):
        out, lse = first_block(q3_ref, k3_ref, v3_ref, rho * BLOCK)
        dst = pl.ds((rho % r4) * rows4 + rho // r4, BLOCK, stride=r4)
        o3_s[dst, :] = out
        l3_s[dst, :] = lse

    for i in range(t // BLOCK):
        r0 = i * BLOCK
        if i == 0:
            out, lse = first_block(q1_ref, k1_ref, v1_ref, r0)
        else:
            out, lse = banded(q1_ref, k1_ref, v1_ref, r0, bias_ref[...])
        o1_s[pl.ds(r0, BLOCK), :] = out
        l1_s[pl.ds(r0, BLOCK), :] = lse

    for idx in range(r4 * nb4):
        rho, sb = idx // nb4, idx % nb4
        r0 = idx * BLOCK
        if sb == 0:
            o2, l2 = first_block(q2_ref, k2_ref, v2_ref, r0)
        else:
            o2, l2 = banded(q2_ref, k2_ref, v2_ref, r0, bias_ref[...])
        nat = pl.ds(sb * (BLOCK * r4) + rho, BLOCK, stride=r4)
        o1, l1 = o1_s[nat, :], l1_s[nat, :]
        o3, l3 = o3_s[pl.ds(r0, BLOCK), :], l3_s[pl.ds(r0, BLOCK), :]
        mx = jnp.maximum(jnp.maximum(l1, l2), l3)
        e1, e2, e3 = jnp.exp2(l1 - mx), jnp.exp2(l2 - mx), jnp.exp2(l3 - mx)
        fin_s[nat, :] = (e1 * o1 + e2 * o2 + e3 * o3) / (e1 + e2 + e3)

    o_ref[...] = fin_s[...].astype(BF16)


def _attn_b(b1, b2, b3):
    bsz, t, w3 = b1.shape
    gw = w3 // 3
    npair = gw // LANES

    def spec(part):
        return pl.BlockSpec((None, t, LANES), lambda b, j: (b, 0, part * npair + j))

    return pl.pallas_call(
        _attn_b_kernel,
        out_shape=jax.ShapeDtypeStruct((bsz, t, gw), BF16),
        grid=(bsz, npair),
        in_specs=[spec(0), spec(1), spec(2)] * 3,
        out_specs=pl.BlockSpec((None, t, LANES), lambda b, j: (b, 0, j)),
        scratch_shapes=[pltpu.VMEM((t, LANES), F32)] * 5
                       + [pltpu.VMEM((2 * BLOCK, 2 * BLOCK), F32),
                          pltpu.VMEM((2 * BLOCK, BLOCK), F32)],
        compiler_params=pltpu.CompilerParams(dimension_semantics=("arbitrary", "arbitrary"),
                                             vmem_limit_bytes=VMEM_LIMIT),
        name="attn_b",
    )(b1, b1, b1, b2, b2, b2, b3, b3, b3)


def _layer_norm(h, g, b):
    mu = jnp.mean(h, axis=-1, keepdims=True)
    xc = h - mu
    var = jnp.mean(xc * xc, axis=-1, keepdims=True)
    return xc * lax.rsqrt(var + LN_EPS) * g + b


def _post_kernel(x_ref, oa_ref, ob_ref, scm_ref, shm_ref, gm_ref, shf_ref, scf_ref, gf_ref,
                 g1_ref, b1_ref, g2_ref, b2_ref,
                 wg_ref, wa_ref, wb_ref, wo_ref, wgu_ref, wd_ref, o_ref, *, alpha, d_ff):
    d = x_ref.shape[1]

    def mix(rows):
        x = x_ref[rows, :]
        u = (x * (1.0 + scm_ref[...]) + shm_ref[...]).astype(BF16)
        ga = jax.nn.sigmoid(jnp.dot(u, wg_ref[:, :d], preferred_element_type=F32))
        ya = jnp.dot(oa_ref[rows, :], wa_ref[...], preferred_element_type=F32)
        merged = ga * ya
        gb = jax.nn.sigmoid(jnp.dot(u, wg_ref[:, d:], preferred_element_type=F32))
        yb = jnp.dot(ob_ref[rows, :], wb_ref[...], preferred_element_type=F32)
        merged = (merged + gb * yb).astype(BF16)
        return jnp.dot(merged, wo_ref[...], preferred_element_type=F32)

    def norm1(rows, y):
        return _layer_norm(alpha * x_ref[rows, :] + (1.0 + gm_ref[...]) * y, g1_ref[...], b1_ref[...])

    def ffn(x1):
        u2 = (x1 * (1.0 + scf_ref[...]) + shf_ref[...]).astype(BF16)
        acc = jnp.zeros_like(x1)
        for c in range(d_ff // FF_CHUNK):
            c0 = c * FF_CHUNK
            hg = jnp.dot(u2, wgu_ref[:, c0:c0 + FF_CHUNK], preferred_element_type=F32)
            hu = jnp.dot(u2, wgu_ref[:, d_ff + c0:d_ff + c0 + FF_CHUNK], preferred_element_type=F32)
            act = (hg * jax.nn.sigmoid(hg) * hu).astype(BF16)
            acc = acc + jnp.dot(act, wd_ref[c0:c0 + FF_CHUNK, :], preferred_element_type=F32)
        return acc

    def norm2(rows, x1, acc):
        o_ref[rows, :] = _layer_norm(alpha * x1 + (1.0 + gf_ref[...]) * acc, g2_ref[...], b2_ref[...])

    subs = [slice(r0, r0 + SUB_POST) for r0 in range(0, x_ref.shape[0], SUB_POST)]
    ys = [mix(rows) for rows in subs]
    x1s, accs = [], []
    for i, rows in enumerate(subs):
        x1s.append(norm1(rows, ys[i]))
        accs.append(ffn(x1s[i]))
        if i > 0:
            norm2(subs[i - 1], x1s[i - 1], accs[i - 1])
    norm2(subs[-1], x1s[-1], accs[-1])


def _post(x, oa, ob, mod3, ln1_g, ln1_b, ln2_g, ln2_b, wg, wa, wb, wo, wgu, wd, alpha):
    bsz, t, d = x.shape
    tm = TM_POST
    tiles = t // tm
    d_ff = wd.shape[0]

    def bq(i):
        return i // tiles, i % tiles

    nat = lambda width: pl.BlockSpec((None, tm, width), lambda i: (bq(i)[0], bq(i)[1], 0))
    modspec = lambda k: pl.BlockSpec((None, 1, d), lambda i: (bq(i)[0], 0, k))
    vec = pl.BlockSpec((1, d), lambda i: (0, 0))
    in_specs = [nat(d), nat(oa.shape[2]), nat(ob.shape[2]),
                modspec(1), modspec(0), modspec(2), modspec(3), modspec(4), modspec(5),
                vec, vec, vec, vec,
                _resident(wg.shape), _resident(wa.shape), _resident(wb.shape),
                _resident(wo.shape), _resident(wgu.shape), _resident(wd.shape)]
    row = lambda v: v.reshape(1, d)
    return pl.pallas_call(
        functools.partial(_post_kernel, alpha=alpha, d_ff=d_ff),
        out_shape=jax.ShapeDtypeStruct((bsz, t, d), F32),
        grid=(bsz * tiles,),
        in_specs=in_specs,
        out_specs=nat(d),
        compiler_params=pltpu.CompilerParams(dimension_semantics=("arbitrary",),
                                             vmem_limit_bytes=VMEM_LIMIT),
        name="post",
    )(x, oa, ob, mod3, mod3, mod3, mod3, mod3, mod3,
      row(ln1_g), row(ln1_b), row(ln2_g), row(ln2_b), wg, wa, wb, wo, wgu, wd)


def kernel(x, c, positions, w_ada, b_ada, w_in, sinks, w_branch_a, w_branch_b, w_o,
           ln1_g, ln1_b, w_gate_up, w_down, ln2_g, ln2_b):
    depth = w_ada.shape[0]
    bsz, t, d = x.shape
    alpha = (2 * depth) ** 0.25
    a_q_w = A_Q_HEADS * HEAD_DIM
    a_kv_w = A_KV_HEADS * HEAD_DIM
    b_w = B_HEADS_PER_GROUP * len(B_PATTERNS) * HEAD_DIM
    cols = {"qa": 0, "qa_w": a_q_w, "ka": a_q_w, "kva_w": a_kv_w,
            "qb": a_q_w + 2 * a_kv_w, "kb": a_q_w + 2 * a_kv_w + b_w,
            "vb": a_q_w + 2 * a_kv_w + 2 * b_w, "bg_w": B_HEADS_PER_GROUP * HEAD_DIM}
    gate0 = a_q_w + 2 * a_kv_w + 3 * b_w

    ctab, stab = _rope_tables(positions)
    for l in range(depth):
        mod3 = _ada(c, w_ada[l], b_ada[l]).reshape(bsz, 1, 6 * d)
        w_bf = w_in[l].astype(BF16)
        qa, ka, va, b1, b2, b3 = _inproj(x, mod3, ctab, stab, w_bf[:, :gate0], cols)
        oa = _attn_a(sinks[l], qa, ka, va)
        ob = _attn_b(b1, b2, b3)
        x = _post(x, oa, ob, mod3, ln1_g[l], ln1_b[l], ln2_g[l], ln2_b[l],
                  w_bf[:, gate0:], w_branch_a[l].astype(BF16), w_branch_b[l].astype(BF16),
                  w_o[l].astype(BF16), w_gate_up[l].astype(BF16), w_down[l].astype(BF16), alpha)
    return x
```

```python
import functools

import jax
import jax.numpy as jnp
from jax import lax
from jax.experimental import pallas as pl
from jax.experimental.pallas import tpu as pltpu

F32 = jnp.float32
BF16 = jnp.bfloat16

HEAD_DIM = 64
HALF = HEAD_DIM // 2
BLOCK = 128
LANES = 128
A_Q_HEADS = 16
A_KV_HEADS = 2
A_GROUP = A_Q_HEADS // A_KV_HEADS
A_WINDOW = 128
B_PATTERNS = ((128, 1), (512, 4), (2048, 16))
B_HEADS_PER_GROUP = 8
ROPE_THETA = 10000.0
LN_EPS = 1e-5
NEG_INF = -1e30
LOG2E = 1.4426950408889634

TM_IN = 512
TM_POST = 1024
SUB_POST = 512
FF_CHUNK = 256
BLOCKS_PER_ITER = 16
A_SLABS_PER_DOT = 2
VMEM_BYTES = 64 * 1024 * 1024
VMEM_LIMIT = VMEM_BYTES - 8 * 1024 * 1024
VMEM_LIMIT_POST = VMEM_BYTES - 2 * 1024 * 1024

_NT = (((1,), (1,)), ((), ()))


def _resident(shape):
    nd = len(shape)
    return pl.BlockSpec(shape, lambda *_: (0,) * nd, pipeline_mode=pl.Buffered(1))


def _ada_kernel(c_ref, w_ref, b_ref, o_ref):
    c = c_ref[...]
    act = (c * jax.nn.sigmoid(c)).astype(BF16)
    o_ref[...] = jnp.dot(act, w_ref[...].astype(BF16), preferred_element_type=F32) + b_ref[...]


def _ada(c, w, b):
    bsz, d = c.shape
    e = w.shape[1]
    tn = d
    return pl.pallas_call(
        _ada_kernel,
        out_shape=jax.ShapeDtypeStruct((bsz, e), F32),
        grid=(e // tn,),
        in_specs=[pl.BlockSpec((bsz, d), lambda j: (0, 0)),
                  pl.BlockSpec((d, tn), lambda j: (0, j)),
                  pl.BlockSpec((1, tn), lambda j: (0, j))],
        out_specs=pl.BlockSpec((bsz, tn), lambda j: (0, j)),
        compiler_params=pltpu.CompilerParams(dimension_semantics=("arbitrary",)),
        name="ada",
    )(c, w, b.reshape(1, e))


def _rope_table_kernel(pos_ref, inv_ref, c_ref, s_ref):
    ang = pos_ref[...] * inv_ref[...]
    cos = jnp.cos(ang)
    sin = jnp.sin(ang)
    reps = LANES // HEAD_DIM
    c_ref[...] = jnp.concatenate([cos, cos] * reps, axis=0).T
    s_ref[...] = jnp.concatenate([-sin, sin] * reps, axis=0).T


def _rope_tables(positions):
    bsz, t = positions.shape
    inv = ROPE_THETA ** (-jnp.arange(HALF, dtype=F32) / HALF)
    tab = pl.BlockSpec((None, t, LANES), lambda b: (b, 0, 0))
    return pl.pallas_call(
        _rope_table_kernel,
        out_shape=[jax.ShapeDtypeStruct((bsz, t, LANES), F32)] * 2,
        grid=(bsz,),
        in_specs=[pl.BlockSpec((None, 1, t), lambda b: (b, 0, 0)),
                  pl.BlockSpec((HALF, 1), lambda b: (0, 0))],
        out_specs=[tab, tab],
        compiler_params=pltpu.CompilerParams(dimension_semantics=("arbitrary",)),
        name="rope_tab",
    )(positions.astype(F32).reshape(bsz, 1, t), inv.reshape(HALF, 1))


def _rope_cols(y, ctab, stab, first_half, scale):
    outs = []
    for j in range(y.shape[1] // LANES):
        slab = y[:, j * LANES:(j + 1) * LANES]
        swapped = jnp.where(first_half,
                            pltpu.roll(slab, LANES - HALF, axis=1),
                            pltpu.roll(slab, HALF, axis=1))
        r = slab * ctab + swapped * stab
        if scale != 1.0:
            r = r * scale
        outs.append(r.astype(BF16))
    return outs[0] if len(outs) == 1 else jnp.concatenate(outs, axis=1)


def _inproj_kernel(x_ref, sc_ref, sh_ref, cn_ref, sn_ref, w_ref,
                   qa_ref, ka_ref, va_ref, b1_ref, b2_ref, b3_ref, us_ref, *, cols):
    tm, d = x_ref.shape
    nslab = d // LANES
    lane = lax.broadcasted_iota(jnp.int32, (tm, LANES), 1)
    first_half = (lane & HALF) == 0
    qscale = HEAD_DIM ** -0.5 * LOG2E

    uf = x_ref[...] * (1.0 + sc_ref[...]) + sh_ref[...]
    un = uf.astype(BF16)
    for j in range(nslab):
        us_ref[j] = uf[:, j * LANES:(j + 1) * LANES]

    def strided_rows(load, r):
        per = tm // r
        return jnp.concatenate([load(rho, per, r) for rho in range(r)], axis=0)

    def u_dilated(r):
        def load(rho, per, stride):
            return jnp.concatenate(
                [us_ref[j, pl.ds(rho, per, stride=stride), :] for j in range(nslab)], axis=1)
        return strided_rows(load, r).astype(BF16)

    def tab_dilated(ref, r):
        return strided_rows(lambda rho, per, stride: ref[pl.ds(rho, per, stride=stride), :], r)

    def proj(u, c0, width):
        return jnp.dot(u, w_ref[:, c0:c0 + width], preferred_element_type=F32)

    cn, sn = cn_ref[...], sn_ref[...]
    half_q = cols["qa_w"] // 2
    for h in range(2):
        y = proj(un, cols["qa"] + h * half_q, half_q)
        qa_ref[:, h * half_q:(h + 1) * half_q] = _rope_cols(y, cn, sn, first_half, qscale)
    kvw = cols["kva_w"]
    y = proj(un, cols["ka"], 2 * kvw)
    ka_ref[...] = _rope_cols(y[:, :kvw], cn, sn, first_half, 1.0)
    va_ref[...] = y[:, kvw:].astype(BF16)

    gw = cols["bg_w"]

    def group(u, g, ctab, stab, store):
        y = proj(u, cols["qb"] + g * gw, gw)
        store(0, _rope_cols(y, ctab, stab, first_half, qscale))
        y = proj(u, cols["kb"] + g * gw, gw)
        store(gw, _rope_cols(y, ctab, stab, first_half, 1.0))
        y = proj(u, cols["vb"] + g * gw, gw)
        store(2 * gw, y.astype(BF16))

    def store_nat(c0, val):
        b1_ref[:, c0:c0 + gw] = val

    def store_res(ref):
        nres, per = ref.shape[0], ref.shape[1]

        def store(c0, val):
            for rho in range(nres):
                ref[rho, :, c0:c0 + gw] = val[rho * per:(rho + 1) * per]
        return store

    group(un, 0, cn, sn, store_nat)
    for g, out_ref in ((1, b2_ref), (2, b3_ref)):
        r = B_PATTERNS[g][1]
        group(u_dilated(r), g, tab_dilated(cn_ref, r), tab_dilated(sn_ref, r), store_res(out_ref))


def _inproj(x, mod3, ctab, stab, w_bf, cols):
    bsz, t, d = x.shape
    tm = TM_IN
    tiles = t // tm
    r4, r16 = B_PATTERNS[1][1], B_PATTERNS[2][1]

    def bq(i):
        return i // tiles, i % tiles

    nat = lambda width: pl.BlockSpec((None, tm, width), lambda i: (bq(i)[0], bq(i)[1], 0))
    res = lambda r, width: pl.BlockSpec((None, r, tm // r, width), lambda i: (bq(i)[0], 0, bq(i)[1], 0))
    in_specs = [
        nat(d),
        pl.BlockSpec((None, 1, d), lambda i: (bq(i)[0], 0, 1)),
        pl.BlockSpec((None, 1, d), lambda i: (bq(i)[0], 0, 0)),
        nat(LANES), nat(LANES),
        _resident(w_bf.shape),
    ]
    gw3 = 3 * cols["bg_w"]
    out_shape = [jax.ShapeDtypeStruct((bsz, t, cols["qa_w"]), BF16),
                 jax.ShapeDtypeStruct((bsz, t, cols["kva_w"]), BF16),
                 jax.ShapeDtypeStruct((bsz, t, cols["kva_w"]), BF16),
                 jax.ShapeDtypeStruct((bsz, t, gw3), BF16),
                 jax.ShapeDtypeStruct((bsz, r4, t // r4, gw3), BF16),
                 jax.ShapeDtypeStruct((bsz, r16, t // r16, gw3), BF16)]
    out_specs = [nat(cols["qa_w"]), nat(cols["kva_w"]), nat(cols["kva_w"]),
                 nat(gw3), res(r4, gw3), res(r16, gw3)]
    qa, ka, va, b1, b2, b3 = pl.pallas_call(
        functools.partial(_inproj_kernel, cols=cols),
        out_shape=out_shape,
        grid=(bsz * tiles,),
        in_specs=in_specs,
        out_specs=out_specs,
        scratch_shapes=[pltpu.VMEM((d // LANES, tm, LANES), F32)],
        compiler_params=pltpu.CompilerParams(dimension_semantics=("arbitrary",),
                                             vmem_limit_bytes=VMEM_LIMIT),
        name="inproj",
    )(x, mod3, mod3, ctab, stab, w_bf)
    return qa, ka, va, b1, b2.reshape(bsz, t, gw3), b3.reshape(bsz, t, gw3)


def _band_bias(keys, n_back, prev_valid):
    qi = lax.broadcasted_iota(jnp.int32, (BLOCK, keys), 0)
    c = lax.broadcasted_iota(jnp.int32, (BLOCK, keys), 1)
    dist = qi + (keys - BLOCK) - c
    ok = (dist >= 0) & (dist <= n_back)
    if not prev_valid:
        ok = ok & (c >= keys - BLOCK)
    return jnp.where(ok, 0.0, NEG_INF).astype(F32)


def _lane_consts():
    lane = lax.broadcasted_iota(jnp.int32, (1, LANES), 1)
    mlo = (lane < HEAD_DIM).astype(BF16)
    mhi = (lane >= HEAD_DIM).astype(BF16)
    lane_lo = lax.broadcasted_iota(jnp.int32, (BLOCK, LANES), 1) < HEAD_DIM
    return mlo, mhi, lane_lo


def _stack_scores(qs, kc, mlo, mhi):
    parts = []
    for q in qs:
        parts += [q * mlo, q * mhi]
    return lax.dot_general(jnp.concatenate(parts, axis=0), kc, _NT, preferred_element_type=F32)


def _stack_softmax(s, sinks2):
    heads = [s[h * BLOCK:(h + 1) * BLOCK] for h in range(s.shape[0] // BLOCK)]
    ms = [jnp.max(h, axis=-1, keepdims=True) for h in heads]
    if sinks2 is not None:
        ms = [jnp.maximum(m, sk) for m, sk in zip(ms, sinks2)]
    p = jnp.concatenate([jnp.exp2(h - m) for h, m in zip(heads, ms)], axis=0).astype(BF16)
    return p, ms


def _stack_values(p, vc, ms, lane_lo, sinks2, want_lse):
    ones = jnp.ones(vc.shape, BF16)
    r = jnp.dot(p, jnp.concatenate([vc, ones], axis=1), preferred_element_type=F32)
    outs = []
    for j in range(len(ms) // 2):
        e, o = r[2 * j * BLOCK:(2 * j + 1) * BLOCK], r[(2 * j + 1) * BLOCK:(2 * j + 2) * BLOCK]
        me, mo = ms[2 * j], ms[2 * j + 1]
        num = jnp.where(lane_lo, e[:, :LANES], o[:, :LANES])
        den = jnp.where(lane_lo, e[:, LANES:], o[:, LANES:])
        if sinks2 is not None:
            den = den + jnp.where(lane_lo, jnp.exp2(sinks2[2 * j] - me), jnp.exp2(sinks2[2 * j + 1] - mo))
        out = num / den
        outs.append((out, jnp.where(lane_lo, me, mo) + jnp.log2(den)) if want_lse else out)
    return outs


def _pair_block(q, kc, vc, bias, mlo, mhi, lane_lo):
    p, ms = _stack_softmax(_stack_scores([q], kc, mlo, mhi) + bias, None)
    return _stack_values(p, vc, ms, lane_lo, None, True)[0]


def _window_block(qs, kc, vc, prev_bias, tri, tri_bf, ntri_bf, mlo, mhi, lane_lo, sinks2):
    s = _stack_scores(qs, kc, mlo, mhi)
    merged = jnp.where(tri, s[:, BLOCK:], s[:, :BLOCK] + prev_bias)
    p, ms = _stack_softmax(merged, sinks2)
    p2 = jnp.concatenate([p * ntri_bf, p * tri_bf], axis=1)
    return _stack_values(p2, vc, ms, lane_lo, sinks2, False)


def _attn_a_kernel(sink_ref, q_ref, k_ref, v_ref, o_ref, kd_ref, vd_ref):
    t = q_ref.shape[0]
    nblk = t // BLOCK
    nslab = A_GROUP // 2
    assert A_WINDOW == BLOCK
    mlo, mhi, lane_lo = _lane_consts()
    lane_lo_t = lax.broadcasted_iota(jnp.int32, (t, LANES), 1) < HEAD_DIM

    for src, dup in ((k_ref, kd_ref), (v_ref, vd_ref)):
        val = src[...].astype(F32)
        swp = pltpu.roll(val, HEAD_DIM, axis=1)
        dup[0] = jnp.where(lane_lo_t, val, swp).astype(BF16)
        dup[1] = jnp.where(lane_lo_t, swp, val).astype(BF16)
    stack = 2 * A_SLABS_PER_DOT * BLOCK
    qi = lax.broadcasted_iota(jnp.int32, (stack, BLOCK), 0) & (BLOCK - 1)
    tri = lax.broadcasted_iota(jnp.int32, (stack, BLOCK), 1) <= qi
    tri_bf = tri.astype(F32).astype(BF16)
    ntri_bf = 1.0 - tri_bf

    def block(i, carry):
        r0 = pl.multiple_of(i * BLOCK, BLOCK)
        p0 = pl.multiple_of(jnp.maximum(i - 1, 0) * BLOCK, BLOCK)
        prev_bias = jnp.where(i > 0, 0.0, NEG_INF).astype(F32)
        for hk in range(A_KV_HEADS):
            kc = jnp.concatenate([kd_ref[hk, pl.ds(p0, BLOCK), :], kd_ref[hk, pl.ds(r0, BLOCK), :]], axis=0)
            vc = jnp.concatenate([vd_ref[hk, pl.ds(p0, BLOCK), :], vd_ref[hk, pl.ds(r0, BLOCK), :]], axis=0)
            for s0 in range(0, nslab, A_SLABS_PER_DOT):
                slabs = [hk * nslab + s0 + u for u in range(A_SLABS_PER_DOT)]
                cols = [slice(sl * LANES, (sl + 1) * LANES) for sl in slabs]
                sinks2 = [sink_ref[2 * sl + h] * LOG2E for sl in slabs for h in range(2)]
                outs = _window_block([q_ref[pl.ds(r0, BLOCK), c] for c in cols], kc, vc, prev_bias,
                                     tri, tri_bf, ntri_bf, mlo, mhi, lane_lo, sinks2)
                for c, out in zip(cols, outs):
                    o_ref[pl.ds(r0, BLOCK), c] = out.astype(BF16)
        return carry

    lax.fori_loop(0, nblk, block, 0, unroll=8)


def _attn_a(sinks, qa, ka, va):
    bsz, t, qw = qa.shape
    kw = ka.shape[2]
    seq = lambda width: pl.BlockSpec((None, t, width), lambda b: (b, 0, 0))
    return pl.pallas_call(
        _attn_a_kernel,
        out_shape=jax.ShapeDtypeStruct((bsz, t, qw), BF16),
        grid=(bsz,),
        in_specs=[pl.BlockSpec(memory_space=pltpu.SMEM), seq(qw), seq(kw), seq(kw)],
        out_specs=seq(qw),
        scratch_shapes=[pltpu.VMEM((A_KV_HEADS, t, LANES), BF16),
                        pltpu.VMEM((A_KV_HEADS, t, LANES), BF16)],
        compiler_params=pltpu.CompilerParams(dimension_semantics=("arbitrary",),
                                             vmem_limit_bytes=VMEM_LIMIT),
        name="attn_a",
    )(sinks, qa, ka, va)


def _attn_b_kernel(q1_ref, k1_ref, v1_ref, q2_ref, k2_ref, v2_ref, q3_ref, k3_ref, v3_ref,
                   o_ref, o1_s, l1_s, o3_s, l3_s, fin_s, bias_ref, bias1_ref):
    t = q1_ref.shape[0]
    mlo, mhi, lane_lo = _lane_consts()
    pair = functools.partial(_pair_block, mlo=mlo, mhi=mhi, lane_lo=lane_lo)
    (w1, _), (w4, r4), (w16, r16) = B_PATTERNS
    nb4 = t // r4 // BLOCK
    assert w1 == w4 // r4 == w16 // r16 and t // r16 == BLOCK and r16 % r4 == 0

    twice = lambda b: jnp.concatenate([b, b], axis=0)
    bias_ref[...] = twice(_band_bias(2 * BLOCK, w1, True))
    bias1_ref[...] = twice(_band_bias(BLOCK, w1, True))

    def first_block(q_ref, k_ref, v_ref, r0):
        k = k_ref[pl.ds(r0, BLOCK), :]
        s = _stack_scores([q_ref[pl.ds(r0, BLOCK), :]], jnp.concatenate([k, k], axis=0), mlo, mhi)
        p, ms = _stack_softmax(s[:, :BLOCK] + bias1_ref[...], None)
        return _stack_values(p, v_ref[pl.ds(r0, BLOCK), :], ms, lane_lo, None, True)[0]

    def banded(q_ref, k_ref, v_ref, r0, bias):
        return pair(q_ref[pl.ds(r0, BLOCK), :], k_ref[pl.ds(r0 - BLOCK, 2 * BLOCK), :],
                    v_ref[pl.ds(r0 - BLOCK, 2 * BLOCK), :], bias)

    rows4 = t // r4

    for rho in range(r16):
        out, lse = first_block(q3_ref, k3_ref, v3_ref, rho * BLOCK)
        dst = pl.ds((rho % r4) * rows4 + rho // r4, BLOCK, stride=r4)
        o3_s[dst, :] = out
        l3_s[dst, :] = lse

    for i in range(t // BLOCK):
        r0 = i * BLOCK
        if i == 0:
            out, lse = first_block(q1_ref, k1_ref, v1_ref, r0)
        else:
            out, lse = banded(q1_ref, k1_ref, v1_ref, r0, bias_ref[...])
        o1_s[pl.ds(r0, BLOCK), :] = out
        l1_s[pl.ds(r0, BLOCK), :] = lse

    for idx in range(r4 * nb4):
        rho, sb = idx // nb4, idx % nb4
        r0 = idx * BLOCK
        if sb == 0:
            o2, l2 = first_block(q2_ref, k2_ref, v2_ref, r0)
        else:
            o2, l2 = banded(q2_ref, k2_ref, v2_ref, r0, bias_ref[...])
        nat = pl.ds(sb * (BLOCK * r4) + rho, BLOCK, stride=r4)
        o1, l1 = o1_s[nat, :], l1_s[nat, :]
        o3, l3 = o3_s[pl.ds(r0, BLOCK), :], l3_s[pl.ds(r0, BLOCK), :]
        mx = jnp.maximum(jnp.maximum(l1, l2), l3)
        e1, e2, e3 = jnp.exp2(l1 - mx), jnp.exp2(l2 - mx), jnp.exp2(l3 - mx)
        fin_s[nat, :] = (e1 * o1 + e2 * o2 + e3 * o3) / (e1 + e2 + e3)

    o_ref[...] = fin_s[...].astype(BF16)


def _attn_b(b1, b2, b3):
    bsz, t, w3 = b1.shape
    gw = w3 // 3
    npair = gw // LANES

    def spec(part):
        return pl.BlockSpec((None, t, LANES), lambda b, j: (b, 0, part * npair + j))

    return pl.pallas_call(
        _attn_b_kernel,
        out_shape=jax.ShapeDtypeStruct((bsz, t, gw), BF16),
        grid=(bsz, npair),
        in_specs=[spec(0), spec(1), spec(2)] * 3,
        out_specs=pl.BlockSpec((None, t, LANES), lambda b, j: (b, 0, j)),
        scratch_shapes=[pltpu.VMEM((t, LANES), F32)] * 5
                       + [pltpu.VMEM((2 * BLOCK, 2 * BLOCK), F32),
                          pltpu.VMEM((2 * BLOCK, BLOCK), F32)],
        compiler_params=pltpu.CompilerParams(dimension_semantics=("arbitrary", "arbitrary"),
                                             vmem_limit_bytes=VMEM_LIMIT),
        name="attn_b",
    )(b1, b1, b1, b2, b2, b2, b3, b3, b3)


def _layer_norm(h, g, b):
    mu = jnp.mean(h, axis=-1, keepdims=True)
    xc = h - mu
    var = jnp.mean(xc * xc, axis=-1, keepdims=True)
    return xc * lax.rsqrt(var + LN_EPS) * g + b


def _post_kernel(x_ref, oa_ref, ob_ref, scm_ref, shm_ref, gm_ref, shf_ref, scf_ref, gf_ref,
                 g1_ref, b1_ref, g2_ref, b2_ref,
                 wg_ref, wa_ref, wb_ref, wo_ref, wgu_ref, wd_ref, o_ref, *, alpha, d_ff):
    d = x_ref.shape[1]

    def mix(rows):
        x = x_ref[rows, :]
        u = (x * (1.0 + scm_ref[...]) + shm_ref[...]).astype(BF16)
        ga = jax.nn.sigmoid(jnp.dot(u, wg_ref[:, :d], preferred_element_type=F32))
        ya = jnp.dot(oa_ref[rows, :], wa_ref[...], preferred_element_type=F32)
        merged = ga * ya
        gb = jax.nn.sigmoid(jnp.dot(u, wg_ref[:, d:], preferred_element_type=F32))
        yb = jnp.dot(ob_ref[rows, :], wb_ref[...], preferred_element_type=F32)
        merged = (merged + gb * yb).astype(BF16)
        return jnp.dot(merged, wo_ref[...], preferred_element_type=F32)

    def norm1(rows, y):
        return _layer_norm(alpha * x_ref[rows, :] + (1.0 + gm_ref[...]) * y, g1_ref[...], b1_ref[...])

    def ffn(x1):
        u2 = (x1 * (1.0 + scf_ref[...]) + shf_ref[...]).astype(BF16)
        acc = jnp.zeros_like(x1)
        for c in range(d_ff // FF_CHUNK):
            c0 = c * FF_CHUNK
            hg = jnp.dot(u2, wgu_ref[:, c0:c0 + FF_CHUNK], preferred_element_type=F32)
            hu = jnp.dot(u2, wgu_ref[:, d_ff + c0:d_ff + c0 + FF_CHUNK], preferred_element_type=F32)
            act = (hg * jax.nn.sigmoid(hg) * hu).astype(BF16)
            acc = acc + jnp.dot(act, wd_ref[c0:c0 + FF_CHUNK, :], preferred_element_type=F32)
        return acc

    def norm2(rows, x1, acc):
        o_ref[rows, :] = _layer_norm(alpha * x1 + (1.0 + gf_ref[...]) * acc, g2_ref[...], b2_ref[...])

    subs = [slice(r0, r0 + SUB_POST) for r0 in range(0, x_ref.shape[0], SUB_POST)]
    ys = [mix(rows) for rows in subs]
    x1s, accs = [], []
    for i, rows in enumerate(subs):
        x1s.append(norm1(rows, ys[i]))
        accs.append(ffn(x1s[i]))
        if i > 0:
            norm2(subs[i - 1], x1s[i - 1], accs[i - 1])
    norm2(subs[-1], x1s[-1], accs[-1])


def _post(x, oa, ob, mod3, ln1_g, ln1_b, ln2_g, ln2_b, wg, wa, wb, wo, wgu, wd, alpha):
    bsz, t, d = x.shape
    tm = TM_POST
    tiles = t // tm
    d_ff = wd.shape[0]

    def bq(i):
        return i // tiles, i % tiles

    nat = lambda width: pl.BlockSpec((None, tm, width), lambda i: (bq(i)[0], bq(i)[1], 0))
    modspec = lambda k: pl.BlockSpec((None, 1, d), lambda i: (bq(i)[0], 0, k))
    vec = pl.BlockSpec((1, d), lambda i: (0, 0))
    in_specs = [nat(d), nat(oa.shape[2]), nat(ob.shape[2]),
                modspec(1), modspec(0), modspec(2), modspec(3), modspec(4), modspec(5),
                vec, vec, vec, vec,
                _resident(wg.shape), _resident(wa.shape), _resident(wb.shape),
                _resident(wo.shape), _resident(wgu.shape), _resident(wd.shape)]
    row = lambda v: v.reshape(1, d)
    return pl.pallas_call(
        functools.partial(_post_kernel, alpha=alpha, d_ff=d_ff),
        out_shape=jax.ShapeDtypeStruct((bsz, t, d), F32),
        grid=(bsz * tiles,),
        in_specs=in_specs,
        out_specs=nat(d),
        compiler_params=pltpu.CompilerParams(dimension_semantics=("arbitrary",),
                                             vmem_limit_bytes=VMEM_LIMIT_POST),
        name="post",
    )(x, oa, ob, mod3, mod3, mod3, mod3, mod3, mod3,
      row(ln1_g), row(ln1_b), row(ln2_g), row(ln2_b), wg, wa, wb, wo, wgu, wd)


def kernel(x, c, positions, w_ada, b_ada, w_in, sinks, w_branch_a, w_branch_b, w_o,
           ln1_g, ln1_b, w_gate_up, w_down, ln2_g, ln2_b):
    depth = w_ada.shape[0]
    bsz, t, d = x.shape
    alpha = (2 * depth) ** 0.25
    a_q_w = A_Q_HEADS * HEAD_DIM
    a_kv_w = A_KV_HEADS * HEAD_DIM
    b_w = B_HEADS_PER_GROUP * len(B_PATTERNS) * HEAD_DIM
    cols = {"qa": 0, "qa_w": a_q_w, "ka": a_q_w, "kva_w": a_kv_w,
            "qb": a_q_w + 2 * a_kv_w, "kb": a_q_w + 2 * a_kv_w + b_w,
            "vb": a_q_w + 2 * a_kv_w + 2 * b_w, "bg_w": B_HEADS_PER_GROUP * HEAD_DIM}
    gate0 = a_q_w + 2 * a_kv_w + 3 * b_w

    ctab, stab = _rope_tables(positions)
    for l in range(depth):
        mod3 = _ada(c, w_ada[l], b_ada[l]).reshape(bsz, 1, 6 * d)
        w_bf = w_in[l].astype(BF16)
        qa, ka, va, b1, b2, b3 = _inproj(x, mod3, ctab, stab, w_bf[:, :gate0], cols)
        oa = _attn_a(sinks[l], qa, ka, va)
        ob = _attn_b(b1, b2, b3)
        x = _post(x, oa, ob, mod3, ln1_g[l], ln1_b[l], ln2_g[l], ln2_b[l],
                  w_bf[:, gate0:], w_branch_a[l].astype(BF16), w_branch_b[l].astype(BF16),
                  w_o[l].astype(BF16), w_gate_up[l].astype(BF16), w_down[l].astype(BF16), alpha)
    return x
```

```python
import functools

import jax
import jax.numpy as jnp
from jax import lax
from jax.experimental import pallas as pl
from jax.experimental.pallas import tpu as pltpu

F32 = jnp.float32
BF16 = jnp.bfloat16

HEAD_DIM = 64
HALF = HEAD_DIM // 2
BLOCK = 128
LANES = 128
A_Q_HEADS = 16
A_KV_HEADS = 2
A_GROUP = A_Q_HEADS // A_KV_HEADS
A_WINDOW = 128
B_PATTERNS = ((128, 1), (512, 4), (2048, 16))
B_HEADS_PER_GROUP = 8
ROPE_THETA = 10000.0
LN_EPS = 1e-5
NEG_INF = -1e30
LOG2E = 1.4426950408889634

TM_IN = 1024
SUB_IN = 512
TM_POST = 1024
SUB_POST = 512
FF_CHUNK = 256
BLOCKS_PER_ITER = 16
A_SLABS_PER_DOT = 2
VMEM_BYTES = 64 * 1024 * 1024
VMEM_LIMIT = VMEM_BYTES - 8 * 1024 * 1024
VMEM_LIMIT_POST = VMEM_BYTES - 2 * 1024 * 1024

_NT = (((1,), (1,)), ((), ()))


def _resident(shape):
    nd = len(shape)
    return pl.BlockSpec(shape, lambda *_: (0,) * nd, pipeline_mode=pl.Buffered(1))


def _ada_kernel(c_ref, w_ref, b_ref, o_ref):
    c = c_ref[...]
    act = (c * jax.nn.sigmoid(c)).astype(BF16)
    o_ref[...] = jnp.dot(act, w_ref[...].astype(BF16), preferred_element_type=F32) + b_ref[...]


def _ada(c, w, b):
    bsz, d = c.shape
    e = w.shape[1]
    tn = d
    return pl.pallas_call(
        _ada_kernel,
        out_shape=jax.ShapeDtypeStruct((bsz, e), F32),
        grid=(e // tn,),
        in_specs=[pl.BlockSpec((bsz, d), lambda j: (0, 0)),
                  pl.BlockSpec((d, tn), lambda j: (0, j)),
                  pl.BlockSpec((1, tn), lambda j: (0, j))],
        out_specs=pl.BlockSpec((bsz, tn), lambda j: (0, j)),
        compiler_params=pltpu.CompilerParams(dimension_semantics=("arbitrary",)),
        name="ada",
    )(c, w, b.reshape(1, e))


def _rope_table_kernel(pos_ref, inv_ref, c_ref, s_ref):
    ang = pos_ref[...] * inv_ref[...]
    cos = jnp.cos(ang)
    sin = jnp.sin(ang)
    reps = LANES // HEAD_DIM
    c_ref[...] = jnp.concatenate([cos, cos] * reps, axis=0).T
    s_ref[...] = jnp.concatenate([-sin, sin] * reps, axis=0).T


def _rope_tables(positions):
    bsz, t = positions.shape
    inv = ROPE_THETA ** (-jnp.arange(HALF, dtype=F32) / HALF)
    tab = pl.BlockSpec((None, t, LANES), lambda b: (b, 0, 0))
    return pl.pallas_call(
        _rope_table_kernel,
        out_shape=[jax.ShapeDtypeStruct((bsz, t, LANES), F32)] * 2,
        grid=(bsz,),
        in_specs=[pl.BlockSpec((None, 1, t), lambda b: (b, 0, 0)),
                  pl.BlockSpec((HALF, 1), lambda b: (0, 0))],
        out_specs=[tab, tab],
        compiler_params=pltpu.CompilerParams(dimension_semantics=("arbitrary",)),
        name="rope_tab",
    )(positions.astype(F32).reshape(bsz, 1, t), inv.reshape(HALF, 1))


def _rope_cols(y, ctab, stab, first_half, scale):
    outs = []
    for j in range(y.shape[1] // LANES):
        slab = y[:, j * LANES:(j + 1) * LANES]
        swapped = jnp.where(first_half,
                            pltpu.roll(slab, LANES - HALF, axis=1),
                            pltpu.roll(slab, HALF, axis=1))
        r = slab * ctab + swapped * stab
        if scale != 1.0:
            r = r * scale
        outs.append(r.astype(BF16))
    return outs[0] if len(outs) == 1 else jnp.concatenate(outs, axis=1)


def _inproj_kernel(x_ref, sc_ref, sh_ref, cn_ref, sn_ref, w_ref,
                   qa_ref, ka_ref, va_ref, b1_ref, b2_ref, b3_ref, us_ref, *, cols):
    tm, d = x_ref.shape
    nslab = d // LANES
    sub = SUB_IN
    lane = lax.broadcasted_iota(jnp.int32, (sub, LANES), 1)
    first_half = (lane & HALF) == 0
    qscale = HEAD_DIM ** -0.5 * LOG2E
    scale1 = 1.0 + sc_ref[...]
    shift = sh_ref[...]
    half_q = cols["qa_w"] // 2
    kvw = cols["kva_w"]
    gw = cols["bg_w"]

    def proj(u, c0, width):
        return jnp.dot(u, w_ref[:, c0:c0 + width], preferred_element_type=F32)

    for base in range(0, tm, sub):
        rows = slice(base, base + sub)
        uf = x_ref[rows, :] * scale1 + shift
        un = uf.astype(BF16)
        for j in range(nslab):
            us_ref[j, rows, :] = uf[:, j * LANES:(j + 1) * LANES]

        def strided_rows(load, r):
            per = sub // r
            return jnp.concatenate([load(pl.ds(base + rho, per, stride=r)) for rho in range(r)], axis=0)

        def u_dilated(r):
            def load(idx):
                return jnp.concatenate([us_ref[j, idx, :] for j in range(nslab)], axis=1)
            return strided_rows(load, r).astype(BF16)

        def tab_dilated(ref, r):
            return strided_rows(lambda idx: ref[idx, :], r)

        cn, sn = cn_ref[rows, :], sn_ref[rows, :]
        for h in range(2):
            y = proj(un, cols["qa"] + h * half_q, half_q)
            qa_ref[rows, h * half_q:(h + 1) * half_q] = _rope_cols(y, cn, sn, first_half, qscale)
        y = proj(un, cols["ka"], 2 * kvw)
        ka_ref[rows, :] = _rope_cols(y[:, :kvw], cn, sn, first_half, 1.0)
        va_ref[rows, :] = y[:, kvw:].astype(BF16)

        def group(u, g, ctab, stab, store):
            y = proj(u, cols["qb"] + g * gw, gw)
            store(0, _rope_cols(y, ctab, stab, first_half, qscale))
            y = proj(u, cols["kb"] + g * gw, gw)
            store(gw, _rope_cols(y, ctab, stab, first_half, 1.0))
            y = proj(u, cols["vb"] + g * gw, gw)
            store(2 * gw, y.astype(BF16))

        def store_nat(c0, val):
            b1_ref[rows, c0:c0 + gw] = val

        def store_res(ref):
            nres = ref.shape[0]
            per = sub // nres
            off = base // nres

            def store(c0, val):
                for rho in range(nres):
                    ref[rho, off:off + per, c0:c0 + gw] = val[rho * per:(rho + 1) * per]
            return store

        group(un, 0, cn, sn, store_nat)
        for g, out_ref in ((1, b2_ref), (2, b3_ref)):
            r = B_PATTERNS[g][1]
            group(u_dilated(r), g, tab_dilated(cn_ref, r), tab_dilated(sn_ref, r), store_res(out_ref))


def _inproj(x, mod3, ctab, stab, w_bf, cols):
    bsz, t, d = x.shape
    tm = TM_IN
    tiles = t // tm
    r4, r16 = B_PATTERNS[1][1], B_PATTERNS[2][1]

    def bq(i):
        return i // tiles, i % tiles

    nat = lambda width: pl.BlockSpec((None, tm, width), lambda i: (bq(i)[0], bq(i)[1], 0))
    res = lambda r, width: pl.BlockSpec((None, r, tm // r, width), lambda i: (bq(i)[0], 0, bq(i)[1], 0))
    in_specs = [
        nat(d),
        pl.BlockSpec((None, 1, d), lambda i: (bq(i)[0], 0, 1)),
        pl.BlockSpec((None, 1, d), lambda i: (bq(i)[0], 0, 0)),
        nat(LANES), nat(LANES),
        _resident(w_bf.shape),
    ]
    gw3 = 3 * cols["bg_w"]
    out_shape = [jax.ShapeDtypeStruct((bsz, t, cols["qa_w"]), BF16),
                 jax.ShapeDtypeStruct((bsz, t, cols["kva_w"]), BF16),
                 jax.ShapeDtypeStruct((bsz, t, cols["kva_w"]), BF16),
                 jax.ShapeDtypeStruct((bsz, t, gw3), BF16),
                 jax.ShapeDtypeStruct((bsz, r4, t // r4, gw3), BF16),
                 jax.ShapeDtypeStruct((bsz, r16, t // r16, gw3), BF16)]
    out_specs = [nat(cols["qa_w"]), nat(cols["kva_w"]), nat(cols["kva_w"]),
                 nat(gw3), res(r4, gw3), res(r16, gw3)]
    qa, ka, va, b1, b2, b3 = pl.pallas_call(
        functools.partial(_inproj_kernel, cols=cols),
        out_shape=out_shape,
        grid=(bsz * tiles,),
        in_specs=in_specs,
        out_specs=out_specs,
        scratch_shapes=[pltpu.VMEM((d // LANES, tm, LANES), F32)],
        compiler_params=pltpu.CompilerParams(dimension_semantics=("arbitrary",),
                                             vmem_limit_bytes=VMEM_LIMIT),
        name="inproj",
    )(x, mod3, mod3, ctab, stab, w_bf)
    return qa, ka, va, b1, b2.reshape(bsz, t, gw3), b3.reshape(bsz, t, gw3)


def _band_bias(keys, n_back, prev_valid):
    qi = lax.broadcasted_iota(jnp.int32, (BLOCK, keys), 0)
    c = lax.broadcasted_iota(jnp.int32, (BLOCK, keys), 1)
    dist = qi + (keys - BLOCK) - c
    ok = (dist >= 0) & (dist <= n_back)
    if not prev_valid:
        ok = ok & (c >= keys - BLOCK)
    return jnp.where(ok, 0.0, NEG_INF).astype(F32)


def _lane_consts():
    lane = lax.broadcasted_iota(jnp.int32, (1, LANES), 1)
    mlo = (lane < HEAD_DIM).astype(BF16)
    mhi = (lane >= HEAD_DIM).astype(BF16)
    lane_lo = lax.broadcasted_iota(jnp.int32, (BLOCK, LANES), 1) < HEAD_DIM
    return mlo, mhi, lane_lo


def _stack_scores(qs, kc, mlo, mhi):
    parts = []
    for q in qs:
        parts += [q * mlo, q * mhi]
    return lax.dot_general(jnp.concatenate(parts, axis=0), kc, _NT, preferred_element_type=F32)


def _stack_softmax(s, sinks2):
    heads = [s[h * BLOCK:(h + 1) * BLOCK] for h in range(s.shape[0] // BLOCK)]
    ms = [jnp.max(h, axis=-1, keepdims=True) for h in heads]
    if sinks2 is not None:
        ms = [jnp.maximum(m, sk) for m, sk in zip(ms, sinks2)]
    p = jnp.concatenate([jnp.exp2(h - m) for h, m in zip(heads, ms)], axis=0).astype(BF16)
    return p, ms


def _stack_values(p, vc, ms, lane_lo, sinks2, want_lse):
    ones = jnp.ones(vc.shape, BF16)
    r = jnp.dot(p, jnp.concatenate([vc, ones], axis=1), preferred_element_type=F32)
    outs = []
    for j in range(len(ms) // 2):
        e, o = r[2 * j * BLOCK:(2 * j + 1) * BLOCK], r[(2 * j + 1) * BLOCK:(2 * j + 2) * BLOCK]
        me, mo = ms[2 * j], ms[2 * j + 1]
        num = jnp.where(lane_lo, e[:, :LANES], o[:, :LANES])
        den = jnp.where(lane_lo, e[:, LANES:], o[:, LANES:])
        if sinks2 is not None:
            den = den + jnp.where(lane_lo, jnp.exp2(sinks2[2 * j] - me), jnp.exp2(sinks2[2 * j + 1] - mo))
        out = num / den
        outs.append((out, jnp.where(lane_lo, me, mo) + jnp.log2(den)) if want_lse else out)
    return outs


def _pair_block(q, kc, vc, bias, mlo, mhi, lane_lo):
    p, ms = _stack_softmax(_stack_scores([q], kc, mlo, mhi) + bias, None)
    return _stack_values(p, vc, ms, lane_lo, None, True)[0]


def _window_block(qs, kc, vc, prev_bias, tri, tri_bf, ntri_bf, mlo, mhi, lane_lo, sinks2):
    s = _stack_scores(qs, kc, mlo, mhi)
    merged = jnp.where(tri, s[:, BLOCK:], s[:, :BLOCK] + prev_bias)
    p, ms = _stack_softmax(merged, sinks2)
    p2 = jnp.concatenate([p * ntri_bf, p * tri_bf], axis=1)
    return _stack_values(p2, vc, ms, lane_lo, sinks2, False)


def _attn_a_kernel(sink_ref, q_ref, k_ref, v_ref, o_ref, kd_ref, vd_ref):
    t = q_ref.shape[0]
    nblk = t // BLOCK
    nslab = A_GROUP // 2
    assert A_WINDOW == BLOCK
    mlo, mhi, lane_lo = _lane_consts()
    lane_lo_t = lax.broadcasted_iota(jnp.int32, (t, LANES), 1) < HEAD_DIM

    for src, dup in ((k_ref, kd_ref), (v_ref, vd_ref)):
        val = src[...].astype(F32)
        swp = pltpu.roll(val, HEAD_DIM, axis=1)
        dup[0] = jnp.where(lane_lo_t, val, swp).astype(BF16)
        dup[1] = jnp.where(lane_lo_t, swp, val).astype(BF16)
    stack = 2 * A_SLABS_PER_DOT * BLOCK
    qi = lax.broadcasted_iota(jnp.int32, (stack, BLOCK), 0) & (BLOCK - 1)
    tri = lax.broadcasted_iota(jnp.int32, (stack, BLOCK), 1) <= qi
    tri_bf = tri.astype(F32).astype(BF16)
    ntri_bf = 1.0 - tri_bf

    def block(i, carry):
        r0 = pl.multiple_of(i * BLOCK, BLOCK)
        p0 = pl.multiple_of(jnp.maximum(i - 1, 0) * BLOCK, BLOCK)
        prev_bias = jnp.where(i > 0, 0.0, NEG_INF).astype(F32)
        for hk in range(A_KV_HEADS):
            kc = jnp.concatenate([kd_ref[hk, pl.ds(p0, BLOCK), :], kd_ref[hk, pl.ds(r0, BLOCK), :]], axis=0)
            vc = jnp.concatenate([vd_ref[hk, pl.ds(p0, BLOCK), :], vd_ref[hk, pl.ds(r0, BLOCK), :]], axis=0)
            for s0 in range(0, nslab, A_SLABS_PER_DOT):
                slabs = [hk * nslab + s0 + u for u in range(A_SLABS_PER_DOT)]
                cols = [slice(sl * LANES, (sl + 1) * LANES) for sl in slabs]
                sinks2 = [sink_ref[2 * sl + h] * LOG2E for sl in slabs for h in range(2)]
                outs = _window_block([q_ref[pl.ds(r0, BLOCK), c] for c in cols], kc, vc, prev_bias,
                                     tri, tri_bf, ntri_bf, mlo, mhi, lane_lo, sinks2)
                for c, out in zip(cols, outs):
                    o_ref[pl.ds(r0, BLOCK), c] = out.astype(BF16)
        return carry

    lax.fori_loop(0, nblk, block, 0, unroll=8)


def _attn_a(sinks, qa, ka, va):
    bsz, t, qw = qa.shape
    kw = ka.shape[2]
    seq = lambda width: pl.BlockSpec((None, t, width), lambda b: (b, 0, 0))
    return pl.pallas_call(
        _attn_a_kernel,
        out_shape=jax.ShapeDtypeStruct((bsz, t, qw), BF16),
        grid=(bsz,),
        in_specs=[pl.BlockSpec(memory_space=pltpu.SMEM), seq(qw), seq(kw), seq(kw)],
        out_specs=seq(qw),
        scratch_shapes=[pltpu.VMEM((A_KV_HEADS, t, LANES), BF16),
                        pltpu.VMEM((A_KV_HEADS, t, LANES), BF16)],
        compiler_params=pltpu.CompilerParams(dimension_semantics=("arbitrary",),
                                             vmem_limit_bytes=VMEM_LIMIT),
        name="attn_a",
    )(sinks, qa, ka, va)


def _attn_b_kernel(q1_ref, k1_ref, v1_ref, q2_ref, k2_ref, v2_ref, q3_ref, k3_ref, v3_ref,
                   o_ref, o1_s, l1_s, o3_s, l3_s, fin_s, bias_ref, bias1_ref):
    t = q1_ref.shape[0]
    mlo, mhi, lane_lo = _lane_consts()
    pair = functools.partial(_pair_block, mlo=mlo, mhi=mhi, lane_lo=lane_lo)
    (w1, _), (w4, r4), (w16, r16) = B_PATTERNS
    nb4 = t // r4 // BLOCK
    assert w1 == w4 // r4 == w16 // r16 and t // r16 == BLOCK and r16 % r4 == 0

    twice = lambda b: jnp.concatenate([b, b], axis=0)
    bias_ref[...] = twice(_band_bias(2 * BLOCK, w1, True))
    bias1_ref[...] = twice(_band_bias(BLOCK, w1, True))

    def first_block(q_ref, k_ref, v_ref, r0):
        k = k_ref[pl.ds(r0, BLOCK), :]
        s = _stack_scores([q_ref[pl.ds(r0, BLOCK), :]], jnp.concatenate([k, k], axis=0), mlo, mhi)
        p, ms = _stack_softmax(s[:, :BLOCK] + bias1_ref[...], None)
        return _stack_values(p, v_ref[pl.ds(r0, BLOCK), :], ms, lane_lo, None, True)[0]

    def banded(q_ref, k_ref, v_ref, r0, bias):
        return pair(q_ref[pl.ds(r0, BLOCK), :], k_ref[pl.ds(r0 - BLOCK, 2 * BLOCK), :],
                    v_ref[pl.ds(r0 - BLOCK, 2 * BLOCK), :], bias)

    rows4 = t // r4

    for rho in range(r16):
        out, lse = first_block(q3_ref, k3_ref, v3_ref, rho * BLOCK)
        dst = pl.ds((rho % r4) * rows4 + rho // r4, BLOCK, stride=r4)
        o3_s[dst, :] = out
        l3_s[dst, :] = lse

    for i in range(t // BLOCK):
        r0 = i * BLOCK
        if i == 0:
            out, lse = first_block(q1_ref, k1_ref, v1_ref, r0)
        else:
            out, lse = banded(q1_ref, k1_ref, v1_ref, r0, bias_ref[...])
        o1_s[pl.ds(r0, BLOCK), :] = out
        l1_s[pl.ds(r0, BLOCK), :] = lse

    for idx in range(r4 * nb4):
        rho, sb = idx // nb4, idx % nb4
        r0 = idx * BLOCK
        if sb == 0:
            o2, l2 = first_block(q2_ref, k2_ref, v2_ref, r0)
        else:
            o2, l2 = banded(q2_ref, k2_ref, v2_ref, r0, bias_ref[...])
        nat = pl.ds(sb * (BLOCK * r4) + rho, BLOCK, stride=r4)
        o1, l1 = o1_s[nat, :], l1_s[nat, :]
        o3, l3 = o3_s[pl.ds(r0, BLOCK), :], l3_s[pl.ds(r0, BLOCK), :]
        mx = jnp.maximum(jnp.maximum(l1, l2), l3)
        e1, e2, e3 = jnp.exp2(l1 - mx), jnp.exp2(l2 - mx), jnp.exp2(l3 - mx)
        fin_s[nat, :] = (e1 * o1 + e2 * o2 + e3 * o3) / (e1 + e2 + e3)

    o_ref[...] = fin_s[...].astype(BF16)


def _attn_b(b1, b2, b3):
    bsz, t, w3 = b1.shape
    gw = w3 // 3
    npair = gw // LANES

    def spec(part):
        return pl.BlockSpec((None, t, LANES), lambda b, j: (b, 0, part * npair + j))

    return pl.pallas_call(
        _attn_b_kernel,
        out_shape=jax.ShapeDtypeStruct((bsz, t, gw), BF16),
        grid=(bsz, npair),
        in_specs=[spec(0), spec(1), spec(2)] * 3,
        out_specs=pl.BlockSpec((None, t, LANES), lambda b, j: (b, 0, j)),
        scratch_shapes=[pltpu.VMEM((t, LANES), F32)] * 5
                       + [pltpu.VMEM((2 * BLOCK, 2 * BLOCK), F32),
                          pltpu.VMEM((2 * BLOCK, BLOCK), F32)],
        compiler_params=pltpu.CompilerParams(dimension_semantics=("arbitrary", "arbitrary"),
                                             vmem_limit_bytes=VMEM_LIMIT),
        name="attn_b",
    )(b1, b1, b1, b2, b2, b2, b3, b3, b3)


def _layer_norm(h, g, b):
    mu = jnp.mean(h, axis=-1, keepdims=True)
    xc = h - mu
    var = jnp.mean(xc * xc, axis=-1, keepdims=True)
    return xc * lax.rsqrt(var + LN_EPS) * g + b


def _post_kernel(x_ref, oa_ref, ob_ref, scm_ref, shm_ref, gm_ref, shf_ref, scf_ref, gf_ref,
                 g1_ref, b1_ref, g2_ref, b2_ref,
                 wg_ref, wa_ref, wb_ref, wo_ref, wgu_ref, wd_ref, o_ref, *, alpha, d_ff):
    d = x_ref.shape[1]

    def mix(rows):
        x = x_ref[rows, :]
        u = (x * (1.0 + scm_ref[...]) + shm_ref[...]).astype(BF16)
        ga = jax.nn.sigmoid(jnp.dot(u, wg_ref[:, :d], preferred_element_type=F32))
        ya = jnp.dot(oa_ref[rows, :], wa_ref[...], preferred_element_type=F32)
        merged = ga * ya
        gb = jax.nn.sigmoid(jnp.dot(u, wg_ref[:, d:], preferred_element_type=F32))
        yb = jnp.dot(ob_ref[rows, :], wb_ref[...], preferred_element_type=F32)
        merged = (merged + gb * yb).astype(BF16)
        return jnp.dot(merged, wo_ref[...], preferred_element_type=F32)

    def norm1(rows, y):
        return _layer_norm(alpha * x_ref[rows, :] + (1.0 + gm_ref[...]) * y, g1_ref[...], b1_ref[...])

    def ffn(x1):
        u2 = (x1 * (1.0 + scf_ref[...]) + shf_ref[...]).astype(BF16)
        acts = []
        for c in range(d_ff // FF_CHUNK):
            c0 = c * FF_CHUNK
            hg = jnp.dot(u2, wgu_ref[:, c0:c0 + FF_CHUNK], preferred_element_type=F32)
            hu = jnp.dot(u2, wgu_ref[:, d_ff + c0:d_ff + c0 + FF_CHUNK], preferred_element_type=F32)
            acts.append((hg * jax.nn.sigmoid(hg) * hu).astype(BF16))
        return jnp.dot(jnp.concatenate(acts, axis=1), wd_ref[...], preferred_element_type=F32)

    def norm2(rows, x1, acc):
        o_ref[rows, :] = _layer_norm(alpha * x1 + (1.0 + gf_ref[...]) * acc, g2_ref[...], b2_ref[...])

    subs = [slice(r0, r0 + SUB_POST) for r0 in range(0, x_ref.shape[0], SUB_POST)]
    ys = [mix(rows) for rows in subs]
    x1s, accs = [], []
    for i, rows in enumerate(subs):
        x1s.append(norm1(rows, ys[i]))
        accs.append(ffn(x1s[i]))
        if i > 0:
            norm2(subs[i - 1], x1s[i - 1], accs[i - 1])
    norm2(subs[-1], x1s[-1], accs[-1])


def _post(x, oa, ob, mod3, ln1_g, ln1_b, ln2_g, ln2_b, wg, wa, wb, wo, wgu, wd, alpha):
    bsz, t, d = x.shape
    tm = TM_POST
    tiles = t // tm
    d_ff = wd.shape[0]

    def bq(i):
        return i // tiles, i % tiles

    nat = lambda width: pl.BlockSpec((None, tm, width), lambda i: (bq(i)[0], bq(i)[1], 0))
    modspec = lambda k: pl.BlockSpec((None, 1, d), lambda i: (bq(i)[0], 0, k))
    vec = pl.BlockSpec((1, d), lambda i: (0, 0))
    in_specs = [nat(d), nat(oa.shape[2]), nat(ob.shape[2]),
                modspec(1), modspec(0), modspec(2), modspec(3), modspec(4), modspec(5),
                vec, vec, vec, vec,
                _resident(wg.shape), _resident(wa.shape), _resident(wb.shape),
                _resident(wo.shape), _resident(wgu.shape), _resident(wd.shape)]
    row = lambda v: v.reshape(1, d)
    return pl.pallas_call(
        functools.partial(_post_kernel, alpha=alpha, d_ff=d_ff),
        out_shape=jax.ShapeDtypeStruct((bsz, t, d), F32),
        grid=(bsz * tiles,),
        in_specs=in_specs,
        out_specs=nat(d),
        compiler_params=pltpu.CompilerParams(dimension_semantics=("arbitrary",),
                                             vmem_limit_bytes=VMEM_LIMIT_POST),
        name="post",
    )(x, oa, ob, mod3, mod3, mod3, mod3, mod3, mod3,
      row(ln1_g), row(ln1_b), row(ln2_g), row(ln2_b), wg, wa, wb, wo, wgu, wd)


def kernel(x, c, positions, w_ada, b_ada, w_in, sinks, w_branch_a, w_branch_b, w_o,
           ln1_g, ln1_b, w_gate_up, w_down, ln2_g, ln2_b):
    depth = w_ada.shape[0]
    bsz, t, d = x.shape
    alpha = (2 * depth) ** 0.25
    a_q_w = A_Q_HEADS * HEAD_DIM
    a_kv_w = A_KV_HEADS * HEAD_DIM
    b_w = B_HEADS_PER_GROUP * len(B_PATTERNS) * HEAD_DIM
    cols = {"qa": 0, "qa_w": a_q_w, "ka": a_q_w, "kva_w": a_kv_w,
            "qb": a_q_w + 2 * a_kv_w, "kb": a_q_w + 2 * a_kv_w + b_w,
            "vb": a_q_w + 2 * a_kv_w + 2 * b_w, "bg_w": B_HEADS_PER_GROUP * HEAD_DIM}
    gate0 = a_q_w + 2 * a_kv_w + 3 * b_w

    ctab, stab = _rope_tables(positions)
    for l in range(depth):
        mod3 = _ada(c, w_ada[l], b_ada[l]).reshape(bsz, 1, 6 * d)
        w_qkv = w_in[l][:, :gate0].astype(BF16)
        w_gates = w_in[l][:, gate0:].astype(BF16)
        qa, ka, va, b1, b2, b3 = _inproj(x, mod3, ctab, stab, w_qkv, cols)
        oa = _attn_a(sinks[l], qa, ka, va)
        ob = _attn_b(b1, b2, b3)
        x = _post(x, oa, ob, mod3, ln1_g[l], ln1_b[l], ln2_g[l], ln2_b[l],
                  w_gates, w_branch_a[l].astype(BF16), w_branch_b[l].astype(BF16),
                  w_o[l].astype(BF16), w_gate_up[l].astype(BF16), w_down[l].astype(BF16), alpha)
    return x
```

```python
import functools

import jax
import jax.numpy as jnp
from jax import lax
from jax.experimental import pallas as pl
from jax.experimental.pallas import tpu as pltpu

F32 = jnp.float32
BF16 = jnp.bfloat16

HEAD_DIM = 64
HALF = HEAD_DIM // 2
BLOCK = 128
LANES = 128
A_Q_HEADS = 16
A_KV_HEADS = 2
A_GROUP = A_Q_HEADS // A_KV_HEADS
A_WINDOW = 128
B_PATTERNS = ((128, 1), (512, 4), (2048, 16))
B_HEADS_PER_GROUP = 8
ROPE_THETA = 10000.0
LN_EPS = 1e-5
NEG_INF = -1e30
LOG2E = 1.4426950408889634

TM_IN = 1024
SUB_IN = 512
TM_POST = 1024
SUB_POST = 512
FF_CHUNK = 256
B_PAIRS_PER_STEP = 2
A_SLABS_PER_DOT = 2
VMEM_BYTES = 64 * 1024 * 1024
VMEM_LIMIT = VMEM_BYTES - 8 * 1024 * 1024
VMEM_LIMIT_POST = VMEM_BYTES - 2 * 1024 * 1024

_NT = (((1,), (1,)), ((), ()))


def _resident(shape):
    nd = len(shape)
    return pl.BlockSpec(shape, lambda *_: (0,) * nd, pipeline_mode=pl.Buffered(1))


def _ada_kernel(c_ref, w_ref, b_ref, o_ref):
    c = c_ref[...]
    act = (c * jax.nn.sigmoid(c)).astype(BF16)
    o_ref[...] = jnp.dot(act, w_ref[...].astype(BF16), preferred_element_type=F32) + b_ref[...]


def _ada(c, w, b):
    bsz, d = c.shape
    e = w.shape[1]
    tn = d
    return pl.pallas_call(
        _ada_kernel,
        out_shape=jax.ShapeDtypeStruct((bsz, e), F32),
        grid=(e // tn,),
        in_specs=[pl.BlockSpec((bsz, d), lambda j: (0, 0)),
                  pl.BlockSpec((d, tn), lambda j: (0, j)),
                  pl.BlockSpec((1, tn), lambda j: (0, j))],
        out_specs=pl.BlockSpec((bsz, tn), lambda j: (0, j)),
        compiler_params=pltpu.CompilerParams(dimension_semantics=("arbitrary",)),
        name="ada",
    )(c, w, b.reshape(1, e))


def _rope_table_kernel(pos_ref, inv_ref, c_ref, s_ref):
    ang = pos_ref[...] * inv_ref[...]
    cos = jnp.cos(ang)
    sin = jnp.sin(ang)
    reps = LANES // HEAD_DIM
    c_ref[...] = jnp.concatenate([cos, cos] * reps, axis=0).T
    s_ref[...] = jnp.concatenate([-sin, sin] * reps, axis=0).T


def _rope_tables(positions):
    bsz, t = positions.shape
    inv = ROPE_THETA ** (-jnp.arange(HALF, dtype=F32) / HALF)
    tab = pl.BlockSpec((None, t, LANES), lambda b: (b, 0, 0))
    return pl.pallas_call(
        _rope_table_kernel,
        out_shape=[jax.ShapeDtypeStruct((bsz, t, LANES), F32)] * 2,
        grid=(bsz,),
        in_specs=[pl.BlockSpec((None, 1, t), lambda b: (b, 0, 0)),
                  pl.BlockSpec((HALF, 1), lambda b: (0, 0))],
        out_specs=[tab, tab],
        compiler_params=pltpu.CompilerParams(dimension_semantics=("arbitrary",)),
        name="rope_tab",
    )(positions.astype(F32).reshape(bsz, 1, t), inv.reshape(HALF, 1))


def _rope_cols(y, ctab, stab, first_half, scale):
    outs = []
    for j in range(y.shape[1] // LANES):
        slab = y[:, j * LANES:(j + 1) * LANES]
        swapped = jnp.where(first_half,
                            pltpu.roll(slab, LANES - HALF, axis=1),
                            pltpu.roll(slab, HALF, axis=1))
        r = slab * ctab + swapped * stab
        if scale != 1.0:
            r = r * scale
        outs.append(r.astype(BF16))
    return outs[0] if len(outs) == 1 else jnp.concatenate(outs, axis=1)


def _inproj_kernel(x_ref, sc_ref, sh_ref, cn_ref, sn_ref, w_ref,
                   qa_ref, ka_ref, va_ref, b1_ref, b2_ref, b3_ref, us_ref, *, cols):
    tm, d = x_ref.shape
    nslab = d // LANES
    sub = SUB_IN
    lane = lax.broadcasted_iota(jnp.int32, (sub, LANES), 1)
    first_half = (lane & HALF) == 0
    qscale = HEAD_DIM ** -0.5 * LOG2E
    scale1 = 1.0 + sc_ref[...]
    shift = sh_ref[...]
    half_q = cols["qa_w"] // 2
    kvw = cols["kva_w"]
    gw = cols["bg_w"]

    def proj(u, c0, width):
        return jnp.dot(u, w_ref[:, c0:c0 + width], preferred_element_type=F32)

    for base in range(0, tm, sub):
        rows = slice(base, base + sub)
        uf = x_ref[rows, :] * scale1 + shift
        un = uf.astype(BF16)
        for j in range(nslab):
            us_ref[j, rows, :] = uf[:, j * LANES:(j + 1) * LANES]

        def strided_rows(load, r):
            per = sub // r
            return jnp.concatenate([load(pl.ds(base + rho, per, stride=r)) for rho in range(r)], axis=0)

        def u_dilated(r):
            def load(idx):
                return jnp.concatenate([us_ref[j, idx, :] for j in range(nslab)], axis=1)
            return strided_rows(load, r).astype(BF16)

        def tab_dilated(ref, r):
            return strided_rows(lambda idx: ref[idx, :], r)

        cn, sn = cn_ref[rows, :], sn_ref[rows, :]
        for h in range(2):
            y = proj(un, cols["qa"] + h * half_q, half_q)
            qa_ref[rows, h * half_q:(h + 1) * half_q] = _rope_cols(y, cn, sn, first_half, qscale)
        y = proj(un, cols["ka"], 2 * kvw)
        ka_ref[rows, :] = _rope_cols(y[:, :kvw], cn, sn, first_half, 1.0)
        va_ref[rows, :] = y[:, kvw:].astype(BF16)

        def group(u, g, ctab, stab, store):
            y = proj(u, cols["qb"] + g * gw, gw)
            store(0, _rope_cols(y, ctab, stab, first_half, qscale))
            y = proj(u, cols["kb"] + g * gw, gw)
            store(gw, _rope_cols(y, ctab, stab, first_half, 1.0))
            y = proj(u, cols["vb"] + g * gw, gw)
            store(2 * gw, y.astype(BF16))

        def store_nat(c0, val):
            b1_ref[rows, c0:c0 + gw] = val

        def store_res(ref):
            nres = ref.shape[0]
            per = sub // nres
            off = base // nres

            def store(c0, val):
                for rho in range(nres):
                    ref[rho, off:off + per, c0:c0 + gw] = val[rho * per:(rho + 1) * per]
            return store

        group(un, 0, cn, sn, store_nat)
        for g, out_ref in ((1, b2_ref), (2, b3_ref)):
            r = B_PATTERNS[g][1]
            group(u_dilated(r), g, tab_dilated(cn_ref, r), tab_dilated(sn_ref, r), store_res(out_ref))


def _inproj(x, mod3, ctab, stab, w_bf, cols):
    bsz, t, d = x.shape
    tm = TM_IN
    tiles = t // tm
    r4, r16 = B_PATTERNS[1][1], B_PATTERNS[2][1]

    def bq(i):
        return i // tiles, i % tiles

    nat = lambda width: pl.BlockSpec((None, tm, width), lambda i: (bq(i)[0], bq(i)[1], 0))
    res = lambda r, width: pl.BlockSpec((None, r, tm // r, width), lambda i: (bq(i)[0], 0, bq(i)[1], 0))
    in_specs = [
        nat(d),
        pl.BlockSpec((None, 1, d), lambda i: (bq(i)[0], 0, 1)),
        pl.BlockSpec((None, 1, d), lambda i: (bq(i)[0], 0, 0)),
        nat(LANES), nat(LANES),
        _resident(w_bf.shape),
    ]
    gw3 = 3 * cols["bg_w"]
    out_shape = [jax.ShapeDtypeStruct((bsz, t, cols["qa_w"]), BF16),
                 jax.ShapeDtypeStruct((bsz, t, cols["kva_w"]), BF16),
                 jax.ShapeDtypeStruct((bsz, t, cols["kva_w"]), BF16),
                 jax.ShapeDtypeStruct((bsz, t, gw3), BF16),
                 jax.ShapeDtypeStruct((bsz, r4, t // r4, gw3), BF16),
                 jax.ShapeDtypeStruct((bsz, r16, t // r16, gw3), BF16)]
    out_specs = [nat(cols["qa_w"]), nat(cols["kva_w"]), nat(cols["kva_w"]),
                 nat(gw3), res(r4, gw3), res(r16, gw3)]
    qa, ka, va, b1, b2, b3 = pl.pallas_call(
        functools.partial(_inproj_kernel, cols=cols),
        out_shape=out_shape,
        grid=(bsz * tiles,),
        in_specs=in_specs,
        out_specs=out_specs,
        scratch_shapes=[pltpu.VMEM((d // LANES, tm, LANES), F32)],
        compiler_params=pltpu.CompilerParams(dimension_semantics=("arbitrary",),
                                             vmem_limit_bytes=VMEM_LIMIT),
        name="inproj",
    )(x, mod3, mod3, ctab, stab, w_bf)
    return qa, ka, va, b1, b2.reshape(bsz, t, gw3), b3.reshape(bsz, t, gw3)


def _band_bias(keys, n_back, prev_valid):
    qi = lax.broadcasted_iota(jnp.int32, (BLOCK, keys), 0)
    c = lax.broadcasted_iota(jnp.int32, (BLOCK, keys), 1)
    dist = qi + (keys - BLOCK) - c
    ok = (dist >= 0) & (dist <= n_back)
    if not prev_valid:
        ok = ok & (c >= keys - BLOCK)
    return jnp.where(ok, 0.0, NEG_INF).astype(F32)


def _lane_consts():
    lane = lax.broadcasted_iota(jnp.int32, (1, LANES), 1)
    mlo = (lane < HEAD_DIM).astype(BF16)
    mhi = (lane >= HEAD_DIM).astype(BF16)
    lane_lo = lax.broadcasted_iota(jnp.int32, (BLOCK, LANES), 1) < HEAD_DIM
    return mlo, mhi, lane_lo


def _stack_scores(qs, kc, mlo, mhi):
    parts = []
    for q in qs:
        parts += [q * mlo, q * mhi]
    return lax.dot_general(jnp.concatenate(parts, axis=0), kc, _NT, preferred_element_type=F32)


def _stack_softmax(s, sinks2):
    heads = [s[h * BLOCK:(h + 1) * BLOCK] for h in range(s.shape[0] // BLOCK)]
    ms = [jnp.max(h, axis=-1, keepdims=True) for h in heads]
    if sinks2 is not None:
        ms = [jnp.maximum(m, sk) for m, sk in zip(ms, sinks2)]
    p = jnp.concatenate([jnp.exp2(h - m) for h, m in zip(heads, ms)], axis=0).astype(BF16)
    return p, ms


def _stack_values(p, vc, ms, lane_lo, sinks2, normalise):
    ones = jnp.ones(vc.shape, BF16)
    r = jnp.dot(p, jnp.concatenate([vc, ones], axis=1), preferred_element_type=F32)
    outs = []
    for j in range(len(ms) // 2):
        e, o = r[2 * j * BLOCK:(2 * j + 1) * BLOCK], r[(2 * j + 1) * BLOCK:(2 * j + 2) * BLOCK]
        me, mo = ms[2 * j], ms[2 * j + 1]
        num = jnp.where(lane_lo, e[:, :LANES], o[:, :LANES])
        den = jnp.where(lane_lo, e[:, LANES:], o[:, LANES:])
        if sinks2 is not None:
            den = den + jnp.where(lane_lo, jnp.exp2(sinks2[2 * j] - me), jnp.exp2(sinks2[2 * j + 1] - mo))
        outs.append(num / den if normalise else (num, den, jnp.where(lane_lo, me, mo)))
    return outs


def _pair_block(q, kc, vc, bias, mlo, mhi, lane_lo):
    p, ms = _stack_softmax(_stack_scores([q], kc, mlo, mhi) + bias, None)
    return _stack_values(p, vc, ms, lane_lo, None, False)[0]


def _window_block(qs, kc, vc, prev_bias, tri, tri_bf, ntri_bf, mlo, mhi, lane_lo, sinks2):
    s = _stack_scores(qs, kc, mlo, mhi)
    merged = jnp.where(tri, s[:, BLOCK:], s[:, :BLOCK] + prev_bias)
    p, ms = _stack_softmax(merged, sinks2)
    p2 = jnp.concatenate([p * ntri_bf, p * tri_bf], axis=1)
    return _stack_values(p2, vc, ms, lane_lo, sinks2, True)


def _attn_a_kernel(sink_ref, q_ref, k_ref, v_ref, o_ref, kd_ref, vd_ref):
    t = q_ref.shape[0]
    nblk = t // BLOCK
    nslab = A_GROUP // 2
    assert A_WINDOW == BLOCK
    mlo, mhi, lane_lo = _lane_consts()
    lane_lo_t = lax.broadcasted_iota(jnp.int32, (t, LANES), 1) < HEAD_DIM

    for src, dup in ((k_ref, kd_ref), (v_ref, vd_ref)):
        val = src[...].astype(F32)
        swp = pltpu.roll(val, HEAD_DIM, axis=1)
        dup[0] = jnp.where(lane_lo_t, val, swp).astype(BF16)
        dup[1] = jnp.where(lane_lo_t, swp, val).astype(BF16)
    stack = 2 * A_SLABS_PER_DOT * BLOCK
    qi = lax.broadcasted_iota(jnp.int32, (stack, BLOCK), 0) & (BLOCK - 1)
    tri = lax.broadcasted_iota(jnp.int32, (stack, BLOCK), 1) <= qi
    tri_bf = tri.astype(F32).astype(BF16)
    ntri_bf = 1.0 - tri_bf

    for i in range(nblk):
        r0 = i * BLOCK
        p0 = max(i - 1, 0) * BLOCK
        prev_bias = 0.0 if i > 0 else NEG_INF
        for hk in range(A_KV_HEADS):
            kc = jnp.concatenate([kd_ref[hk, pl.ds(p0, BLOCK), :], kd_ref[hk, pl.ds(r0, BLOCK), :]], axis=0)
            vc = jnp.concatenate([vd_ref[hk, pl.ds(p0, BLOCK), :], vd_ref[hk, pl.ds(r0, BLOCK), :]], axis=0)
            for s0 in range(0, nslab, A_SLABS_PER_DOT):
                slabs = [hk * nslab + s0 + u for u in range(A_SLABS_PER_DOT)]
                cols = [slice(sl * LANES, (sl + 1) * LANES) for sl in slabs]
                sinks2 = [sink_ref[2 * sl + h] * LOG2E for sl in slabs for h in range(2)]
                outs = _window_block([q_ref[pl.ds(r0, BLOCK), c] for c in cols], kc, vc, prev_bias,
                                     tri, tri_bf, ntri_bf, mlo, mhi, lane_lo, sinks2)
                for c, out in zip(cols, outs):
                    o_ref[pl.ds(r0, BLOCK), c] = out.astype(BF16)


def _attn_a(sinks, qa, ka, va):
    bsz, t, qw = qa.shape
    kw = ka.shape[2]
    seq = lambda width: pl.BlockSpec((None, t, width), lambda b: (b, 0, 0))
    return pl.pallas_call(
        _attn_a_kernel,
        out_shape=jax.ShapeDtypeStruct((bsz, t, qw), BF16),
        grid=(bsz,),
        in_specs=[pl.BlockSpec(memory_space=pltpu.SMEM), seq(qw), seq(kw), seq(kw)],
        out_specs=seq(qw),
        scratch_shapes=[pltpu.VMEM((A_KV_HEADS, t, LANES), BF16),
                        pltpu.VMEM((A_KV_HEADS, t, LANES), BF16)],
        compiler_params=pltpu.CompilerParams(dimension_semantics=("arbitrary",),
                                             vmem_limit_bytes=VMEM_LIMIT),
        name="attn_a",
    )(sinks, qa, ka, va)


def _attn_b_kernel(q1_ref, k1_ref, v1_ref, q2_ref, k2_ref, v2_ref, q3_ref, k3_ref, v3_ref,
                   o_ref, n1_s, d1_s, m1_s, n3_s, d3_s, m3_s, fin_s, bias_ref, bias1_ref):
    t = q1_ref.shape[0]
    mlo, mhi, lane_lo = _lane_consts()
    pair = functools.partial(_pair_block, mlo=mlo, mhi=mhi, lane_lo=lane_lo)
    (w1, _), (w4, r4), (w16, r16) = B_PATTERNS
    nb4 = t // r4 // BLOCK
    rows4 = t // r4
    assert w1 == w4 // r4 == w16 // r16 and t // r16 == BLOCK and r16 % r4 == 0

    twice = lambda b: jnp.concatenate([b, b], axis=0)
    bias_ref[...] = twice(_band_bias(2 * BLOCK, w1, True))
    bias1_ref[...] = twice(_band_bias(BLOCK, w1, True))

    def head_pair(pp):
        cs = slice(pp * LANES, (pp + 1) * LANES)
        g1_s, g3_s = (n1_s, d1_s, m1_s), (n3_s, d3_s, m3_s)

        def first_block(q_ref, k_ref, v_ref, r0):
            k = k_ref[pl.ds(r0, BLOCK), cs]
            s = _stack_scores([q_ref[pl.ds(r0, BLOCK), cs]], jnp.concatenate([k, k], axis=0), mlo, mhi)
            p, ms = _stack_softmax(s[:, :BLOCK] + bias1_ref[...], None)
            return _stack_values(p, v_ref[pl.ds(r0, BLOCK), cs], ms, lane_lo, None, False)[0]

        def banded(q_ref, k_ref, v_ref, r0):
            return pair(q_ref[pl.ds(r0, BLOCK), cs], k_ref[pl.ds(r0 - BLOCK, 2 * BLOCK), cs],
                        v_ref[pl.ds(r0 - BLOCK, 2 * BLOCK), cs], bias_ref[...])


        for rho in range(r16):
            stats = first_block(q3_ref, k3_ref, v3_ref, rho * BLOCK)
            dst = pl.ds((rho % r4) * rows4 + rho // r4, BLOCK, stride=r4)
            for ref, val in zip(g3_s, stats):
                ref[pp, dst, :] = val

        for i in range(t // BLOCK):
            r0 = i * BLOCK
            stats = first_block(q1_ref, k1_ref, v1_ref, r0) if i == 0 else banded(q1_ref, k1_ref, v1_ref, r0)
            for ref, val in zip(g1_s, stats):
                ref[pp, pl.ds(r0, BLOCK), :] = val

        for idx in range(r4 * nb4):
            rho, sb = idx // nb4, idx % nb4
            r0 = idx * BLOCK
            n2, d2, m2 = first_block(q2_ref, k2_ref, v2_ref, r0) if sb == 0 else banded(q2_ref, k2_ref, v2_ref, r0)
            nat = pl.ds(sb * (BLOCK * r4) + rho, BLOCK, stride=r4)
            n1, d1, m1 = (ref[pp, nat, :] for ref in g1_s)
            n3, d3, m3 = (ref[pp, pl.ds(r0, BLOCK), :] for ref in g3_s)
            mx = jnp.maximum(jnp.maximum(m1, m2), m3)
            a1, a2, a3 = jnp.exp2(m1 - mx), jnp.exp2(m2 - mx), jnp.exp2(m3 - mx)
            fin_s[pp, nat, :] = (a1 * n1 + a2 * n2 + a3 * n3) / (a1 * d1 + a2 * d2 + a3 * d3)

        o_ref[:, cs] = fin_s[pp].astype(BF16)

    for pp in range(o_ref.shape[1] // LANES):
        head_pair(pp)


def _attn_b(b1, b2, b3):
    bsz, t, w3 = b1.shape
    gw = w3 // 3
    wblk = B_PAIRS_PER_STEP * LANES
    nblk = gw // wblk

    def spec(part):
        return pl.BlockSpec((None, t, wblk), lambda b, j: (b, 0, part * nblk + j))

    return pl.pallas_call(
        _attn_b_kernel,
        out_shape=jax.ShapeDtypeStruct((bsz, t, gw), BF16),
        grid=(bsz, nblk),
        in_specs=[spec(0), spec(1), spec(2)] * 3,
        out_specs=pl.BlockSpec((None, t, wblk), lambda b, j: (b, 0, j)),
        scratch_shapes=[pltpu.VMEM((B_PAIRS_PER_STEP, t, LANES), F32)] * 7
                       + [pltpu.VMEM((2 * BLOCK, 2 * BLOCK), F32),
                          pltpu.VMEM((2 * BLOCK, BLOCK), F32)],
        compiler_params=pltpu.CompilerParams(dimension_semantics=("arbitrary", "arbitrary"),
                                             vmem_limit_bytes=VMEM_LIMIT),
        name="attn_b",
    )(b1, b1, b1, b2, b2, b2, b3, b3, b3)


def _layer_norm(h, g, b):
    mu = jnp.mean(h, axis=-1, keepdims=True)
    xc = h - mu
    var = jnp.mean(xc * xc, axis=-1, keepdims=True)
    return xc * lax.rsqrt(var + LN_EPS) * g + b


def _post_kernel(x_ref, oa_ref, ob_ref, scm_ref, shm_ref, gm_ref, shf_ref, scf_ref, gf_ref,
                 g1_ref, b1_ref, g2_ref, b2_ref,
                 wg_ref, wa_ref, wb_ref, wo_ref, wgu_ref, wd_ref, o_ref, *, alpha, d_ff):
    d = x_ref.shape[1]

    def mix(rows):
        x = x_ref[rows, :]
        u = (x * (1.0 + scm_ref[...]) + shm_ref[...]).astype(BF16)
        oa, ob = oa_ref[rows, :], ob_ref[rows, :]
        parts = []
        for c0 in range(0, d, FF_CHUNK):
            ga = jax.nn.sigmoid(jnp.dot(u, wg_ref[:, c0:c0 + FF_CHUNK], preferred_element_type=F32))
            ya = jnp.dot(oa, wa_ref[:, c0:c0 + FF_CHUNK], preferred_element_type=F32)
            gb = jax.nn.sigmoid(jnp.dot(u, wg_ref[:, d + c0:d + c0 + FF_CHUNK], preferred_element_type=F32))
            yb = jnp.dot(ob, wb_ref[:, c0:c0 + FF_CHUNK], preferred_element_type=F32)
            parts.append((ga * ya + gb * yb).astype(BF16))
        return jnp.dot(jnp.concatenate(parts, axis=1), wo_ref[...], preferred_element_type=F32)

    def norm1(rows, y):
        return _layer_norm(alpha * x_ref[rows, :] + (1.0 + gm_ref[...]) * y, g1_ref[...], b1_ref[...])

    def ffn(x1):
        u2 = (x1 * (1.0 + scf_ref[...]) + shf_ref[...]).astype(BF16)
        acts = []
        for c in range(d_ff // FF_CHUNK):
            c0 = c * FF_CHUNK
            hg = jnp.dot(u2, wgu_ref[:, c0:c0 + FF_CHUNK], preferred_element_type=F32)
            hu = jnp.dot(u2, wgu_ref[:, d_ff + c0:d_ff + c0 + FF_CHUNK], preferred_element_type=F32)
            acts.append((hg * jax.nn.sigmoid(hg) * hu).astype(BF16))
        return jnp.dot(jnp.concatenate(acts, axis=1), wd_ref[...], preferred_element_type=F32)

    def norm2(rows, x1, acc):
        o_ref[rows, :] = _layer_norm(alpha * x1 + (1.0 + gf_ref[...]) * acc, g2_ref[...], b2_ref[...])

    subs = [slice(r0, r0 + SUB_POST) for r0 in range(0, x_ref.shape[0], SUB_POST)]
    ys = [mix(rows) for rows in subs]
    x1s, accs = [], []
    for i, rows in enumerate(subs):
        x1s.append(norm1(rows, ys[i]))
        accs.append(ffn(x1s[i]))
        if i > 0:
            norm2(subs[i - 1], x1s[i - 1], accs[i - 1])
    norm2(subs[-1], x1s[-1], accs[-1])


def _post(x, oa, ob, mod3, ln1_g, ln1_b, ln2_g, ln2_b, wg, wa, wb, wo, wgu, wd, alpha):
    bsz, t, d = x.shape
    tm = TM_POST
    tiles = t // tm
    d_ff = wd.shape[0]

    def bq(i):
        return i // tiles, i % tiles

    nat = lambda width: pl.BlockSpec((None, tm, width), lambda i: (bq(i)[0], bq(i)[1], 0))
    modspec = lambda k: pl.BlockSpec((None, 1, d), lambda i: (bq(i)[0], 0, k))
    vec = pl.BlockSpec((1, d), lambda i: (0, 0))
    in_specs = [nat(d), nat(oa.shape[2]), nat(ob.shape[2]),
                modspec(1), modspec(0), modspec(2), modspec(3), modspec(4), modspec(5),
                vec, vec, vec, vec,
                _resident(wg.shape), _resident(wa.shape), _resident(wb.shape),
                _resident(wo.shape), _resident(wgu.shape), _resident(wd.shape)]
    row = lambda v: v.reshape(1, d)
    return pl.pallas_call(
        functools.partial(_post_kernel, alpha=alpha, d_ff=d_ff),
        out_shape=jax.ShapeDtypeStruct((bsz, t, d), F32),
        grid=(bsz * tiles,),
        in_specs=in_specs,
        out_specs=nat(d),
        compiler_params=pltpu.CompilerParams(dimension_semantics=("arbitrary",),
                                             vmem_limit_bytes=VMEM_LIMIT_POST),
        name="post",
    )(x, oa, ob, mod3, mod3, mod3, mod3, mod3, mod3,
      row(ln1_g), row(ln1_b), row(ln2_g), row(ln2_b), wg, wa, wb, wo, wgu, wd)


def kernel(x, c, positions, w_ada, b_ada, w_in, sinks, w_branch_a, w_branch_b, w_o,
           ln1_g, ln1_b, w_gate_up, w_down, ln2_g, ln2_b):
    depth = w_ada.shape[0]
    bsz, t, d = x.shape
    alpha = (2 * depth) ** 0.25
    a_q_w = A_Q_HEADS * HEAD_DIM
    a_kv_w = A_KV_HEADS * HEAD_DIM
    b_w = B_HEADS_PER_GROUP * len(B_PATTERNS) * HEAD_DIM
    cols = {"qa": 0, "qa_w": a_q_w, "ka": a_q_w, "kva_w": a_kv_w,
            "qb": a_q_w + 2 * a_kv_w, "kb": a_q_w + 2 * a_kv_w + b_w,
            "vb": a_q_w + 2 * a_kv_w + 2 * b_w, "bg_w": B_HEADS_PER_GROUP * HEAD_DIM}
    gate0 = a_q_w + 2 * a_kv_w + 3 * b_w

    ctab, stab = _rope_tables(positions)
    for l in range(depth):
        mod3 = _ada(c, w_ada[l], b_ada[l]).reshape(bsz, 1, 6 * d)
        w_qkv = w_in[l][:, :gate0].astype(BF16)
        w_gates = w_in[l][:, gate0:].astype(BF16)
        qa, ka, va, b1, b2, b3 = _inproj(x, mod3, ctab, stab, w_qkv, cols)
        oa = _attn_a(sinks[l], qa, ka, va)
        ob = _attn_b(b1, b2, b3)
        x = _post(x, oa, ob, mod3, ln1_g[l], ln1_b[l], ln2_g[l], ln2_b[l],
                  w_gates, w_branch_a[l].astype(BF16), w_branch_b[l].astype(BF16),
                  w_o[l].astype(BF16), w_gate_up[l].astype(BF16), w_down[l].astype(BF16), alpha)
    return x
```

```python
import functools

import jax
import jax.numpy as jnp
from jax import lax
from jax.experimental import pallas as pl
from jax.experimental.pallas import tpu as pltpu

F32 = jnp.float32
BF16 = jnp.bfloat16

HEAD_DIM = 64
HALF = HEAD_DIM // 2
BLOCK = 128
LANES = 128
A_Q_HEADS = 16
A_KV_HEADS = 2
A_GROUP = A_Q_HEADS // A_KV_HEADS
A_WINDOW = 128
B_PATTERNS = ((128, 1), (512, 4), (2048, 16))
B_HEADS_PER_GROUP = 8
ROPE_THETA = 10000.0
LN_EPS = 1e-5
NEG_INF = -1e30
LOG2E = 1.4426950408889634

TM_IN = 1024
SUB_IN = 512
TM_POST = 1024
SUB_POST = 512
FF_CHUNK = 256
B_PAIRS_PER_STEP = 2
A_SLABS_PER_DOT = 2
VMEM_BYTES = 64 * 1024 * 1024
VMEM_LIMIT = VMEM_BYTES - 8 * 1024 * 1024
VMEM_LIMIT_POST = VMEM_BYTES - 2 * 1024 * 1024

_NT = (((1,), (1,)), ((), ()))


def _resident(shape):
    nd = len(shape)
    return pl.BlockSpec(shape, lambda *_: (0,) * nd, pipeline_mode=pl.Buffered(1))


def _ada_kernel(c_ref, w_ref, b_ref, o_ref):
    c = c_ref[...]
    act = (c * jax.nn.sigmoid(c)).astype(BF16)
    o_ref[...] = jnp.dot(act, w_ref[...].astype(BF16), preferred_element_type=F32) + b_ref[...]


def _ada(c, w, b):
    bsz, d = c.shape
    e = w.shape[1]
    tn = d
    return pl.pallas_call(
        _ada_kernel,
        out_shape=jax.ShapeDtypeStruct((bsz, e), F32),
        grid=(e // tn,),
        in_specs=[pl.BlockSpec((bsz, d), lambda j: (0, 0)),
                  pl.BlockSpec((d, tn), lambda j: (0, j)),
                  pl.BlockSpec((1, tn), lambda j: (0, j))],
        out_specs=pl.BlockSpec((bsz, tn), lambda j: (0, j)),
        compiler_params=pltpu.CompilerParams(dimension_semantics=("arbitrary",)),
        name="ada",
    )(c, w, b.reshape(1, e))


def _rope_cols(y, ctab, stab, first_half, scale):
    outs = []
    for j in range(y.shape[1] // LANES):
        slab = y[:, j * LANES:(j + 1) * LANES]
        swapped = jnp.where(first_half,
                            pltpu.roll(slab, LANES - HALF, axis=1),
                            pltpu.roll(slab, HALF, axis=1))
        r = slab * ctab + swapped * stab
        if scale != 1.0:
            r = r * scale
        outs.append(r.astype(BF16))
    return outs[0] if len(outs) == 1 else jnp.concatenate(outs, axis=1)


def _inproj_kernel(x_ref, sc_ref, sh_ref, pos_ref, inv_ref, w_ref,
                   qa_ref, ka_ref, va_ref, b1_ref, b2_ref, b3_ref, us_ref, cn_ref, sn_ref, *, cols):
    tm, d = x_ref.shape
    nslab = d // LANES
    sub = SUB_IN
    lane = lax.broadcasted_iota(jnp.int32, (sub, LANES), 1)
    first_half = (lane & HALF) == 0
    qscale = HEAD_DIM ** -0.5 * LOG2E
    scale1 = 1.0 + sc_ref[...]
    shift = sh_ref[...]
    half_q = cols["qa_w"] // 2
    kvw = cols["kva_w"]
    gw = cols["bg_w"]

    def proj(u, c0, width):
        return jnp.dot(u, w_ref[:, c0:c0 + width], preferred_element_type=F32)

    for base in range(0, tm, sub):
        rows = slice(base, base + sub)
        uf = x_ref[rows, :] * scale1 + shift
        un = uf.astype(BF16)
        for j in range(nslab):
            us_ref[j, rows, :] = uf[:, j * LANES:(j + 1) * LANES]
        ang = pos_ref[:, rows] * inv_ref[...]
        cos, sin = jnp.cos(ang), jnp.sin(ang)
        reps = LANES // HEAD_DIM
        cn_ref[rows, :] = jnp.concatenate([cos, cos] * reps, axis=0).T
        sn_ref[rows, :] = jnp.concatenate([-sin, sin] * reps, axis=0).T

        def strided_rows(load, r):
            per = sub // r
            return jnp.concatenate([load(pl.ds(base + rho, per, stride=r)) for rho in range(r)], axis=0)

        def u_dilated(r):
            def load(idx):
                return jnp.concatenate([us_ref[j, idx, :] for j in range(nslab)], axis=1)
            return strided_rows(load, r).astype(BF16)

        def tab_dilated(ref, r):
            return strided_rows(lambda idx: ref[idx, :], r)

        def store_nat(c0, val):
            b1_ref[rows, c0:c0 + gw] = val

        def store_res(ref):
            nres = ref.shape[0]
            per = sub // nres
            off = base // nres

            def store(c0, val):
                for rho in range(nres):
                    ref[rho, off:off + per, c0:c0 + gw] = val[rho * per:(rho + 1) * per]
            return store

        groups = [(un, store_nat)] + [(u_dilated(B_PATTERNS[g][1]), store_res(ref))
                                      for g, ref in ((1, b2_ref), (2, b3_ref))]

        def values(g):
            u, store = groups[g]
            store(2 * gw, proj(u, cols["vb"] + g * gw, gw).astype(BF16))

        def queries_keys(g, ctab, stab):
            u, store = groups[g]
            store(0, _rope_cols(proj(u, cols["qb"] + g * gw, gw), ctab, stab, first_half, qscale))
            store(gw, _rope_cols(proj(u, cols["kb"] + g * gw, gw), ctab, stab, first_half, 1.0))

        values(0)
        cn, sn = cn_ref[rows, :], sn_ref[rows, :]
        for h in range(2):
            y = proj(un, cols["qa"] + h * half_q, half_q)
            qa_ref[rows, h * half_q:(h + 1) * half_q] = _rope_cols(y, cn, sn, first_half, qscale)
        y = proj(un, cols["ka"], 2 * kvw)
        ka_ref[rows, :] = _rope_cols(y[:, :kvw], cn, sn, first_half, 1.0)
        va_ref[rows, :] = y[:, kvw:].astype(BF16)
        queries_keys(0, cn, sn)
        for g in (1, 2):
            r = B_PATTERNS[g][1]
            queries_keys(g, tab_dilated(cn_ref, r), tab_dilated(sn_ref, r))
            values(g)


def _inproj(x, mod3, positions, w_bf, cols):
    bsz, t, d = x.shape
    tm = TM_IN
    tiles = t // tm
    r4, r16 = B_PATTERNS[1][1], B_PATTERNS[2][1]

    def bq(i):
        return i // tiles, i % tiles

    nat = lambda width: pl.BlockSpec((None, tm, width), lambda i: (bq(i)[0], bq(i)[1], 0))
    res = lambda r, width: pl.BlockSpec((None, r, tm // r, width), lambda i: (bq(i)[0], 0, bq(i)[1], 0))
    in_specs = [
        nat(d),
        pl.BlockSpec((None, 1, d), lambda i: (bq(i)[0], 0, 1)),
        pl.BlockSpec((None, 1, d), lambda i: (bq(i)[0], 0, 0)),
        pl.BlockSpec((None, 1, tm), lambda i: (bq(i)[0], 0, bq(i)[1])),
        pl.BlockSpec((HALF, 1), lambda i: (0, 0)),
        _resident(w_bf.shape),
    ]
    inv = ROPE_THETA ** (-jnp.arange(HALF, dtype=F32) / HALF)
    gw3 = 3 * cols["bg_w"]
    out_shape = [jax.ShapeDtypeStruct((bsz, t, cols["qa_w"]), BF16),
                 jax.ShapeDtypeStruct((bsz, t, cols["kva_w"]), BF16),
                 jax.ShapeDtypeStruct((bsz, t, cols["kva_w"]), BF16),
                 jax.ShapeDtypeStruct((bsz, t, gw3), BF16),
                 jax.ShapeDtypeStruct((bsz, r4, t // r4, gw3), BF16),
                 jax.ShapeDtypeStruct((bsz, r16, t // r16, gw3), BF16)]
    out_specs = [nat(cols["qa_w"]), nat(cols["kva_w"]), nat(cols["kva_w"]),
                 nat(gw3), res(r4, gw3), res(r16, gw3)]
    qa, ka, va, b1, b2, b3 = pl.pallas_call(
        functools.partial(_inproj_kernel, cols=cols),
        out_shape=out_shape,
        grid=(bsz * tiles,),
        in_specs=in_specs,
        out_specs=out_specs,
        scratch_shapes=[pltpu.VMEM((d // LANES, tm, LANES), F32),
                        pltpu.VMEM((tm, LANES), F32), pltpu.VMEM((tm, LANES), F32)],
        compiler_params=pltpu.CompilerParams(dimension_semantics=("arbitrary",),
                                             vmem_limit_bytes=VMEM_LIMIT),
        name="inproj",
    )(x, mod3, mod3, positions.astype(F32).reshape(bsz, 1, t), inv.reshape(HALF, 1), w_bf)
    return qa, ka, va, b1, b2.reshape(bsz, t, gw3), b3.reshape(bsz, t, gw3)


def _band_bias(keys, n_back, prev_valid):
    qi = lax.broadcasted_iota(jnp.int32, (BLOCK, keys), 0)
    c = lax.broadcasted_iota(jnp.int32, (BLOCK, keys), 1)
    dist = qi + (keys - BLOCK) - c
    ok = (dist >= 0) & (dist <= n_back)
    if not prev_valid:
        ok = ok & (c >= keys - BLOCK)
    return jnp.where(ok, 0.0, NEG_INF).astype(F32)


def _lane_consts():
    lane = lax.broadcasted_iota(jnp.int32, (1, LANES), 1)
    mlo = (lane < HEAD_DIM).astype(BF16)
    mhi = (lane >= HEAD_DIM).astype(BF16)
    lane_lo = lax.broadcasted_iota(jnp.int32, (BLOCK, LANES), 1) < HEAD_DIM
    return mlo, mhi, lane_lo


def _stack_scores(qs, kc, mlo, mhi):
    parts = []
    for q in qs:
        parts += [q * mlo, q * mhi]
    return lax.dot_general(jnp.concatenate(parts, axis=0), kc, _NT, preferred_element_type=F32)


def _stack_softmax(s, sinks2):
    heads = [s[h * BLOCK:(h + 1) * BLOCK] for h in range(s.shape[0] // BLOCK)]
    ms = [jnp.max(h, axis=-1, keepdims=True) for h in heads]
    if sinks2 is not None:
        ms = [jnp.maximum(m, sk) for m, sk in zip(ms, sinks2)]
    p = jnp.concatenate([jnp.exp2(h - m) for h, m in zip(heads, ms)], axis=0).astype(BF16)
    return p, ms


def _stack_values(p, vc, ms, lane_lo, sinks2, normalise):
    ones = jnp.ones(vc.shape, BF16)
    r = jnp.dot(p, jnp.concatenate([vc, ones], axis=1), preferred_element_type=F32)
    outs = []
    for j in range(len(ms) // 2):
        e, o = r[2 * j * BLOCK:(2 * j + 1) * BLOCK], r[(2 * j + 1) * BLOCK:(2 * j + 2) * BLOCK]
        me, mo = ms[2 * j], ms[2 * j + 1]
        num = jnp.where(lane_lo, e[:, :LANES], o[:, :LANES])
        den = jnp.where(lane_lo, e[:, LANES:], o[:, LANES:])
        if sinks2 is not None:
            den = den + jnp.where(lane_lo, jnp.exp2(sinks2[2 * j] - me), jnp.exp2(sinks2[2 * j + 1] - mo))
        outs.append(num / den if normalise else (num, den, jnp.where(lane_lo, me, mo)))
    return outs


def _pair_block(q, kc, vc, bias, mlo, mhi, lane_lo):
    p, ms = _stack_softmax(_stack_scores([q], kc, mlo, mhi) + bias, None)
    return _stack_values(p, vc, ms, lane_lo, None, False)[0]


def _window_block(qs, kc, vc, prev_bias, tri, tri_bf, ntri_bf, mlo, mhi, lane_lo, sinks2):
    s = _stack_scores(qs, kc, mlo, mhi)
    merged = jnp.where(tri, s[:, BLOCK:], s[:, :BLOCK] + prev_bias)
    p, ms = _stack_softmax(merged, sinks2)
    p2 = jnp.concatenate([p * ntri_bf, p * tri_bf], axis=1)
    return _stack_values(p2, vc, ms, lane_lo, sinks2, True)


def _attn_a_kernel(sink_ref, q_ref, k_ref, v_ref, o_ref, kd_ref, vd_ref):
    t = q_ref.shape[0]
    nblk = t // BLOCK
    nslab = A_GROUP // 2
    assert A_WINDOW == BLOCK
    mlo, mhi, lane_lo = _lane_consts()
    lane_lo_t = lax.broadcasted_iota(jnp.int32, (t, LANES), 1) < HEAD_DIM

    for src, dup in ((k_ref, kd_ref), (v_ref, vd_ref)):
        val = src[...].astype(F32)
        swp = pltpu.roll(val, HEAD_DIM, axis=1)
        dup[0] = jnp.where(lane_lo_t, val, swp).astype(BF16)
        dup[1] = jnp.where(lane_lo_t, swp, val).astype(BF16)
    stack = 2 * A_SLABS_PER_DOT * BLOCK
    qi = lax.broadcasted_iota(jnp.int32, (stack, BLOCK), 0) & (BLOCK - 1)
    tri = lax.broadcasted_iota(jnp.int32, (stack, BLOCK), 1) <= qi
    tri_bf = tri.astype(F32).astype(BF16)
    ntri_bf = 1.0 - tri_bf

    for i in range(nblk):
        r0 = i * BLOCK
        p0 = max(i - 1, 0) * BLOCK
        prev_bias = 0.0 if i > 0 else NEG_INF
        for hk in range(A_KV_HEADS):
            kc = jnp.concatenate([kd_ref[hk, pl.ds(p0, BLOCK), :], kd_ref[hk, pl.ds(r0, BLOCK), :]], axis=0)
            vc = jnp.concatenate([vd_ref[hk, pl.ds(p0, BLOCK), :], vd_ref[hk, pl.ds(r0, BLOCK), :]], axis=0)
            for s0 in range(0, nslab, A_SLABS_PER_DOT):
                slabs = [hk * nslab + s0 + u for u in range(A_SLABS_PER_DOT)]
                cols = [slice(sl * LANES, (sl + 1) * LANES) for sl in slabs]
                sinks2 = [sink_ref[2 * sl + h] * LOG2E for sl in slabs for h in range(2)]
                outs = _window_block([q_ref[pl.ds(r0, BLOCK), c] for c in cols], kc, vc, prev_bias,
                                     tri, tri_bf, ntri_bf, mlo, mhi, lane_lo, sinks2)
                for c, out in zip(cols, outs):
                    o_ref[pl.ds(r0, BLOCK), c] = out.astype(BF16)


def _attn_a(sinks, qa, ka, va):
    bsz, t, qw = qa.shape
    kw = ka.shape[2]
    seq = lambda width: pl.BlockSpec((None, t, width), lambda b: (b, 0, 0))
    return pl.pallas_call(
        _attn_a_kernel,
        out_shape=jax.ShapeDtypeStruct((bsz, t, qw), BF16),
        grid=(bsz,),
        in_specs=[pl.BlockSpec(memory_space=pltpu.SMEM), seq(qw), seq(kw), seq(kw)],
        out_specs=seq(qw),
        scratch_shapes=[pltpu.VMEM((A_KV_HEADS, t, LANES), BF16),
                        pltpu.VMEM((A_KV_HEADS, t, LANES), BF16)],
        compiler_params=pltpu.CompilerParams(dimension_semantics=("arbitrary",),
                                             vmem_limit_bytes=VMEM_LIMIT),
        name="attn_a",
    )(sinks, qa, ka, va)


def _attn_b_kernel(q1_ref, k1_ref, v1_ref, q2_ref, k2_ref, v2_ref, q3_ref, k3_ref, v3_ref,
                   o_ref, n1_s, d1_s, m1_s, n3_s, d3_s, m3_s, fin_s, bias_ref, bias1_ref):
    t = q1_ref.shape[0]
    mlo, mhi, lane_lo = _lane_consts()
    pair = functools.partial(_pair_block, mlo=mlo, mhi=mhi, lane_lo=lane_lo)
    (w1, _), (w4, r4), (w16, r16) = B_PATTERNS
    nb4 = t // r4 // BLOCK
    rows4 = t // r4
    assert w1 == w4 // r4 == w16 // r16 and t // r16 == BLOCK and r16 % r4 == 0

    twice = lambda b: jnp.concatenate([b, b], axis=0)
    bias_ref[...] = twice(_band_bias(2 * BLOCK, w1, True))
    bias1_ref[...] = twice(_band_bias(BLOCK, w1, True))

    def head_pair(pp):
        cs = slice(pp * LANES, (pp + 1) * LANES)
        g1_s, g3_s = (n1_s, d1_s, m1_s), (n3_s, d3_s, m3_s)

        def first_block(q_ref, k_ref, v_ref, r0):
            k = k_ref[pl.ds(r0, BLOCK), cs]
            s = _stack_scores([q_ref[pl.ds(r0, BLOCK), cs]], jnp.concatenate([k, k], axis=0), mlo, mhi)
            p, ms = _stack_softmax(s[:, :BLOCK] + bias1_ref[...], None)
            return _stack_values(p, v_ref[pl.ds(r0, BLOCK), cs], ms, lane_lo, None, False)[0]

        def banded(q_ref, k_ref, v_ref, r0):
            return pair(q_ref[pl.ds(r0, BLOCK), cs], k_ref[pl.ds(r0 - BLOCK, 2 * BLOCK), cs],
                        v_ref[pl.ds(r0 - BLOCK, 2 * BLOCK), cs], bias_ref[...])


        for rho in range(r16):
            stats = first_block(q3_ref, k3_ref, v3_ref, rho * BLOCK)
            dst = pl.ds((rho % r4) * rows4 + rho // r4, BLOCK, stride=r4)
            for ref, val in zip(g3_s, stats):
                ref[pp, dst, :] = val

        for i in range(t // BLOCK):
            r0 = i * BLOCK
            stats = first_block(q1_ref, k1_ref, v1_ref, r0) if i == 0 else banded(q1_ref, k1_ref, v1_ref, r0)
            for ref, val in zip(g1_s, stats):
                ref[pp, pl.ds(r0, BLOCK), :] = val

        for idx in range(r4 * nb4):
            rho, sb = idx // nb4, idx % nb4
            r0 = idx * BLOCK
            n2, d2, m2 = first_block(q2_ref, k2_ref, v2_ref, r0) if sb == 0 else banded(q2_ref, k2_ref, v2_ref, r0)
            nat = pl.ds(sb * (BLOCK * r4) + rho, BLOCK, stride=r4)
            n1, d1, m1 = (ref[pp, nat, :] for ref in g1_s)
            n3, d3, m3 = (ref[pp, pl.ds(r0, BLOCK), :] for ref in g3_s)
            mx = jnp.maximum(jnp.maximum(m1, m2), m3)
            a1, a2, a3 = jnp.exp2(m1 - mx), jnp.exp2(m2 - mx), jnp.exp2(m3 - mx)
            fin_s[pp, nat, :] = (a1 * n1 + a2 * n2 + a3 * n3) / (a1 * d1 + a2 * d2 + a3 * d3)

        o_ref[:, cs] = fin_s[pp].astype(BF16)

    for pp in range(o_ref.shape[1] // LANES):
        head_pair(pp)


def _attn_b(b1, b2, b3):
    bsz, t, w3 = b1.shape
    gw = w3 // 3
    wblk = B_PAIRS_PER_STEP * LANES
    nblk = gw // wblk

    def spec(part):
        return pl.BlockSpec((None, t, wblk), lambda b, j: (b, 0, part * nblk + j))

    return pl.pallas_call(
        _attn_b_kernel,
        out_shape=jax.ShapeDtypeStruct((bsz, t, gw), BF16),
        grid=(bsz, nblk),
        in_specs=[spec(0), spec(1), spec(2)] * 3,
        out_specs=pl.BlockSpec((None, t, wblk), lambda b, j: (b, 0, j)),
        scratch_shapes=[pltpu.VMEM((B_PAIRS_PER_STEP, t, LANES), F32)] * 7
                       + [pltpu.VMEM((2 * BLOCK, 2 * BLOCK), F32),
                          pltpu.VMEM((2 * BLOCK, BLOCK), F32)],
        compiler_params=pltpu.CompilerParams(dimension_semantics=("arbitrary", "arbitrary"),
                                             vmem_limit_bytes=VMEM_LIMIT),
        name="attn_b",
    )(b1, b1, b1, b2, b2, b2, b3, b3, b3)


def _layer_norm(h, g, b):
    mu = jnp.mean(h, axis=-1, keepdims=True)
    xc = h - mu
    var = jnp.mean(xc * xc, axis=-1, keepdims=True)
    return xc * lax.rsqrt(var + LN_EPS) * g + b


def _post_kernel(x_ref, oa_ref, ob_ref, scm_ref, shm_ref, gm_ref, shf_ref, scf_ref, gf_ref,
                 g1_ref, b1_ref, g2_ref, b2_ref,
                 wg_ref, wa_ref, wb_ref, wo_ref, wgu_ref, wd_ref, o_ref, *, alpha, d_ff):
    d = x_ref.shape[1]

    def mix(rows):
        x = x_ref[rows, :]
        u = (x * (1.0 + scm_ref[...]) + shm_ref[...]).astype(BF16)
        oa, ob = oa_ref[rows, :], ob_ref[rows, :]
        parts = []
        for c0 in range(0, d, FF_CHUNK):
            ga = jax.nn.sigmoid(jnp.dot(u, wg_ref[:, c0:c0 + FF_CHUNK], preferred_element_type=F32))
            ya = jnp.dot(oa, wa_ref[:, c0:c0 + FF_CHUNK], preferred_element_type=F32)
            gb = jax.nn.sigmoid(jnp.dot(u, wg_ref[:, d + c0:d + c0 + FF_CHUNK], preferred_element_type=F32))
            yb = jnp.dot(ob, wb_ref[:, c0:c0 + FF_CHUNK], preferred_element_type=F32)
            parts.append((ga * ya + gb * yb).astype(BF16))
        return jnp.dot(jnp.concatenate(parts, axis=1), wo_ref[...], preferred_element_type=F32)

    def norm1(rows, y):
        return _layer_norm(alpha * x_ref[rows, :] + (1.0 + gm_ref[...]) * y, g1_ref[...], b1_ref[...])

    def ffn(x1):
        u2 = (x1 * (1.0 + scf_ref[...]) + shf_ref[...]).astype(BF16)
        acts = []
        for c in range(d_ff // FF_CHUNK):
            c0 = c * FF_CHUNK
            hg = jnp.dot(u2, wgu_ref[:, c0:c0 + FF_CHUNK], preferred_element_type=F32)
            hu = jnp.dot(u2, wgu_ref[:, d_ff + c0:d_ff + c0 + FF_CHUNK], preferred_element_type=F32)
            acts.append((hg * jax.nn.sigmoid(hg) * hu).astype(BF16))
        return jnp.dot(jnp.concatenate(acts, axis=1), wd_ref[...], preferred_element_type=F32)

    def norm2(rows, x1, acc):
        o_ref[rows, :] = _layer_norm(alpha * x1 + (1.0 + gf_ref[...]) * acc, g2_ref[...], b2_ref[...])

    subs = [slice(r0, r0 + SUB_POST) for r0 in range(0, x_ref.shape[0], SUB_POST)]
    ys = [mix(rows) for rows in subs]
    x1s, accs = [], []
    for i, rows in enumerate(subs):
        x1s.append(norm1(rows, ys[i]))
        accs.append(ffn(x1s[i]))
        if i > 0:
            norm2(subs[i - 1], x1s[i - 1], accs[i - 1])
    norm2(subs[-1], x1s[-1], accs[-1])


def _post(x, oa, ob, mod3, ln1_g, ln1_b, ln2_g, ln2_b, wg, wa, wb, wo, wgu, wd, alpha):
    bsz, t, d = x.shape
    tm = TM_POST
    tiles = t // tm
    d_ff = wd.shape[0]

    def bq(i):
        return i // tiles, i % tiles

    nat = lambda width: pl.BlockSpec((None, tm, width), lambda i: (bq(i)[0], bq(i)[1], 0))
    modspec = lambda k: pl.BlockSpec((None, 1, d), lambda i: (bq(i)[0], 0, k))
    vec = pl.BlockSpec((1, d), lambda i: (0, 0))
    in_specs = [nat(d), nat(oa.shape[2]), nat(ob.shape[2]),
                modspec(1), modspec(0), modspec(2), modspec(3), modspec(4), modspec(5),
                vec, vec, vec, vec,
                _resident(wg.shape), _resident(wa.shape), _resident(wb.shape),
                _resident(wo.shape), _resident(wgu.shape), _resident(wd.shape)]
    row = lambda v: v.reshape(1, d)
    return pl.pallas_call(
        functools.partial(_post_kernel, alpha=alpha, d_ff=d_ff),
        out_shape=jax.ShapeDtypeStruct((bsz, t, d), F32),
        grid=(bsz * tiles,),
        in_specs=in_specs,
        out_specs=nat(d),
        compiler_params=pltpu.CompilerParams(dimension_semantics=("arbitrary",),
                                             vmem_limit_bytes=VMEM_LIMIT_POST),
        name="post",
    )(x, oa, ob, mod3, mod3, mod3, mod3, mod3, mod3,
      row(ln1_g), row(ln1_b), row(ln2_g), row(ln2_b), wg, wa, wb, wo, wgu, wd)


def kernel(x, c, positions, w_ada, b_ada, w_in, sinks, w_branch_a, w_branch_b, w_o,
           ln1_g, ln1_b, w_gate_up, w_down, ln2_g, ln2_b):
    depth = w_ada.shape[0]
    bsz, t, d = x.shape
    alpha = (2 * depth) ** 0.25
    a_q_w = A_Q_HEADS * HEAD_DIM
    a_kv_w = A_KV_HEADS * HEAD_DIM
    b_w = B_HEADS_PER_GROUP * len(B_PATTERNS) * HEAD_DIM
    cols = {"qa": 0, "qa_w": a_q_w, "ka": a_q_w, "kva_w": a_kv_w,
            "qb": a_q_w + 2 * a_kv_w, "kb": a_q_w + 2 * a_kv_w + b_w,
            "vb": a_q_w + 2 * a_kv_w + 2 * b_w, "bg_w": B_HEADS_PER_GROUP * HEAD_DIM}
    gate0 = a_q_w + 2 * a_kv_w + 3 * b_w

    for l in range(depth):
        mod3 = _ada(c, w_ada[l], b_ada[l]).reshape(bsz, 1, 6 * d)
        w_qkv = w_in[l][:, :gate0].astype(BF16)
        w_gates = w_in[l][:, gate0:].astype(BF16)
        qa, ka, va, b1, b2, b3 = _inproj(x, mod3, positions, w_qkv, cols)
        oa = _attn_a(sinks[l], qa, ka, va)
        ob = _attn_b(b1, b2, b3)
        x = _post(x, oa, ob, mod3, ln1_g[l], ln1_b[l], ln2_g[l], ln2_b[l],
                  w_gates, w_branch_a[l].astype(BF16), w_branch_b[l].astype(BF16),
                  w_o[l].astype(BF16), w_gate_up[l].astype(BF16), w_down[l].astype(BF16), alpha)
    return x
```

```python
import functools

import jax
import jax.numpy as jnp
from jax import lax
from jax.experimental import pallas as pl
from jax.experimental.pallas import tpu as pltpu

F32 = jnp.float32
BF16 = jnp.bfloat16

HEAD_DIM = 64
HALF = HEAD_DIM // 2
BLOCK = 128
LANES = 128
A_Q_HEADS = 16
A_KV_HEADS = 2
A_GROUP = A_Q_HEADS // A_KV_HEADS
A_WINDOW = 128
B_PATTERNS = ((128, 1), (512, 4), (2048, 16))
B_HEADS_PER_GROUP = 8
ROPE_THETA = 10000.0
LN_EPS = 1e-5
NEG_INF = -1e30
LOG2E = 1.4426950408889634

TM_IN = 1024
SUB_IN = 512
TM_POST = 1024
SUB_POST = 512
FF_CHUNK = 256
B_PAIRS_PER_STEP = 2
A_SLABS_PER_DOT = 2
VMEM_BYTES = 64 * 1024 * 1024
VMEM_LIMIT = VMEM_BYTES - 8 * 1024 * 1024
VMEM_LIMIT_POST = VMEM_BYTES - 2 * 1024 * 1024

_NT = (((1,), (1,)), ((), ()))


def _resident(shape):
    nd = len(shape)
    return pl.BlockSpec(shape, lambda *_: (0,) * nd, pipeline_mode=pl.Buffered(1))


def _ada_kernel(c_ref, w_ref, b_ref, o_ref):
    c = c_ref[...]
    act = (c * jax.nn.sigmoid(c)).astype(BF16)
    o_ref[...] = jnp.dot(act, w_ref[...].astype(BF16), preferred_element_type=F32) + b_ref[...]


def _ada(c, w, b):
    bsz, d = c.shape
    e = w.shape[1]
    tn = d
    return pl.pallas_call(
        _ada_kernel,
        out_shape=jax.ShapeDtypeStruct((bsz, e), F32),
        grid=(e // tn,),
        in_specs=[pl.BlockSpec((bsz, d), lambda j: (0, 0)),
                  pl.BlockSpec((d, tn), lambda j: (0, j)),
                  pl.BlockSpec((1, tn), lambda j: (0, j))],
        out_specs=pl.BlockSpec((bsz, tn), lambda j: (0, j)),
        compiler_params=pltpu.CompilerParams(dimension_semantics=("arbitrary",)),
        name="ada",
    )(c, w, b.reshape(1, e))


def _rope_cols(y, ctab, stab, first_half, scale):
    outs = []
    for j in range(y.shape[1] // LANES):
        slab = y[:, j * LANES:(j + 1) * LANES]
        swapped = jnp.where(first_half,
                            pltpu.roll(slab, LANES - HALF, axis=1),
                            pltpu.roll(slab, HALF, axis=1))
        r = slab * ctab + swapped * stab
        if scale != 1.0:
            r = r * scale
        outs.append(r.astype(BF16))
    return outs[0] if len(outs) == 1 else jnp.concatenate(outs, axis=1)


def _inproj_kernel(x_ref, sc_ref, sh_ref, pos_ref, inv_ref, w_ref,
                   qa_ref, ka_ref, va_ref, b1_ref, b2_ref, b3_ref, us_ref, cn_ref, sn_ref, *, cols):
    tm, d = x_ref.shape
    nslab = d // LANES
    sub = SUB_IN
    lane = lax.broadcasted_iota(jnp.int32, (sub, LANES), 1)
    first_half = (lane & HALF) == 0
    qscale = HEAD_DIM ** -0.5 * LOG2E
    scale1 = 1.0 + sc_ref[...]
    shift = sh_ref[...]
    half_q = cols["qa_w"] // 2
    kvw = cols["kva_w"]
    gw = cols["bg_w"]

    def proj(u, c0, width):
        return jnp.dot(u, w_ref[:, c0:c0 + width], preferred_element_type=F32)

    for base in range(0, tm, sub):
        rows = slice(base, base + sub)
        uf = x_ref[rows, :] * scale1 + shift
        un = uf.astype(BF16)
        for j in range(nslab):
            us_ref[j, rows, :] = uf[:, j * LANES:(j + 1) * LANES]
        ang = pos_ref[:, rows] * inv_ref[...]
        cos, sin = jnp.cos(ang), jnp.sin(ang)
        reps = LANES // HEAD_DIM
        cn_ref[rows, :] = jnp.concatenate([cos, cos] * reps, axis=0).T
        sn_ref[rows, :] = jnp.concatenate([-sin, sin] * reps, axis=0).T

        def strided_rows(load, r):
            per = sub // r
            return jnp.concatenate([load(pl.ds(base + rho, per, stride=r)) for rho in range(r)], axis=0)

        def u_dilated(r):
            def load(idx):
                return jnp.concatenate([us_ref[j, idx, :] for j in range(nslab)], axis=1)
            return strided_rows(load, r).astype(BF16)

        def tab_dilated(ref, r):
            return strided_rows(lambda idx: ref[idx, :], r)

        def store_nat(c0, val):
            b1_ref[rows, c0:c0 + gw] = val

        def store_res(ref):
            nres = ref.shape[0]
            per = sub // nres
            off = base // nres

            def store(c0, val):
                for rho in range(nres):
                    ref[rho, off:off + per, c0:c0 + gw] = val[rho * per:(rho + 1) * per]
            return store

        groups = [(un, store_nat)] + [(u_dilated(B_PATTERNS[g][1]), store_res(ref))
                                      for g, ref in ((1, b2_ref), (2, b3_ref))]

        def values(g):
            u, store = groups[g]
            store(2 * gw, proj(u, cols["vb"] + g * gw, gw).astype(BF16))

        def queries_keys(g, ctab, stab):
            u, store = groups[g]
            store(0, _rope_cols(proj(u, cols["qb"] + g * gw, gw), ctab, stab, first_half, qscale))
            store(gw, _rope_cols(proj(u, cols["kb"] + g * gw, gw), ctab, stab, first_half, 1.0))

        values(0)
        cn, sn = cn_ref[rows, :], sn_ref[rows, :]
        for h in range(2):
            y = proj(un, cols["qa"] + h * half_q, half_q)
            qa_ref[rows, h * half_q:(h + 1) * half_q] = _rope_cols(y, cn, sn, first_half, qscale)
        y = proj(un, cols["ka"], 2 * kvw)
        ka_ref[rows, :] = _rope_cols(y[:, :kvw], cn, sn, first_half, 1.0)
        va_ref[rows, :] = y[:, kvw:].astype(BF16)
        queries_keys(0, cn, sn)
        for g in (1, 2):
            r = B_PATTERNS[g][1]
            queries_keys(g, tab_dilated(cn_ref, r), tab_dilated(sn_ref, r))
            values(g)


def _inproj(x, mod3, positions, w_bf, cols):
    bsz, t, d = x.shape
    tm = TM_IN
    tiles = t // tm
    r4, r16 = B_PATTERNS[1][1], B_PATTERNS[2][1]

    def bq(i):
        return i // tiles, i % tiles

    nat = lambda width: pl.BlockSpec((None, tm, width), lambda i: (bq(i)[0], bq(i)[1], 0))
    res = lambda r, width: pl.BlockSpec((None, r, tm // r, width), lambda i: (bq(i)[0], 0, bq(i)[1], 0))
    in_specs = [
        nat(d),
        pl.BlockSpec((None, 1, d), lambda i: (bq(i)[0], 0, 1)),
        pl.BlockSpec((None, 1, d), lambda i: (bq(i)[0], 0, 0)),
        pl.BlockSpec((None, 1, tm), lambda i: (bq(i)[0], 0, bq(i)[1])),
        pl.BlockSpec((HALF, 1), lambda i: (0, 0)),
        _resident(w_bf.shape),
    ]
    inv = ROPE_THETA ** (-jnp.arange(HALF, dtype=F32) / HALF)
    gw3 = 3 * cols["bg_w"]
    out_shape = [jax.ShapeDtypeStruct((bsz, t, cols["qa_w"]), BF16),
                 jax.ShapeDtypeStruct((bsz, t, cols["kva_w"]), BF16),
                 jax.ShapeDtypeStruct((bsz, t, cols["kva_w"]), BF16),
                 jax.ShapeDtypeStruct((bsz, t, gw3), BF16),
                 jax.ShapeDtypeStruct((bsz, r4, t // r4, gw3), BF16),
                 jax.ShapeDtypeStruct((bsz, r16, t // r16, gw3), BF16)]
    out_specs = [nat(cols["qa_w"]), nat(cols["kva_w"]), nat(cols["kva_w"]),
                 nat(gw3), res(r4, gw3), res(r16, gw3)]
    qa, ka, va, b1, b2, b3 = pl.pallas_call(
        functools.partial(_inproj_kernel, cols=cols),
        out_shape=out_shape,
        grid=(bsz * tiles,),
        in_specs=in_specs,
        out_specs=out_specs,
        scratch_shapes=[pltpu.VMEM((d // LANES, tm, LANES), F32),
                        pltpu.VMEM((tm, LANES), F32), pltpu.VMEM((tm, LANES), F32)],
        compiler_params=pltpu.CompilerParams(dimension_semantics=("arbitrary",),
                                             vmem_limit_bytes=VMEM_LIMIT),
        name="inproj",
    )(x, mod3, mod3, positions.astype(F32).reshape(bsz, 1, t), inv.reshape(HALF, 1), w_bf)
    return qa, ka, va, b1, b2.reshape(bsz, t, gw3), b3.reshape(bsz, t, gw3)


def _band_bias(keys, n_back, prev_valid):
    qi = lax.broadcasted_iota(jnp.int32, (BLOCK, keys), 0)
    c = lax.broadcasted_iota(jnp.int32, (BLOCK, keys), 1)
    dist = qi + (keys - BLOCK) - c
    ok = (dist >= 0) & (dist <= n_back)
    if not prev_valid:
        ok = ok & (c >= keys - BLOCK)
    return jnp.where(ok, 0.0, NEG_INF).astype(F32)


def _lane_consts():
    lane = lax.broadcasted_iota(jnp.int32, (1, LANES), 1)
    mlo = (lane < HEAD_DIM).astype(BF16)
    mhi = (lane >= HEAD_DIM).astype(BF16)
    lane_lo = lax.broadcasted_iota(jnp.int32, (BLOCK, LANES), 1) < HEAD_DIM
    return mlo, mhi, lane_lo


def _stack_scores(qs, kc, mlo, mhi):
    parts = []
    for q in qs:
        parts += [q * mlo, q * mhi]
    return lax.dot_general(jnp.concatenate(parts, axis=0), kc, _NT, preferred_element_type=F32)


def _stack_softmax(s, sinks2):
    heads = [s[h * BLOCK:(h + 1) * BLOCK] for h in range(s.shape[0] // BLOCK)]
    ms = [jnp.max(h, axis=-1, keepdims=True) for h in heads]
    if sinks2 is not None:
        ms = [jnp.maximum(m, sk) for m, sk in zip(ms, sinks2)]
    p = jnp.concatenate([jnp.exp2(h - m) for h, m in zip(heads, ms)], axis=0).astype(BF16)
    return p, ms


def _stack_values(p, vc, ms, lane_lo, sinks2, normalise):
    ones = jnp.ones(vc.shape, BF16)
    r = jnp.dot(p, jnp.concatenate([vc, ones], axis=1), preferred_element_type=F32)
    outs = []
    for j in range(len(ms) // 2):
        e, o = r[2 * j * BLOCK:(2 * j + 1) * BLOCK], r[(2 * j + 1) * BLOCK:(2 * j + 2) * BLOCK]
        me, mo = ms[2 * j], ms[2 * j + 1]
        num = jnp.where(lane_lo, e[:, :LANES], o[:, :LANES])
        den = jnp.where(lane_lo, e[:, LANES:], o[:, LANES:])
        if sinks2 is not None:
            den = den + jnp.where(lane_lo, jnp.exp2(sinks2[2 * j] - me), jnp.exp2(sinks2[2 * j + 1] - mo))
        outs.append(num / den if normalise else (num, den, jnp.where(lane_lo, me, mo)))
    return outs


def _pair_block(q, kc, vc, bias, mlo, mhi, lane_lo):
    p, ms = _stack_softmax(_stack_scores([q], kc, mlo, mhi) + bias, None)
    return _stack_values(p, vc, ms, lane_lo, None, False)[0]


def _window_block(qs, kc, vc, prev_bias, tri, tri_bf, ntri_bf, mlo, mhi, lane_lo, sinks2):
    s = _stack_scores(qs, kc, mlo, mhi)
    merged = jnp.where(tri, s[:, BLOCK:], s[:, :BLOCK] + prev_bias)
    p, ms = _stack_softmax(merged, sinks2)
    p2 = jnp.concatenate([p * ntri_bf, p * tri_bf], axis=1)
    return _stack_values(p2, vc, ms, lane_lo, sinks2, True)


def _attn_a_kernel(sink_ref, q_ref, k_ref, v_ref, *rest, casts):
    nw = len(casts)
    w_refs, o_ref, wbf_refs, (kd_ref, vd_ref) = rest[:nw], rest[nw], rest[nw + 1:2 * nw + 1], rest[2 * nw + 1:]
    for w_ref, wbf_ref, (c0, width) in zip(w_refs, wbf_refs, casts):
        wbf_ref[...] = w_ref[:, c0:c0 + width].astype(BF16)
    t = q_ref.shape[0]
    nblk = t // BLOCK
    nslab = A_GROUP // 2
    assert A_WINDOW == BLOCK
    mlo, mhi, lane_lo = _lane_consts()
    lane_lo_t = lax.broadcasted_iota(jnp.int32, (t, LANES), 1) < HEAD_DIM

    for src, dup in ((k_ref, kd_ref), (v_ref, vd_ref)):
        val = src[...].astype(F32)
        swp = pltpu.roll(val, HEAD_DIM, axis=1)
        dup[0] = jnp.where(lane_lo_t, val, swp).astype(BF16)
        dup[1] = jnp.where(lane_lo_t, swp, val).astype(BF16)
    stack = 2 * A_SLABS_PER_DOT * BLOCK
    qi = lax.broadcasted_iota(jnp.int32, (stack, BLOCK), 0) & (BLOCK - 1)
    tri = lax.broadcasted_iota(jnp.int32, (stack, BLOCK), 1) <= qi
    tri_bf = tri.astype(F32).astype(BF16)
    ntri_bf = 1.0 - tri_bf

    for i in range(nblk):
        r0 = i * BLOCK
        p0 = max(i - 1, 0) * BLOCK
        prev_bias = 0.0 if i > 0 else NEG_INF
        for hk in range(A_KV_HEADS):
            kc = jnp.concatenate([kd_ref[hk, pl.ds(p0, BLOCK), :], kd_ref[hk, pl.ds(r0, BLOCK), :]], axis=0)
            vc = jnp.concatenate([vd_ref[hk, pl.ds(p0, BLOCK), :], vd_ref[hk, pl.ds(r0, BLOCK), :]], axis=0)
            for s0 in range(0, nslab, A_SLABS_PER_DOT):
                slabs = [hk * nslab + s0 + u for u in range(A_SLABS_PER_DOT)]
                cols = [slice(sl * LANES, (sl + 1) * LANES) for sl in slabs]
                sinks2 = [sink_ref[2 * sl + h] * LOG2E for sl in slabs for h in range(2)]
                outs = _window_block([q_ref[pl.ds(r0, BLOCK), c] for c in cols], kc, vc, prev_bias,
                                     tri, tri_bf, ntri_bf, mlo, mhi, lane_lo, sinks2)
                for c, out in zip(cols, outs):
                    o_ref[pl.ds(r0, BLOCK), c] = out.astype(BF16)


def _attn_a(sinks, qa, ka, va, weights):
    bsz, t, qw = qa.shape
    kw = ka.shape[2]
    seq = lambda width: pl.BlockSpec((None, t, width), lambda b: (b, 0, 0))
    chunk = lambda w, width: pl.BlockSpec((w.shape[0] // bsz, width), lambda b: (b, 0))
    outs = pl.pallas_call(
        functools.partial(_attn_a_kernel, casts=tuple((c0, width) for _, c0, width in weights)),
        out_shape=[jax.ShapeDtypeStruct((bsz, t, qw), BF16)]
                  + [jax.ShapeDtypeStruct((w.shape[0], width), BF16) for w, _, width in weights],
        grid=(bsz,),
        in_specs=[pl.BlockSpec(memory_space=pltpu.SMEM), seq(qw), seq(kw), seq(kw)]
                 + [chunk(w, w.shape[1]) for w, _, _ in weights],
        out_specs=[seq(qw)] + [chunk(w, width) for w, _, width in weights],
        scratch_shapes=[pltpu.VMEM((A_KV_HEADS, t, LANES), BF16),
                        pltpu.VMEM((A_KV_HEADS, t, LANES), BF16)],
        compiler_params=pltpu.CompilerParams(dimension_semantics=("arbitrary",),
                                             vmem_limit_bytes=VMEM_LIMIT),
        name="attn_a",
    )(sinks, qa, ka, va, *[w for w, _, _ in weights])
    return outs[0], outs[1:]


def _attn_b_kernel(q1_ref, k1_ref, v1_ref, q2_ref, k2_ref, v2_ref, q3_ref, k3_ref, v3_ref,
                   o_ref, n1_s, d1_s, m1_s, n3_s, d3_s, m3_s, fin_s, bias_ref, bias1_ref):
    t = q1_ref.shape[0]
    mlo, mhi, lane_lo = _lane_consts()
    pair = functools.partial(_pair_block, mlo=mlo, mhi=mhi, lane_lo=lane_lo)
    (w1, _), (w4, r4), (w16, r16) = B_PATTERNS
    nb4 = t // r4 // BLOCK
    rows4 = t // r4
    assert w1 == w4 // r4 == w16 // r16 and t // r16 == BLOCK and r16 % r4 == 0

    twice = lambda b: jnp.concatenate([b, b], axis=0)
    bias_ref[...] = twice(_band_bias(2 * BLOCK, w1, True))
    bias1_ref[...] = twice(_band_bias(BLOCK, w1, True))

    def head_pair(pp):
        cs = slice(pp * LANES, (pp + 1) * LANES)
        g1_s, g3_s = (n1_s, d1_s, m1_s), (n3_s, d3_s, m3_s)

        def first_block(q_ref, k_ref, v_ref, r0):
            k = k_ref[pl.ds(r0, BLOCK), cs]
            s = _stack_scores([q_ref[pl.ds(r0, BLOCK), cs]], jnp.concatenate([k, k], axis=0), mlo, mhi)
            p, ms = _stack_softmax(s[:, :BLOCK] + bias1_ref[...], None)
            return _stack_values(p, v_ref[pl.ds(r0, BLOCK), cs], ms, lane_lo, None, False)[0]

        def banded(q_ref, k_ref, v_ref, r0):
            return pair(q_ref[pl.ds(r0, BLOCK), cs], k_ref[pl.ds(r0 - BLOCK, 2 * BLOCK), cs],
                        v_ref[pl.ds(r0 - BLOCK, 2 * BLOCK), cs], bias_ref[...])


        for rho in range(r16):
            stats = first_block(q3_ref, k3_ref, v3_ref, rho * BLOCK)
            dst = pl.ds((rho % r4) * rows4 + rho // r4, BLOCK, stride=r4)
            for ref, val in zip(g3_s, stats):
                ref[pp, dst, :] = val

        for i in range(t // BLOCK):
            r0 = i * BLOCK
            stats = first_block(q1_ref, k1_ref, v1_ref, r0) if i == 0 else banded(q1_ref, k1_ref, v1_ref, r0)
            for ref, val in zip(g1_s, stats):
                ref[pp, pl.ds(r0, BLOCK), :] = val

        for idx in range(r4 * nb4):
            rho, sb = idx // nb4, idx % nb4
            r0 = idx * BLOCK
            n2, d2, m2 = first_block(q2_ref, k2_ref, v2_ref, r0) if sb == 0 else banded(q2_ref, k2_ref, v2_ref, r0)
            nat = pl.ds(sb * (BLOCK * r4) + rho, BLOCK, stride=r4)
            n1, d1, m1 = (ref[pp, nat, :] for ref in g1_s)
            n3, d3, m3 = (ref[pp, pl.ds(r0, BLOCK), :] for ref in g3_s)
            mx = jnp.maximum(jnp.maximum(m1, m2), m3)
            a1, a2, a3 = jnp.exp2(m1 - mx), jnp.exp2(m2 - mx), jnp.exp2(m3 - mx)
            fin_s[pp, nat, :] = (a1 * n1 + a2 * n2 + a3 * n3) / (a1 * d1 + a2 * d2 + a3 * d3)

        o_ref[:, cs] = fin_s[pp].astype(BF16)

    for pp in range(o_ref.shape[1] // LANES):
        head_pair(pp)


def _attn_b(b1, b2, b3):
    bsz, t, w3 = b1.shape
    gw = w3 // 3
    wblk = B_PAIRS_PER_STEP * LANES
    nblk = gw // wblk

    def spec(part):
        return pl.BlockSpec((None, t, wblk), lambda b, j: (b, 0, part * nblk + j))

    return pl.pallas_call(
        _attn_b_kernel,
        out_shape=jax.ShapeDtypeStruct((bsz, t, gw), BF16),
        grid=(bsz, nblk),
        in_specs=[spec(0), spec(1), spec(2)] * 3,
        out_specs=pl.BlockSpec((None, t, wblk), lambda b, j: (b, 0, j)),
        scratch_shapes=[pltpu.VMEM((B_PAIRS_PER_STEP, t, LANES), F32)] * 7
                       + [pltpu.VMEM((2 * BLOCK, 2 * BLOCK), F32),
                          pltpu.VMEM((2 * BLOCK, BLOCK), F32)],
        compiler_params=pltpu.CompilerParams(dimension_semantics=("arbitrary", "arbitrary"),
                                             vmem_limit_bytes=VMEM_LIMIT),
        name="attn_b",
    )(b1, b1, b1, b2, b2, b2, b3, b3, b3)


def _layer_norm(h, g, b):
    mu = jnp.mean(h, axis=-1, keepdims=True)
    xc = h - mu
    var = jnp.mean(xc * xc, axis=-1, keepdims=True)
    return xc * lax.rsqrt(var + LN_EPS) * g + b


def _post_kernel(x_ref, oa_ref, ob_ref, scm_ref, shm_ref, gm_ref, shf_ref, scf_ref, gf_ref,
                 g1_ref, b1_ref, g2_ref, b2_ref,
                 wg_ref, wa_ref, wb_ref, wo_ref, wgu_ref, wd_ref, o_ref, *, alpha, d_ff):
    d = x_ref.shape[1]

    def mix(rows):
        x = x_ref[rows, :]
        u = (x * (1.0 + scm_ref[...]) + shm_ref[...]).astype(BF16)
        oa, ob = oa_ref[rows, :], ob_ref[rows, :]
        parts = []
        for c0 in range(0, d, FF_CHUNK):
            ga = jax.nn.sigmoid(jnp.dot(u, wg_ref[:, c0:c0 + FF_CHUNK], preferred_element_type=F32))
            ya = jnp.dot(oa, wa_ref[:, c0:c0 + FF_CHUNK], preferred_element_type=F32)
            gb = jax.nn.sigmoid(jnp.dot(u, wg_ref[:, d + c0:d + c0 + FF_CHUNK], preferred_element_type=F32))
            yb = jnp.dot(ob, wb_ref[:, c0:c0 + FF_CHUNK], preferred_element_type=F32)
            parts.append((ga * ya + gb * yb).astype(BF16))
        return jnp.dot(jnp.concatenate(parts, axis=1), wo_ref[...], preferred_element_type=F32)

    def norm1(rows, y):
        return _layer_norm(alpha * x_ref[rows, :] + (1.0 + gm_ref[...]) * y, g1_ref[...], b1_ref[...])

    def ffn(x1):
        u2 = (x1 * (1.0 + scf_ref[...]) + shf_ref[...]).astype(BF16)
        acts = []
        for c in range(d_ff // FF_CHUNK):
            c0 = c * FF_CHUNK
            hg = jnp.dot(u2, wgu_ref[:, c0:c0 + FF_CHUNK], preferred_element_type=F32)
            hu = jnp.dot(u2, wgu_ref[:, d_ff + c0:d_ff + c0 + FF_CHUNK], preferred_element_type=F32)
            acts.append((hg * jax.nn.sigmoid(hg) * hu).astype(BF16))
        return jnp.dot(jnp.concatenate(acts, axis=1), wd_ref[...], preferred_element_type=F32)

    def norm2(rows, x1, acc):
        o_ref[rows, :] = _layer_norm(alpha * x1 + (1.0 + gf_ref[...]) * acc, g2_ref[...], b2_ref[...])

    subs = [slice(r0, r0 + SUB_POST) for r0 in range(0, x_ref.shape[0], SUB_POST)]
    ys = [mix(rows) for rows in subs]
    x1s, accs = [], []
    for i, rows in enumerate(subs):
        x1s.append(norm1(rows, ys[i]))
        accs.append(ffn(x1s[i]))
        if i > 0:
            norm2(subs[i - 1], x1s[i - 1], accs[i - 1])
    norm2(subs[-1], x1s[-1], accs[-1])


def _post(x, oa, ob, mod3, ln1_g, ln1_b, ln2_g, ln2_b, wg, wa, wb, wo, wgu, wd, alpha):
    bsz, t, d = x.shape
    tm = TM_POST
    tiles = t // tm
    d_ff = wd.shape[0]

    def bq(i):
        return i // tiles, i % tiles

    nat = lambda width: pl.BlockSpec((None, tm, width), lambda i: (bq(i)[0], bq(i)[1], 0))
    modspec = lambda k: pl.BlockSpec((None, 1, d), lambda i: (bq(i)[0], 0, k))
    vec = pl.BlockSpec((1, d), lambda i: (0, 0))
    in_specs = [nat(d), nat(oa.shape[2]), nat(ob.shape[2]),
                modspec(1), modspec(0), modspec(2), modspec(3), modspec(4), modspec(5),
                vec, vec, vec, vec,
                _resident(wg.shape), _resident(wa.shape), _resident(wb.shape),
                _resident(wo.shape), _resident(wgu.shape), _resident(wd.shape)]
    row = lambda v: v.reshape(1, d)
    return pl.pallas_call(
        functools.partial(_post_kernel, alpha=alpha, d_ff=d_ff),
        out_shape=jax.ShapeDtypeStruct((bsz, t, d), F32),
        grid=(bsz * tiles,),
        in_specs=in_specs,
        out_specs=nat(d),
        compiler_params=pltpu.CompilerParams(dimension_semantics=("arbitrary",),
                                             vmem_limit_bytes=VMEM_LIMIT_POST),
        name="post",
    )(x, oa, ob, mod3, mod3, mod3, mod3, mod3, mod3,
      row(ln1_g), row(ln1_b), row(ln2_g), row(ln2_b), wg, wa, wb, wo, wgu, wd)


def kernel(x, c, positions, w_ada, b_ada, w_in, sinks, w_branch_a, w_branch_b, w_o,
           ln1_g, ln1_b, w_gate_up, w_down, ln2_g, ln2_b):
    depth = w_ada.shape[0]
    bsz, t, d = x.shape
    alpha = (2 * depth) ** 0.25
    a_q_w = A_Q_HEADS * HEAD_DIM
    a_kv_w = A_KV_HEADS * HEAD_DIM
    b_w = B_HEADS_PER_GROUP * len(B_PATTERNS) * HEAD_DIM
    cols = {"qa": 0, "qa_w": a_q_w, "ka": a_q_w, "kva_w": a_kv_w,
            "qb": a_q_w + 2 * a_kv_w, "kb": a_q_w + 2 * a_kv_w + b_w,
            "vb": a_q_w + 2 * a_kv_w + 2 * b_w, "bg_w": B_HEADS_PER_GROUP * HEAD_DIM}
    gate0 = a_q_w + 2 * a_kv_w + 3 * b_w

    for l in range(depth):
        mod3 = _ada(c, w_ada[l], b_ada[l]).reshape(bsz, 1, 6 * d)
        w_qkv = w_in[l][:, :gate0].astype(BF16)
        qa, ka, va, b1, b2, b3 = _inproj(x, mod3, positions, w_qkv, cols)
        full = lambda w: (w, 0, w.shape[1])
        post_w = [(w_in[l], gate0, w_in.shape[2] - gate0), full(w_branch_a[l]), full(w_branch_b[l]),
                  full(w_o[l]), full(w_gate_up[l]), full(w_down[l])]
        oa, post_w_bf = _attn_a(sinks[l], qa, ka, va, post_w)
        ob = _attn_b(b1, b2, b3)
        x = _post(x, oa, ob, mod3, ln1_g[l], ln1_b[l], ln2_g[l], ln2_b[l], *post_w_bf, alpha)
    return x
```

```python
import functools

import jax
import jax.numpy as jnp
from jax import lax
from jax.experimental import pallas as pl
from jax.experimental.pallas import tpu as pltpu

F32 = jnp.float32
BF16 = jnp.bfloat16

HEAD_DIM = 64
HALF = HEAD_DIM // 2
BLOCK = 128
LANES = 128
A_Q_HEADS = 16
A_KV_HEADS = 2
A_GROUP = A_Q_HEADS // A_KV_HEADS
A_WINDOW = 128
B_PATTERNS = ((128, 1), (512, 4), (2048, 16))
B_HEADS_PER_GROUP = 8
ROPE_THETA = 10000.0
LN_EPS = 1e-5
NEG_INF = -1e30
LOG2E = 1.4426950408889634

TM_IN = 1024
SUB_IN = 512
TM_POST = 1024
SUB_POST = 512
FF_CHUNK = 256
B_PAIRS_PER_STEP = 2
A_SLABS_PER_DOT = 2
VMEM_BYTES = 64 * 1024 * 1024
VMEM_LIMIT = VMEM_BYTES - 8 * 1024 * 1024
VMEM_LIMIT_POST = VMEM_BYTES - 2 * 1024 * 1024

_NT = (((1,), (1,)), ((), ()))


def _resident(shape):
    nd = len(shape)
    return pl.BlockSpec(shape, lambda *_: (0,) * nd, pipeline_mode=pl.Buffered(1))


def _ada_kernel(c_ref, w_ref, b_ref, o_ref):
    c = c_ref[...]
    act = (c * jax.nn.sigmoid(c)).astype(BF16)
    o_ref[...] = jnp.dot(act, w_ref[...].astype(BF16), preferred_element_type=F32) + b_ref[...]


def _ada(c, w, b):
    bsz, d = c.shape
    e = w.shape[1]
    tn = 2 * d
    return pl.pallas_call(
        _ada_kernel,
        out_shape=jax.ShapeDtypeStruct((bsz, e), F32),
        grid=(e // tn,),
        in_specs=[pl.BlockSpec((bsz, d), lambda j: (0, 0)),
                  pl.BlockSpec((d, tn), lambda j: (0, j)),
                  pl.BlockSpec((1, tn), lambda j: (0, j))],
        out_specs=pl.BlockSpec((bsz, tn), lambda j: (0, j)),
        compiler_params=pltpu.CompilerParams(dimension_semantics=("arbitrary",),
                                             vmem_limit_bytes=VMEM_LIMIT),
        name="ada",
    )(c, w, b.reshape(1, e))


def _rope_cols(y, ctab, stab, first_half, scale):
    outs = []
    for j in range(y.shape[1] // LANES):
        slab = y[:, j * LANES:(j + 1) * LANES]
        swapped = jnp.where(first_half,
                            pltpu.roll(slab, LANES - HALF, axis=1),
                            pltpu.roll(slab, HALF, axis=1))
        r = slab * ctab + swapped * stab
        if scale != 1.0:
            r = r * scale
        outs.append(r.astype(BF16))
    return outs[0] if len(outs) == 1 else jnp.concatenate(outs, axis=1)


def _inproj_kernel(x_ref, sc_ref, sh_ref, pos_ref, inv_ref, w_ref,
                   qa_ref, ka_ref, va_ref, b1_ref, b2_ref, b3_ref, us_ref, cn_ref, sn_ref, *, cols):
    tm, d = x_ref.shape
    nslab = d // LANES
    sub = SUB_IN
    lane = lax.broadcasted_iota(jnp.int32, (sub, LANES), 1)
    first_half = (lane & HALF) == 0
    qscale = HEAD_DIM ** -0.5 * LOG2E
    scale1 = 1.0 + sc_ref[...]
    shift = sh_ref[...]
    half_q = cols["qa_w"] // 2
    kvw = cols["kva_w"]
    gw = cols["bg_w"]

    def proj(u, c0, width):
        return jnp.dot(u, w_ref[:, c0:c0 + width], preferred_element_type=F32)

    for base in range(0, tm, sub):
        rows = slice(base, base + sub)
        uf = x_ref[rows, :] * scale1 + shift
        un = uf.astype(BF16)
        for j in range(nslab):
            us_ref[j, rows, :] = uf[:, j * LANES:(j + 1) * LANES]
        ang = pos_ref[:, rows] * inv_ref[...]
        cos, sin = jnp.cos(ang), jnp.sin(ang)
        reps = LANES // HEAD_DIM
        cn_ref[rows, :] = jnp.concatenate([cos, cos] * reps, axis=0).T
        sn_ref[rows, :] = jnp.concatenate([-sin, sin] * reps, axis=0).T

        def strided_rows(load, r):
            per = sub // r
            return jnp.concatenate([load(pl.ds(base + rho, per, stride=r)) for rho in range(r)], axis=0)

        def u_dilated(r):
            def load(idx):
                return jnp.concatenate([us_ref[j, idx, :] for j in range(nslab)], axis=1)
            return strided_rows(load, r).astype(BF16)

        def tab_dilated(ref, r):
            return strided_rows(lambda idx: ref[idx, :], r)

        def store_nat(c0, val):
            b1_ref[rows, c0:c0 + gw] = val

        def store_res(ref):
            nres = ref.shape[0]
            per = sub // nres
            off = base // nres

            def store(c0, val):
                for rho in range(nres):
                    ref[rho, off:off + per, c0:c0 + gw] = val[rho * per:(rho + 1) * per]
            return store

        groups = [(un, store_nat)] + [(u_dilated(B_PATTERNS[g][1]), store_res(ref))
                                      for g, ref in ((1, b2_ref), (2, b3_ref))]

        def values(g):
            u, store = groups[g]
            store(2 * gw, proj(u, cols["vb"] + g * gw, gw).astype(BF16))

        def queries_keys(g, ctab, stab):
            u, store = groups[g]
            store(0, _rope_cols(proj(u, cols["qb"] + g * gw, gw), ctab, stab, first_half, qscale))
            store(gw, _rope_cols(proj(u, cols["kb"] + g * gw, gw), ctab, stab, first_half, 1.0))

        values(0)
        cn, sn = cn_ref[rows, :], sn_ref[rows, :]
        for h in range(2):
            y = proj(un, cols["qa"] + h * half_q, half_q)
            qa_ref[rows, h * half_q:(h + 1) * half_q] = _rope_cols(y, cn, sn, first_half, qscale)
        y = proj(un, cols["ka"], 2 * kvw)
        ka_ref[rows, :] = _rope_cols(y[:, :kvw], cn, sn, first_half, 1.0)
        va_ref[rows, :] = y[:, kvw:].astype(BF16)
        queries_keys(0, cn, sn)
        for g in (1, 2):
            r = B_PATTERNS[g][1]
            queries_keys(g, tab_dilated(cn_ref, r), tab_dilated(sn_ref, r))
            values(g)


def _inproj(x, mod3, positions, w_bf, cols):
    bsz, t, d = x.shape
    tm = TM_IN
    tiles = t // tm
    r4, r16 = B_PATTERNS[1][1], B_PATTERNS[2][1]

    def bq(i):
        return i // tiles, i % tiles

    nat = lambda width: pl.BlockSpec((None, tm, width), lambda i: (bq(i)[0], bq(i)[1], 0))
    res = lambda r, width: pl.BlockSpec((None, r, tm // r, width), lambda i: (bq(i)[0], 0, bq(i)[1], 0))
    in_specs = [
        nat(d),
        pl.BlockSpec((None, 1, d), lambda i: (bq(i)[0], 0, 1)),
        pl.BlockSpec((None, 1, d), lambda i: (bq(i)[0], 0, 0)),
        pl.BlockSpec((None, 1, tm), lambda i: (bq(i)[0], 0, bq(i)[1])),
        pl.BlockSpec((HALF, 1), lambda i: (0, 0)),
        _resident(w_bf.shape),
    ]
    inv = ROPE_THETA ** (-jnp.arange(HALF, dtype=F32) / HALF)
    gw3 = 3 * cols["bg_w"]
    out_shape = [jax.ShapeDtypeStruct((bsz, t, cols["qa_w"]), BF16),
                 jax.ShapeDtypeStruct((bsz, t, cols["kva_w"]), BF16),
                 jax.ShapeDtypeStruct((bsz, t, cols["kva_w"]), BF16),
                 jax.ShapeDtypeStruct((bsz, t, gw3), BF16),
                 jax.ShapeDtypeStruct((bsz, r4, t // r4, gw3), BF16),
                 jax.ShapeDtypeStruct((bsz, r16, t // r16, gw3), BF16)]
    out_specs = [nat(cols["qa_w"]), nat(cols["kva_w"]), nat(cols["kva_w"]),
                 nat(gw3), res(r4, gw3), res(r16, gw3)]
    qa, ka, va, b1, b2, b3 = pl.pallas_call(
        functools.partial(_inproj_kernel, cols=cols),
        out_shape=out_shape,
        grid=(bsz * tiles,),
        in_specs=in_specs,
        out_specs=out_specs,
        scratch_shapes=[pltpu.VMEM((d // LANES, tm, LANES), F32),
                        pltpu.VMEM((tm, LANES), F32), pltpu.VMEM((tm, LANES), F32)],
        compiler_params=pltpu.CompilerParams(dimension_semantics=("arbitrary",),
                                             vmem_limit_bytes=VMEM_LIMIT),
        name="inproj",
    )(x, mod3, mod3, positions.astype(F32).reshape(bsz, 1, t), inv.reshape(HALF, 1), w_bf)
    return qa, ka, va, b1, b2.reshape(bsz, t, gw3), b3.reshape(bsz, t, gw3)


def _band_bias(keys, n_back, prev_valid):
    qi = lax.broadcasted_iota(jnp.int32, (BLOCK, keys), 0)
    c = lax.broadcasted_iota(jnp.int32, (BLOCK, keys), 1)
    dist = qi + (keys - BLOCK) - c
    ok = (dist >= 0) & (dist <= n_back)
    if not prev_valid:
        ok = ok & (c >= keys - BLOCK)
    return jnp.where(ok, 0.0, NEG_INF).astype(F32)


def _lane_consts():
    lane = lax.broadcasted_iota(jnp.int32, (1, LANES), 1)
    mlo = (lane < HEAD_DIM).astype(BF16)
    mhi = (lane >= HEAD_DIM).astype(BF16)
    lane_lo = lax.broadcasted_iota(jnp.int32, (BLOCK, LANES), 1) < HEAD_DIM
    return mlo, mhi, lane_lo


def _stack_scores(qs, kc, mlo, mhi):
    parts = []
    for q in qs:
        parts += [q * mlo, q * mhi]
    return lax.dot_general(jnp.concatenate(parts, axis=0), kc, _NT, preferred_element_type=F32)


def _stack_softmax(s, sinks2):
    heads = [s[h * BLOCK:(h + 1) * BLOCK] for h in range(s.shape[0] // BLOCK)]
    ms = [jnp.max(h, axis=-1, keepdims=True) for h in heads]
    if sinks2 is not None:
        ms = [jnp.maximum(m, sk) for m, sk in zip(ms, sinks2)]
    p = jnp.concatenate([jnp.exp2(h - m) for h, m in zip(heads, ms)], axis=0).astype(BF16)
    return p, ms


def _stack_values(p, vc, ms, lane_lo, sinks2, normalise):
    ones = jnp.ones(vc.shape, BF16)
    r = jnp.dot(p, jnp.concatenate([vc, ones], axis=1), preferred_element_type=F32)
    outs = []
    for j in range(len(ms) // 2):
        e, o = r[2 * j * BLOCK:(2 * j + 1) * BLOCK], r[(2 * j + 1) * BLOCK:(2 * j + 2) * BLOCK]
        me, mo = ms[2 * j], ms[2 * j + 1]
        num = jnp.where(lane_lo, e[:, :LANES], o[:, :LANES])
        den = jnp.where(lane_lo, e[:, LANES:], o[:, LANES:])
        if sinks2 is not None:
            den = den + jnp.where(lane_lo, jnp.exp2(sinks2[2 * j] - me), jnp.exp2(sinks2[2 * j + 1] - mo))
        outs.append(num / den if normalise else (num, den, jnp.where(lane_lo, me, mo)))
    return outs


def _pair_block(q, kc, vc, bias, mlo, mhi, lane_lo):
    p, ms = _stack_softmax(_stack_scores([q], kc, mlo, mhi) + bias, None)
    return _stack_values(p, vc, ms, lane_lo, None, False)[0]


def _window_block(qs, kc, vc, prev_bias, tri, tri_bf, ntri_bf, mlo, mhi, lane_lo, sinks2):
    s = _stack_scores(qs, kc, mlo, mhi)
    merged = jnp.where(tri, s[:, BLOCK:], s[:, :BLOCK] + prev_bias)
    p, ms = _stack_softmax(merged, sinks2)
    p2 = jnp.concatenate([p * ntri_bf, p * tri_bf], axis=1)
    return _stack_values(p2, vc, ms, lane_lo, sinks2, True)


def _attn_a_kernel(sink_ref, q_ref, k_ref, v_ref, *rest, casts):
    nw = len(casts)
    w_refs, o_ref, wbf_refs, (kd_ref, vd_ref) = rest[:nw], rest[nw], rest[nw + 1:2 * nw + 1], rest[2 * nw + 1:]
    for w_ref, wbf_ref, (c0, width) in zip(w_refs, wbf_refs, casts):
        wbf_ref[...] = w_ref[:, c0:c0 + width].astype(BF16)
    t = q_ref.shape[0]
    nblk = t // BLOCK
    nslab = A_GROUP // 2
    assert A_WINDOW == BLOCK
    mlo, mhi, lane_lo = _lane_consts()
    lane_lo_t = lax.broadcasted_iota(jnp.int32, (t, LANES), 1) < HEAD_DIM

    for src, dup in ((k_ref, kd_ref), (v_ref, vd_ref)):
        val = src[...].astype(F32)
        swp = pltpu.roll(val, HEAD_DIM, axis=1)
        dup[0] = jnp.where(lane_lo_t, val, swp).astype(BF16)
        dup[1] = jnp.where(lane_lo_t, swp, val).astype(BF16)
    stack = 2 * A_SLABS_PER_DOT * BLOCK
    qi = lax.broadcasted_iota(jnp.int32, (stack, BLOCK), 0) & (BLOCK - 1)
    tri = lax.broadcasted_iota(jnp.int32, (stack, BLOCK), 1) <= qi
    tri_bf = tri.astype(F32).astype(BF16)
    ntri_bf = 1.0 - tri_bf

    for i in range(nblk):
        r0 = i * BLOCK
        p0 = max(i - 1, 0) * BLOCK
        prev_bias = 0.0 if i > 0 else NEG_INF
        for hk in range(A_KV_HEADS):
            kc = jnp.concatenate([kd_ref[hk, pl.ds(p0, BLOCK), :], kd_ref[hk, pl.ds(r0, BLOCK), :]], axis=0)
            vc = jnp.concatenate([vd_ref[hk, pl.ds(p0, BLOCK), :], vd_ref[hk, pl.ds(r0, BLOCK), :]], axis=0)
            for s0 in range(0, nslab, A_SLABS_PER_DOT):
                slabs = [hk * nslab + s0 + u for u in range(A_SLABS_PER_DOT)]
                cols = [slice(sl * LANES, (sl + 1) * LANES) for sl in slabs]
                sinks2 = [sink_ref[2 * sl + h] * LOG2E for sl in slabs for h in range(2)]
                outs = _window_block([q_ref[pl.ds(r0, BLOCK), c] for c in cols], kc, vc, prev_bias,
                                     tri, tri_bf, ntri_bf, mlo, mhi, lane_lo, sinks2)
                for c, out in zip(cols, outs):
                    o_ref[pl.ds(r0, BLOCK), c] = out.astype(BF16)


def _attn_a(sinks, qa, ka, va, weights):
    bsz, t, qw = qa.shape
    kw = ka.shape[2]
    seq = lambda width: pl.BlockSpec((None, t, width), lambda b: (b, 0, 0))
    chunk = lambda w, width: pl.BlockSpec((w.shape[0] // bsz, width), lambda b: (b, 0))
    outs = pl.pallas_call(
        functools.partial(_attn_a_kernel, casts=tuple((c0, width) for _, c0, width in weights)),
        out_shape=[jax.ShapeDtypeStruct((bsz, t, qw), BF16)]
                  + [jax.ShapeDtypeStruct((w.shape[0], width), BF16) for w, _, width in weights],
        grid=(bsz,),
        in_specs=[pl.BlockSpec(memory_space=pltpu.SMEM), seq(qw), seq(kw), seq(kw)]
                 + [chunk(w, w.shape[1]) for w, _, _ in weights],
        out_specs=[seq(qw)] + [chunk(w, width) for w, _, width in weights],
        scratch_shapes=[pltpu.VMEM((A_KV_HEADS, t, LANES), BF16),
                        pltpu.VMEM((A_KV_HEADS, t, LANES), BF16)],
        compiler_params=pltpu.CompilerParams(dimension_semantics=("arbitrary",),
                                             vmem_limit_bytes=VMEM_LIMIT),
        name="attn_a",
    )(sinks, qa, ka, va, *[w for w, _, _ in weights])
    return outs[0], outs[1:]


def _attn_b_kernel(q1_ref, k1_ref, v1_ref, q2_ref, k2_ref, v2_ref, q3_ref, k3_ref, v3_ref,
                   o_ref, n1_s, d1_s, m1_s, n3_s, d3_s, m3_s, fin_s, bias_ref, bias1_ref):
    t = q1_ref.shape[0]
    mlo, mhi, lane_lo = _lane_consts()
    pair = functools.partial(_pair_block, mlo=mlo, mhi=mhi, lane_lo=lane_lo)
    (w1, _), (w4, r4), (w16, r16) = B_PATTERNS
    nb4 = t // r4 // BLOCK
    rows4 = t // r4
    assert w1 == w4 // r4 == w16 // r16 and t // r16 == BLOCK and r16 % r4 == 0

    twice = lambda b: jnp.concatenate([b, b], axis=0)
    bias_ref[...] = twice(_band_bias(2 * BLOCK, w1, True))
    bias1_ref[...] = twice(_band_bias(BLOCK, w1, True))

    def head_pair(pp):
        cs = slice(pp * LANES, (pp + 1) * LANES)
        g1_s, g3_s = (n1_s, d1_s, m1_s), (n3_s, d3_s, m3_s)

        def first_block(q_ref, k_ref, v_ref, r0):
            k = k_ref[pl.ds(r0, BLOCK), cs]
            s = _stack_scores([q_ref[pl.ds(r0, BLOCK), cs]], jnp.concatenate([k, k], axis=0), mlo, mhi)
            p, ms = _stack_softmax(s[:, :BLOCK] + bias1_ref[...], None)
            return _stack_values(p, v_ref[pl.ds(r0, BLOCK), cs], ms, lane_lo, None, False)[0]

        def banded(q_ref, k_ref, v_ref, r0):
            return pair(q_ref[pl.ds(r0, BLOCK), cs], k_ref[pl.ds(r0 - BLOCK, 2 * BLOCK), cs],
                        v_ref[pl.ds(r0 - BLOCK, 2 * BLOCK), cs], bias_ref[...])


        for rho in range(r16):
            stats = first_block(q3_ref, k3_ref, v3_ref, rho * BLOCK)
            dst = pl.ds((rho % r4) * rows4 + rho // r4, BLOCK, stride=r4)
            for ref, val in zip(g3_s, stats):
                ref[pp, dst, :] = val

        for i in range(t // BLOCK):
            r0 = i * BLOCK
            stats = first_block(q1_ref, k1_ref, v1_ref, r0) if i == 0 else banded(q1_ref, k1_ref, v1_ref, r0)
            for ref, val in zip(g1_s, stats):
                ref[pp, pl.ds(r0, BLOCK), :] = val

        for idx in range(r4 * nb4):
            rho, sb = idx // nb4, idx % nb4
            r0 = idx * BLOCK
            n2, d2, m2 = first_block(q2_ref, k2_ref, v2_ref, r0) if sb == 0 else banded(q2_ref, k2_ref, v2_ref, r0)
            nat = pl.ds(sb * (BLOCK * r4) + rho, BLOCK, stride=r4)
            n1, d1, m1 = (ref[pp, nat, :] for ref in g1_s)
            n3, d3, m3 = (ref[pp, pl.ds(r0, BLOCK), :] for ref in g3_s)
            mx = jnp.maximum(jnp.maximum(m1, m2), m3)
            a1, a2, a3 = jnp.exp2(m1 - mx), jnp.exp2(m2 - mx), jnp.exp2(m3 - mx)
            fin_s[pp, nat, :] = (a1 * n1 + a2 * n2 + a3 * n3) / (a1 * d1 + a2 * d2 + a3 * d3)

        o_ref[:, cs] = fin_s[pp].astype(BF16)

    for pp in range(o_ref.shape[1] // LANES):
        head_pair(pp)


def _attn_b(b1, b2, b3):
    bsz, t, w3 = b1.shape
    gw = w3 // 3
    wblk = B_PAIRS_PER_STEP * LANES
    nblk = gw // wblk

    def spec(part):
        return pl.BlockSpec((None, t, wblk), lambda b, j: (b, 0, part * nblk + j))

    return pl.pallas_call(
        _attn_b_kernel,
        out_shape=jax.ShapeDtypeStruct((bsz, t, gw), BF16),
        grid=(bsz, nblk),
        in_specs=[spec(0), spec(1), spec(2)] * 3,
        out_specs=pl.BlockSpec((None, t, wblk), lambda b, j: (b, 0, j)),
        scratch_shapes=[pltpu.VMEM((B_PAIRS_PER_STEP, t, LANES), F32)] * 7
                       + [pltpu.VMEM((2 * BLOCK, 2 * BLOCK), F32),
                          pltpu.VMEM((2 * BLOCK, BLOCK), F32)],
        compiler_params=pltpu.CompilerParams(dimension_semantics=("arbitrary", "arbitrary"),
                                             vmem_limit_bytes=VMEM_LIMIT),
        name="attn_b",
    )(b1, b1, b1, b2, b2, b2, b3, b3, b3)


def _layer_norm(h, g, b):
    mu = jnp.mean(h, axis=-1, keepdims=True)
    xc = h - mu
    var = jnp.mean(xc * xc, axis=-1, keepdims=True)
    return xc * lax.rsqrt(var + LN_EPS) * g + b


def _post_kernel(x_ref, oa_ref, ob_ref, scm_ref, shm_ref, gm_ref, shf_ref, scf_ref, gf_ref,
                 g1_ref, b1_ref, g2_ref, b2_ref,
                 wg_ref, wa_ref, wb_ref, wo_ref, wgu_ref, wd_ref, o_ref, *, alpha, d_ff):
    d = x_ref.shape[1]

    def mix(rows):
        x = x_ref[rows, :]
        u = (x * (1.0 + scm_ref[...]) + shm_ref[...]).astype(BF16)
        oa, ob = oa_ref[rows, :], ob_ref[rows, :]
        parts = []
        for c0 in range(0, d, FF_CHUNK):
            ga = jax.nn.sigmoid(jnp.dot(u, wg_ref[:, c0:c0 + FF_CHUNK], preferred_element_type=F32))
            ya = jnp.dot(oa, wa_ref[:, c0:c0 + FF_CHUNK], preferred_element_type=F32)
            gb = jax.nn.sigmoid(jnp.dot(u, wg_ref[:, d + c0:d + c0 + FF_CHUNK], preferred_element_type=F32))
            yb = jnp.dot(ob, wb_ref[:, c0:c0 + FF_CHUNK], preferred_element_type=F32)
            parts.append((ga * ya + gb * yb).astype(BF16))
        return jnp.dot(jnp.concatenate(parts, axis=1), wo_ref[...], preferred_element_type=F32)

    def norm1(rows, y):
        return _layer_norm(alpha * x_ref[rows, :] + (1.0 + gm_ref[...]) * y, g1_ref[...], b1_ref[...])

    def ffn(x1):
        u2 = (x1 * (1.0 + scf_ref[...]) + shf_ref[...]).astype(BF16)
        acts = []
        for c in range(d_ff // FF_CHUNK):
            c0 = c * FF_CHUNK
            hg = jnp.dot(u2, wgu_ref[:, c0:c0 + FF_CHUNK], preferred_element_type=F32)
            hu = jnp.dot(u2, wgu_ref[:, d_ff + c0:d_ff + c0 + FF_CHUNK], preferred_element_type=F32)
            acts.append((hg * jax.nn.sigmoid(hg) * hu).astype(BF16))
        return jnp.dot(jnp.concatenate(acts, axis=1), wd_ref[...], preferred_element_type=F32)

    def norm2(rows, x1, acc):
        o_ref[rows, :] = _layer_norm(alpha * x1 + (1.0 + gf_ref[...]) * acc, g2_ref[...], b2_ref[...])

    subs = [slice(r0, r0 + SUB_POST) for r0 in range(0, x_ref.shape[0], SUB_POST)]
    ys = [mix(rows) for rows in subs]
    x1s, accs = [], []
    for i, rows in enumerate(subs):
        x1s.append(norm1(rows, ys[i]))
        accs.append(ffn(x1s[i]))
        if i > 0:
            norm2(subs[i - 1], x1s[i - 1], accs[i - 1])
    norm2(subs[-1], x1s[-1], accs[-1])


def _post(x, oa, ob, mod3, ln1_g, ln1_b, ln2_g, ln2_b, wg, wa, wb, wo, wgu, wd, alpha):
    bsz, t, d = x.shape
    tm = TM_POST
    tiles = t // tm
    d_ff = wd.shape[0]

    def bq(i):
        return i // tiles, i % tiles

    nat = lambda width: pl.BlockSpec((None, tm, width), lambda i: (bq(i)[0], bq(i)[1], 0))
    modspec = lambda k: pl.BlockSpec((None, 1, d), lambda i: (bq(i)[0], 0, k))
    vec = pl.BlockSpec((1, d), lambda i: (0, 0))
    in_specs = [nat(d), nat(oa.shape[2]), nat(ob.shape[2]),
                modspec(1), modspec(0), modspec(2), modspec(3), modspec(4), modspec(5),
                vec, vec, vec, vec,
                _resident(wg.shape), _resident(wa.shape), _resident(wb.shape),
                _resident(wo.shape), _resident(wgu.shape), _resident(wd.shape)]
    row = lambda v: v.reshape(1, d)
    return pl.pallas_call(
        functools.partial(_post_kernel, alpha=alpha, d_ff=d_ff),
        out_shape=jax.ShapeDtypeStruct((bsz, t, d), F32),
        grid=(bsz * tiles,),
        in_specs=in_specs,
        out_specs=nat(d),
        compiler_params=pltpu.CompilerParams(dimension_semantics=("arbitrary",),
                                             vmem_limit_bytes=VMEM_LIMIT_POST),
        name="post",
    )(x, oa, ob, mod3, mod3, mod3, mod3, mod3, mod3,
      row(ln1_g), row(ln1_b), row(ln2_g), row(ln2_b), wg, wa, wb, wo, wgu, wd)


def kernel(x, c, positions, w_ada, b_ada, w_in, sinks, w_branch_a, w_branch_b, w_o,
           ln1_g, ln1_b, w_gate_up, w_down, ln2_g, ln2_b):
    depth = w_ada.shape[0]
    bsz, t, d = x.shape
    alpha = (2 * depth) ** 0.25
    a_q_w = A_Q_HEADS * HEAD_DIM
    a_kv_w = A_KV_HEADS * HEAD_DIM
    b_w = B_HEADS_PER_GROUP * len(B_PATTERNS) * HEAD_DIM
    cols = {"qa": 0, "qa_w": a_q_w, "ka": a_q_w, "kva_w": a_kv_w,
            "qb": a_q_w + 2 * a_kv_w, "kb": a_q_w + 2 * a_kv_w + b_w,
            "vb": a_q_w + 2 * a_kv_w + 2 * b_w, "bg_w": B_HEADS_PER_GROUP * HEAD_DIM}
    gate0 = a_q_w + 2 * a_kv_w + 3 * b_w

    for l in range(depth):
        mod3 = _ada(c, w_ada[l], b_ada[l]).reshape(bsz, 1, 6 * d)
        w_qkv = w_in[l][:, :gate0].astype(BF16)
        qa, ka, va, b1, b2, b3 = _inproj(x, mod3, positions, w_qkv, cols)
        full = lambda w: (w, 0, w.shape[1])
        post_w = [(w_in[l], gate0, w_in.shape[2] - gate0), full(w_branch_a[l]), full(w_branch_b[l]),
                  full(w_o[l]), full(w_gate_up[l]), full(w_down[l])]
        oa, post_w_bf = _attn_a(sinks[l], qa, ka, va, post_w)
        ob = _attn_b(b1, b2, b3)
        x = _post(x, oa, ob, mod3, ln1_g[l], ln1_b[l], ln2_g[l], ln2_b[l], *post_w_bf, alpha)
    return x
```

```python
import functools

import jax
import jax.numpy as jnp
from jax import lax
from jax.experimental import pallas as pl
from jax.experimental.pallas import tpu as pltpu

F32 = jnp.float32
BF16 = jnp.bfloat16

HEAD_DIM = 64
HALF = HEAD_DIM // 2
BLOCK = 128
LANES = 128
A_Q_HEADS = 16
A_KV_HEADS = 2
A_GROUP = A_Q_HEADS // A_KV_HEADS
A_WINDOW = 128
B_PATTERNS = ((128, 1), (512, 4), (2048, 16))
B_HEADS_PER_GROUP = 8
ROPE_THETA = 10000.0
LN_EPS = 1e-5
NEG_INF = -1e30
LOG2E = 1.4426950408889634

TM_IN = 1024
SUB_IN = 512
TM_POST = 1024
SUB_POST = 512
FF_CHUNK = 256
B_PAIRS_PER_STEP = 2
A_SLABS_PER_DOT = 2
VMEM_BYTES = 64 * 1024 * 1024
VMEM_LIMIT = VMEM_BYTES - 8 * 1024 * 1024
VMEM_LIMIT_POST = VMEM_BYTES - 2 * 1024 * 1024

_NT = (((1,), (1,)), ((), ()))


def _resident(shape):
    nd = len(shape)
    return pl.BlockSpec(shape, lambda *_: (0,) * nd, pipeline_mode=pl.Buffered(1))


def _ada_kernel(c_ref, w_ref, b_ref, o_ref):
    c = c_ref[...]
    act = (c * jax.nn.sigmoid(c)).astype(BF16)
    o_ref[...] = jnp.dot(act, w_ref[...].astype(BF16), preferred_element_type=F32) + b_ref[...]


def _ada(c, w, b):
    bsz, d = c.shape
    e = w.shape[1]
    tn = 2 * d
    return pl.pallas_call(
        _ada_kernel,
        out_shape=jax.ShapeDtypeStruct((bsz, e), F32),
        grid=(e // tn,),
        in_specs=[pl.BlockSpec((bsz, d), lambda j: (0, 0)),
                  pl.BlockSpec((d, tn), lambda j: (0, j)),
                  pl.BlockSpec((1, tn), lambda j: (0, j))],
        out_specs=pl.BlockSpec((bsz, tn), lambda j: (0, j)),
        compiler_params=pltpu.CompilerParams(dimension_semantics=("arbitrary",),
                                             vmem_limit_bytes=VMEM_LIMIT),
        name="ada",
    )(c, w, b.reshape(1, e))


def _rope_cols(y, ctab, stab, first_half, scale, out_dtype=BF16):
    outs = []
    for j in range(y.shape[1] // LANES):
        slab = y[:, j * LANES:(j + 1) * LANES]
        swapped = jnp.where(first_half,
                            pltpu.roll(slab, LANES - HALF, axis=1),
                            pltpu.roll(slab, HALF, axis=1))
        r = slab * ctab + swapped * stab
        if scale != 1.0:
            r = r * scale
        outs.append(r.astype(out_dtype))
    return outs[0] if len(outs) == 1 else jnp.concatenate(outs, axis=1)


def _inproj_kernel(x_ref, sc_ref, sh_ref, pos_ref, inv_ref, w_ref,
                   qa_ref, ka_ref, va_ref, b1_ref, b2_ref, b3_ref, us_ref, cn_ref, sn_ref, *, cols):
    tm, d = x_ref.shape
    nslab = d // LANES
    sub = SUB_IN
    lane = lax.broadcasted_iota(jnp.int32, (sub, LANES), 1)
    first_half = (lane & HALF) == 0
    lane_lo = lane < HEAD_DIM
    qscale = HEAD_DIM ** -0.5 * LOG2E
    scale1 = 1.0 + sc_ref[...]
    shift = sh_ref[...]
    half_q = cols["qa_w"] // 2
    kvw = cols["kva_w"]
    gw = cols["bg_w"]

    def proj(u, c0, width):
        return jnp.dot(u, w_ref[:, c0:c0 + width], preferred_element_type=F32)

    for base in range(0, tm, sub):
        rows = slice(base, base + sub)
        uf = x_ref[rows, :] * scale1 + shift
        un = uf.astype(BF16)
        for j in range(nslab):
            us_ref[j, rows, :] = uf[:, j * LANES:(j + 1) * LANES]
        ang = pos_ref[:, rows] * inv_ref[...]
        cos, sin = jnp.cos(ang), jnp.sin(ang)
        reps = LANES // HEAD_DIM
        cn_ref[rows, :] = jnp.concatenate([cos, cos] * reps, axis=0).T
        sn_ref[rows, :] = jnp.concatenate([-sin, sin] * reps, axis=0).T

        def strided_rows(load, r):
            per = sub // r
            return jnp.concatenate([load(pl.ds(base + rho, per, stride=r)) for rho in range(r)], axis=0)

        def u_dilated(r):
            def load(idx):
                return jnp.concatenate([us_ref[j, idx, :] for j in range(nslab)], axis=1)
            return strided_rows(load, r).astype(BF16)

        def tab_dilated(ref, r):
            return strided_rows(lambda idx: ref[idx, :], r)

        def store_nat(c0, val):
            b1_ref[rows, c0:c0 + gw] = val

        def store_res(ref):
            nres = ref.shape[0]
            per = sub // nres
            off = base // nres

            def store(c0, val):
                for rho in range(nres):
                    ref[rho, off:off + per, c0:c0 + gw] = val[rho * per:(rho + 1) * per]
            return store

        groups = [(un, store_nat)] + [(u_dilated(B_PATTERNS[g][1]), store_res(ref))
                                      for g, ref in ((1, b2_ref), (2, b3_ref))]

        def values(g):
            u, store = groups[g]
            store(2 * gw, proj(u, cols["vb"] + g * gw, gw).astype(BF16))

        def queries_keys(g, ctab, stab):
            u, store = groups[g]
            store(0, _rope_cols(proj(u, cols["qb"] + g * gw, gw), ctab, stab, first_half, qscale))
            store(gw, _rope_cols(proj(u, cols["kb"] + g * gw, gw), ctab, stab, first_half, 1.0))

        values(0)
        cn, sn = cn_ref[rows, :], sn_ref[rows, :]
        for h in range(2):
            y = proj(un, cols["qa"] + h * half_q, half_q)
            qa_ref[rows, h * half_q:(h + 1) * half_q] = _rope_cols(y, cn, sn, first_half, qscale)
        y = proj(un, cols["ka"], 2 * kvw)
        for ref, val in ((ka_ref, _rope_cols(y[:, :kvw], cn, sn, first_half, 1.0, F32)), (va_ref, y[:, kvw:])):
            swp = pltpu.roll(val, HEAD_DIM, axis=1)
            ref[rows, :] = jnp.concatenate([jnp.where(lane_lo, val, swp), jnp.where(lane_lo, swp, val)],
                                           axis=1).astype(BF16)
        queries_keys(0, cn, sn)
        for g in (1, 2):
            r = B_PATTERNS[g][1]
            queries_keys(g, tab_dilated(cn_ref, r), tab_dilated(sn_ref, r))
            values(g)


def _inproj(x, mod3, positions, w_bf, cols):
    bsz, t, d = x.shape
    tm = TM_IN
    tiles = t // tm
    r4, r16 = B_PATTERNS[1][1], B_PATTERNS[2][1]

    def bq(i):
        return i // tiles, i % tiles

    nat = lambda width: pl.BlockSpec((None, tm, width), lambda i: (bq(i)[0], bq(i)[1], 0))
    res = lambda r, width: pl.BlockSpec((None, r, tm // r, width), lambda i: (bq(i)[0], 0, bq(i)[1], 0))
    in_specs = [
        nat(d),
        pl.BlockSpec((None, 1, d), lambda i: (bq(i)[0], 0, 1)),
        pl.BlockSpec((None, 1, d), lambda i: (bq(i)[0], 0, 0)),
        pl.BlockSpec((None, 1, tm), lambda i: (bq(i)[0], 0, bq(i)[1])),
        pl.BlockSpec((HALF, 1), lambda i: (0, 0)),
        _resident(w_bf.shape),
    ]
    inv = ROPE_THETA ** (-jnp.arange(HALF, dtype=F32) / HALF)
    gw3 = 3 * cols["bg_w"]
    kv_dup = 2 * cols["kva_w"]
    out_shape = [jax.ShapeDtypeStruct((bsz, t, cols["qa_w"]), BF16),
                 jax.ShapeDtypeStruct((bsz, t, kv_dup), BF16),
                 jax.ShapeDtypeStruct((bsz, t, kv_dup), BF16),
                 jax.ShapeDtypeStruct((bsz, t, gw3), BF16),
                 jax.ShapeDtypeStruct((bsz, r4, t // r4, gw3), BF16),
                 jax.ShapeDtypeStruct((bsz, r16, t // r16, gw3), BF16)]
    out_specs = [nat(cols["qa_w"]), nat(kv_dup), nat(kv_dup),
                 nat(gw3), res(r4, gw3), res(r16, gw3)]
    qa, ka, va, b1, b2, b3 = pl.pallas_call(
        functools.partial(_inproj_kernel, cols=cols),
        out_shape=out_shape,
        grid=(bsz * tiles,),
        in_specs=in_specs,
        out_specs=out_specs,
        scratch_shapes=[pltpu.VMEM((d // LANES, tm, LANES), F32),
                        pltpu.VMEM((tm, LANES), F32), pltpu.VMEM((tm, LANES), F32)],
        compiler_params=pltpu.CompilerParams(dimension_semantics=("arbitrary",),
                                             vmem_limit_bytes=VMEM_LIMIT),
        name="inproj",
    )(x, mod3, mod3, positions.astype(F32).reshape(bsz, 1, t), inv.reshape(HALF, 1), w_bf)
    return qa, ka, va, b1, b2.reshape(bsz, t, gw3), b3.reshape(bsz, t, gw3)


def _band_bias(keys, n_back, prev_valid):
    qi = lax.broadcasted_iota(jnp.int32, (BLOCK, keys), 0)
    c = lax.broadcasted_iota(jnp.int32, (BLOCK, keys), 1)
    dist = qi + (keys - BLOCK) - c
    ok = (dist >= 0) & (dist <= n_back)
    if not prev_valid:
        ok = ok & (c >= keys - BLOCK)
    return jnp.where(ok, 0.0, NEG_INF).astype(F32)


def _lane_consts():
    lane = lax.broadcasted_iota(jnp.int32, (1, LANES), 1)
    mlo = (lane < HEAD_DIM).astype(BF16)
    mhi = (lane >= HEAD_DIM).astype(BF16)
    lane_lo = lax.broadcasted_iota(jnp.int32, (BLOCK, LANES), 1) < HEAD_DIM
    return mlo, mhi, lane_lo


def _stack_scores(qs, kc, mlo, mhi):
    parts = []
    for q in qs:
        parts += [q * mlo, q * mhi]
    return lax.dot_general(jnp.concatenate(parts, axis=0), kc, _NT, preferred_element_type=F32)


def _stack_softmax(s, sinks2):
    heads = [s[h * BLOCK:(h + 1) * BLOCK] for h in range(s.shape[0] // BLOCK)]
    ms = [jnp.max(h, axis=-1, keepdims=True) for h in heads]
    if sinks2 is not None:
        ms = [jnp.maximum(m, sk) for m, sk in zip(ms, sinks2)]
    p = jnp.concatenate([jnp.exp2(h - m) for h, m in zip(heads, ms)], axis=0).astype(BF16)
    return p, ms


def _stack_values(p, vc, ms, lane_lo, sinks2, normalise):
    ones = jnp.ones(vc.shape, BF16)
    r = jnp.dot(p, jnp.concatenate([vc, ones], axis=1), preferred_element_type=F32)
    outs = []
    for j in range(len(ms) // 2):
        e, o = r[2 * j * BLOCK:(2 * j + 1) * BLOCK], r[(2 * j + 1) * BLOCK:(2 * j + 2) * BLOCK]
        me, mo = ms[2 * j], ms[2 * j + 1]
        num = jnp.where(lane_lo, e[:, :LANES], o[:, :LANES])
        den = jnp.where(lane_lo, e[:, LANES:], o[:, LANES:])
        if sinks2 is not None:
            den = den + jnp.where(lane_lo, jnp.exp2(sinks2[2 * j] - me), jnp.exp2(sinks2[2 * j + 1] - mo))
        outs.append(num / den if normalise else (num, den, jnp.where(lane_lo, me, mo)))
    return outs


def _pair_block(q, kc, vc, bias, mlo, mhi, lane_lo):
    p, ms = _stack_softmax(_stack_scores([q], kc, mlo, mhi) + bias, None)
    return _stack_values(p, vc, ms, lane_lo, None, False)[0]


def _window_block(qs, kc, vc, prev_bias, tri, tri_bf, ntri_bf, mlo, mhi, lane_lo, sinks2):
    s = _stack_scores(qs, kc, mlo, mhi)
    merged = jnp.where(tri, s[:, BLOCK:], s[:, :BLOCK] + prev_bias)
    p, ms = _stack_softmax(merged, sinks2)
    p2 = jnp.concatenate([p * ntri_bf, p * tri_bf], axis=1)
    return _stack_values(p2, vc, ms, lane_lo, sinks2, True)


def _attn_a_kernel(sink_ref, q_ref, k_ref, v_ref, *rest, casts):
    nw = len(casts)
    w_refs, o_ref, wbf_refs = rest[:nw], rest[nw], rest[nw + 1:]
    for w_ref, wbf_ref, (c0, width) in zip(w_refs, wbf_refs, casts):
        wbf_ref[...] = w_ref[:, c0:c0 + width].astype(BF16)
    t = q_ref.shape[0]
    nblk = t // BLOCK
    nslab = A_GROUP // 2
    assert A_WINDOW == BLOCK
    mlo, mhi, lane_lo = _lane_consts()
    stack = 2 * A_SLABS_PER_DOT * BLOCK
    qi = lax.broadcasted_iota(jnp.int32, (stack, BLOCK), 0) & (BLOCK - 1)
    tri = lax.broadcasted_iota(jnp.int32, (stack, BLOCK), 1) <= qi
    tri_bf = tri.astype(F32).astype(BF16)
    ntri_bf = 1.0 - tri_bf

    for i in range(nblk):
        r0 = i * BLOCK
        p0 = max(i - 1, 0) * BLOCK
        prev_bias = 0.0 if i > 0 else NEG_INF
        for hk in range(A_KV_HEADS):
            hc = slice(hk * LANES, (hk + 1) * LANES)
            kc = jnp.concatenate([k_ref[pl.ds(p0, BLOCK), hc], k_ref[pl.ds(r0, BLOCK), hc]], axis=0)
            vc = jnp.concatenate([v_ref[pl.ds(p0, BLOCK), hc], v_ref[pl.ds(r0, BLOCK), hc]], axis=0)
            for s0 in range(0, nslab, A_SLABS_PER_DOT):
                slabs = [hk * nslab + s0 + u for u in range(A_SLABS_PER_DOT)]
                cols = [slice(sl * LANES, (sl + 1) * LANES) for sl in slabs]
                sinks2 = [sink_ref[2 * sl + h] * LOG2E for sl in slabs for h in range(2)]
                outs = _window_block([q_ref[pl.ds(r0, BLOCK), c] for c in cols], kc, vc, prev_bias,
                                     tri, tri_bf, ntri_bf, mlo, mhi, lane_lo, sinks2)
                for c, out in zip(cols, outs):
                    o_ref[pl.ds(r0, BLOCK), c] = out.astype(BF16)


def _attn_a(sinks, qa, ka, va, weights):
    bsz, t, qw = qa.shape
    kw = ka.shape[2]
    seq = lambda width: pl.BlockSpec((None, t, width), lambda b: (b, 0, 0))
    chunk = lambda w, width: pl.BlockSpec((w.shape[0] // bsz, width), lambda b: (b, 0))
    outs = pl.pallas_call(
        functools.partial(_attn_a_kernel, casts=tuple((c0, width) for _, c0, width in weights)),
        out_shape=[jax.ShapeDtypeStruct((bsz, t, qw), BF16)]
                  + [jax.ShapeDtypeStruct((w.shape[0], width), BF16) for w, _, width in weights],
        grid=(bsz,),
        in_specs=[pl.BlockSpec(memory_space=pltpu.SMEM), seq(qw), seq(kw), seq(kw)]
                 + [chunk(w, w.shape[1]) for w, _, _ in weights],
        out_specs=[seq(qw)] + [chunk(w, width) for w, _, width in weights],
        compiler_params=pltpu.CompilerParams(dimension_semantics=("arbitrary",),
                                             vmem_limit_bytes=VMEM_LIMIT),
        name="attn_a",
    )(sinks, qa, ka, va, *[w for w, _, _ in weights])
    return outs[0], outs[1:]


def _attn_b_kernel(q1_ref, k1_ref, v1_ref, q2_ref, k2_ref, v2_ref, q3_ref, k3_ref, v3_ref,
                   o_ref, n1_s, d1_s, m1_s, n3_s, d3_s, m3_s, fin_s, bias_ref, bias1_ref):
    t = q1_ref.shape[0]
    mlo, mhi, lane_lo = _lane_consts()
    pair = functools.partial(_pair_block, mlo=mlo, mhi=mhi, lane_lo=lane_lo)
    (w1, _), (w4, r4), (w16, r16) = B_PATTERNS
    nb4 = t // r4 // BLOCK
    rows4 = t // r4
    assert w1 == w4 // r4 == w16 // r16 and t // r16 == BLOCK and r16 % r4 == 0

    twice = lambda b: jnp.concatenate([b, b], axis=0)
    bias_ref[...] = twice(_band_bias(2 * BLOCK, w1, True))
    bias1_ref[...] = twice(_band_bias(BLOCK, w1, True))

    def head_pair(pp):
        cs = slice(pp * LANES, (pp + 1) * LANES)
        g1_s, g3_s = (n1_s, d1_s, m1_s), (n3_s, d3_s, m3_s)

        def first_block(q_ref, k_ref, v_ref, r0):
            k = k_ref[pl.ds(r0, BLOCK), cs]
            s = _stack_scores([q_ref[pl.ds(r0, BLOCK), cs]], jnp.concatenate([k, k], axis=0), mlo, mhi)
            p, ms = _stack_softmax(s[:, :BLOCK] + bias1_ref[...], None)
            return _stack_values(p, v_ref[pl.ds(r0, BLOCK), cs], ms, lane_lo, None, False)[0]

        def banded(q_ref, k_ref, v_ref, r0):
            return pair(q_ref[pl.ds(r0, BLOCK), cs], k_ref[pl.ds(r0 - BLOCK, 2 * BLOCK), cs],
                        v_ref[pl.ds(r0 - BLOCK, 2 * BLOCK), cs], bias_ref[...])


        for rho in range(r16):
            stats = first_block(q3_ref, k3_ref, v3_ref, rho * BLOCK)
            dst = pl.ds((rho % r4) * rows4 + rho // r4, BLOCK, stride=r4)
            for ref, val in zip(g3_s, stats):
                ref[pp, dst, :] = val

        for i in range(t // BLOCK):
            r0 = i * BLOCK
            stats = first_block(q1_ref, k1_ref, v1_ref, r0) if i == 0 else banded(q1_ref, k1_ref, v1_ref, r0)
            for ref, val in zip(g1_s, stats):
                ref[pp, pl.ds(r0, BLOCK), :] = val

        for idx in range(r4 * nb4):
            rho, sb = idx // nb4, idx % nb4
            r0 = idx * BLOCK
            n2, d2, m2 = first_block(q2_ref, k2_ref, v2_ref, r0) if sb == 0 else banded(q2_ref, k2_ref, v2_ref, r0)
            nat = pl.ds(sb * (BLOCK * r4) + rho, BLOCK, stride=r4)
            n1, d1, m1 = (ref[pp, nat, :] for ref in g1_s)
            n3, d3, m3 = (ref[pp, pl.ds(r0, BLOCK), :] for ref in g3_s)
            mx = jnp.maximum(jnp.maximum(m1, m2), m3)
            a1, a2, a3 = jnp.exp2(m1 - mx), jnp.exp2(m2 - mx), jnp.exp2(m3 - mx)
            fin_s[pp, nat, :] = (a1 * n1 + a2 * n2 + a3 * n3) / (a1 * d1 + a2 * d2 + a3 * d3)

        o_ref[:, cs] = fin_s[pp].astype(BF16)

    for pp in range(o_ref.shape[1] // LANES):
        head_pair(pp)


def _attn_b(b1, b2, b3):
    bsz, t, w3 = b1.shape
    gw = w3 // 3
    wblk = B_PAIRS_PER_STEP * LANES
    nblk = gw // wblk

    def spec(part):
        return pl.BlockSpec((None, t, wblk), lambda b, j: (b, 0, part * nblk + j))

    return pl.pallas_call(
        _attn_b_kernel,
        out_shape=jax.ShapeDtypeStruct((bsz, t, gw), BF16),
        grid=(bsz, nblk),
        in_specs=[spec(0), spec(1), spec(2)] * 3,
        out_specs=pl.BlockSpec((None, t, wblk), lambda b, j: (b, 0, j)),
        scratch_shapes=[pltpu.VMEM((B_PAIRS_PER_STEP, t, LANES), F32)] * 7
                       + [pltpu.VMEM((2 * BLOCK, 2 * BLOCK), F32),
                          pltpu.VMEM((2 * BLOCK, BLOCK), F32)],
        compiler_params=pltpu.CompilerParams(dimension_semantics=("arbitrary", "arbitrary"),
                                             vmem_limit_bytes=VMEM_LIMIT),
        name="attn_b",
    )(b1, b1, b1, b2, b2, b2, b3, b3, b3)


def _layer_norm(h, g, b):
    mu = jnp.mean(h, axis=-1, keepdims=True)
    xc = h - mu
    var = jnp.mean(xc * xc, axis=-1, keepdims=True)
    return xc * lax.rsqrt(var + LN_EPS) * g + b


def _post_kernel(x_ref, oa_ref, ob_ref, scm_ref, shm_ref, gm_ref, shf_ref, scf_ref, gf_ref,
                 g1_ref, b1_ref, g2_ref, b2_ref,
                 wg_ref, wa_ref, wb_ref, wo_ref, wgu_ref, wd_ref, o_ref, *, alpha, d_ff):
    d = x_ref.shape[1]

    def mix(rows):
        x = x_ref[rows, :]
        u = (x * (1.0 + scm_ref[...]) + shm_ref[...]).astype(BF16)
        oa, ob = oa_ref[rows, :], ob_ref[rows, :]
        parts = []
        for c0 in range(0, d, FF_CHUNK):
            ga = jax.nn.sigmoid(jnp.dot(u, wg_ref[:, c0:c0 + FF_CHUNK], preferred_element_type=F32))
            ya = jnp.dot(oa, wa_ref[:, c0:c0 + FF_CHUNK], preferred_element_type=F32)
            gb = jax.nn.sigmoid(jnp.dot(u, wg_ref[:, d + c0:d + c0 + FF_CHUNK], preferred_element_type=F32))
            yb = jnp.dot(ob, wb_ref[:, c0:c0 + FF_CHUNK], preferred_element_type=F32)
            parts.append((ga * ya + gb * yb).astype(BF16))
        return jnp.dot(jnp.concatenate(parts, axis=1), wo_ref[...], preferred_element_type=F32)

    def norm1(rows, y):
        return _layer_norm(alpha * x_ref[rows, :] + (1.0 + gm_ref[...]) * y, g1_ref[...], b1_ref[...])

    def ffn(x1):
        u2 = (x1 * (1.0 + scf_ref[...]) + shf_ref[...]).astype(BF16)
        acts = []
        for c in range(d_ff // FF_CHUNK):
            c0 = c * FF_CHUNK
            hg = jnp.dot(u2, wgu_ref[:, c0:c0 + FF_CHUNK], preferred_element_type=F32)
            hu = jnp.dot(u2, wgu_ref[:, d_ff + c0:d_ff + c0 + FF_CHUNK], preferred_element_type=F32)
            acts.append((hg * jax.nn.sigmoid(hg) * hu).astype(BF16))
        return jnp.dot(jnp.concatenate(acts, axis=1), wd_ref[...], preferred_element_type=F32)

    def norm2(rows, x1, acc):
        o_ref[rows, :] = _layer_norm(alpha * x1 + (1.0 + gf_ref[...]) * acc, g2_ref[...], b2_ref[...])

    subs = [slice(r0, r0 + SUB_POST) for r0 in range(0, x_ref.shape[0], SUB_POST)]
    ys = [mix(rows) for rows in subs]
    x1s, accs = [], []
    for i, rows in enumerate(subs):
        x1s.append(norm1(rows, ys[i]))
        accs.append(ffn(x1s[i]))
        if i > 0:
            norm2(subs[i - 1], x1s[i - 1], accs[i - 1])
    norm2(subs[-1], x1s[-1], accs[-1])


def _post(x, oa, ob, mod3, ln1_g, ln1_b, ln2_g, ln2_b, wg, wa, wb, wo, wgu, wd, alpha):
    bsz, t, d = x.shape
    tm = TM_POST
    tiles = t // tm
    d_ff = wd.shape[0]

    def bq(i):
        return i // tiles, i % tiles

    nat = lambda width: pl.BlockSpec((None, tm, width), lambda i: (bq(i)[0], bq(i)[1], 0))
    modspec = lambda k: pl.BlockSpec((None, 1, d), lambda i: (bq(i)[0], 0, k))
    vec = pl.BlockSpec((1, d), lambda i: (0, 0))
    in_specs = [nat(d), nat(oa.shape[2]), nat(ob.shape[2]),
                modspec(1), modspec(0), modspec(2), modspec(3), modspec(4), modspec(5),
                vec, vec, vec, vec,
                _resident(wg.shape), _resident(wa.shape), _resident(wb.shape),
                _resident(wo.shape), _resident(wgu.shape), _resident(wd.shape)]
    row = lambda v: v.reshape(1, d)
    return pl.pallas_call(
        functools.partial(_post_kernel, alpha=alpha, d_ff=d_ff),
        out_shape=jax.ShapeDtypeStruct((bsz, t, d), F32),
        grid=(bsz * tiles,),
        in_specs=in_specs,
        out_specs=nat(d),
        compiler_params=pltpu.CompilerParams(dimension_semantics=("arbitrary",),
                                             vmem_limit_bytes=VMEM_LIMIT_POST),
        name="post",
    )(x, oa, ob, mod3, mod3, mod3, mod3, mod3, mod3,
      row(ln1_g), row(ln1_b), row(ln2_g), row(ln2_b), wg, wa, wb, wo, wgu, wd)


def kernel(x, c, positions, w_ada, b_ada, w_in, sinks, w_branch_a, w_branch_b, w_o,
           ln1_g, ln1_b, w_gate_up, w_down, ln2_g, ln2_b):
    depth = w_ada.shape[0]
    bsz, t, d = x.shape
    alpha = (2 * depth) ** 0.25
    a_q_w = A_Q_HEADS * HEAD_DIM
    a_kv_w = A_KV_HEADS * HEAD_DIM
    b_w = B_HEADS_PER_GROUP * len(B_PATTERNS) * HEAD_DIM
    cols = {"qa": 0, "qa_w": a_q_w, "ka": a_q_w, "kva_w": a_kv_w,
            "qb": a_q_w + 2 * a_kv_w, "kb": a_q_w + 2 * a_kv_w + b_w,
            "vb": a_q_w + 2 * a_kv_w + 2 * b_w, "bg_w": B_HEADS_PER_GROUP * HEAD_DIM}
    gate0 = a_q_w + 2 * a_kv_w + 3 * b_w

    for l in range(depth):
        mod3 = _ada(c, w_ada[l], b_ada[l]).reshape(bsz, 1, 6 * d)
        w_qkv = w_in[l][:, :gate0].astype(BF16)
        qa, ka, va, b1, b2, b3 = _inproj(x, mod3, positions, w_qkv, cols)
        full = lambda w: (w, 0, w.shape[1])
        post_w = [(w_in[l], gate0, w_in.shape[2] - gate0), full(w_branch_a[l]), full(w_branch_b[l]),
                  full(w_o[l]), full(w_gate_up[l]), full(w_down[l])]
        oa, post_w_bf = _attn_a(sinks[l], qa, ka, va, post_w)
        ob = _attn_b(b1, b2, b3)
        x = _post(x, oa, ob, mod3, ln1_g[l], ln1_b[l], ln2_g[l], ln2_b[l], *post_w_bf, alpha)
    return x
```

```python
import functools

import jax
import jax.numpy as jnp
from jax import lax
from jax.experimental import pallas as pl
from jax.experimental.pallas import tpu as pltpu

F32 = jnp.float32
BF16 = jnp.bfloat16

HEAD_DIM = 64
HALF = HEAD_DIM // 2
BLOCK = 128
LANES = 128
A_Q_HEADS = 16
A_KV_HEADS = 2
A_GROUP = A_Q_HEADS // A_KV_HEADS
A_WINDOW = 128
B_PATTERNS = ((128, 1), (512, 4), (2048, 16))
B_HEADS_PER_GROUP = 8
ROPE_THETA = 10000.0
LN_EPS = 1e-5
NEG_INF = -1e30
LOG2E = 1.4426950408889634

TM_IN = 1024
SUB_IN = 512
TM_POST = 1024
SUB_POST = 512
FF_CHUNK = 256
ADA_STEPS = 8
B_PAIRS_PER_STEP = 2
A_SLABS_PER_DOT = 2
VMEM_BYTES = 64 * 1024 * 1024
VMEM_LIMIT = VMEM_BYTES - 8 * 1024 * 1024
VMEM_LIMIT_POST = VMEM_BYTES - 2 * 1024 * 1024

_NT = (((1,), (1,)), ((), ()))


def _resident(shape):
    nd = len(shape)
    return pl.BlockSpec(shape, lambda *_: (0,) * nd, pipeline_mode=pl.Buffered(1))


def _ada_kernel(c_ref, w_ref, b_ref, wqkv_ref, o_ref, wqkv_bf_ref):
    c = c_ref[...]
    act = (c * jax.nn.sigmoid(c)).astype(BF16)
    o_ref[...] = jnp.dot(act, w_ref[...].astype(BF16), preferred_element_type=F32) + b_ref[...]
    wqkv_bf_ref[...] = wqkv_ref[...].astype(BF16)


def _ada(c, w, b, w_in, qkv_cols):
    bsz, d = c.shape
    e = w.shape[1]
    steps = ADA_STEPS
    tn = e // steps
    rows = w_in.shape[0] // steps
    return pl.pallas_call(
        _ada_kernel,
        out_shape=[jax.ShapeDtypeStruct((bsz, e), F32),
                   jax.ShapeDtypeStruct((w_in.shape[0], qkv_cols), BF16)],
        grid=(steps,),
        in_specs=[pl.BlockSpec((bsz, d), lambda j: (0, 0)),
                  pl.BlockSpec((d, tn), lambda j: (0, j)),
                  pl.BlockSpec((1, tn), lambda j: (0, j)),
                  pl.BlockSpec((rows, qkv_cols), lambda j: (j, 0))],
        out_specs=[pl.BlockSpec((bsz, tn), lambda j: (0, j)),
                   pl.BlockSpec((rows, qkv_cols), lambda j: (j, 0))],
        compiler_params=pltpu.CompilerParams(dimension_semantics=("arbitrary",),
                                             vmem_limit_bytes=VMEM_LIMIT),
        name="ada",
    )(c, w, b.reshape(1, e), w_in)


def _rope_cols(y, ctab, stab, first_half, scale, out_dtype=BF16):
    outs = []
    for j in range(y.shape[1] // LANES):
        slab = y[:, j * LANES:(j + 1) * LANES]
        swapped = jnp.where(first_half,
                            pltpu.roll(slab, LANES - HALF, axis=1),
                            pltpu.roll(slab, HALF, axis=1))
        r = slab * ctab + swapped * stab
        if scale != 1.0:
            r = r * scale
        outs.append(r.astype(out_dtype))
    return outs[0] if len(outs) == 1 else jnp.concatenate(outs, axis=1)


def _inproj_kernel(x_ref, sc_ref, sh_ref, pos_ref, inv_ref, w_ref,
                   qa_ref, ka_ref, va_ref, b1_ref, b2_ref, b3_ref, us_ref, cn_ref, sn_ref, *, cols):
    tm, d = x_ref.shape
    nslab = d // LANES
    sub = SUB_IN
    lane = lax.broadcasted_iota(jnp.int32, (sub, LANES), 1)
    first_half = (lane & HALF) == 0
    lane_lo = lane < HEAD_DIM
    qscale = HEAD_DIM ** -0.5 * LOG2E
    scale1 = 1.0 + sc_ref[...]
    shift = sh_ref[...]
    half_q = cols["qa_w"] // 2
    kvw = cols["kva_w"]
    gw = cols["bg_w"]

    def proj(u, c0, width):
        return jnp.dot(u, w_ref[:, c0:c0 + width], preferred_element_type=F32)

    for base in range(0, tm, sub):
        rows = slice(base, base + sub)
        uf = x_ref[rows, :] * scale1 + shift
        un = uf.astype(BF16)
        for j in range(nslab):
            us_ref[j, rows, :] = uf[:, j * LANES:(j + 1) * LANES]
        ang = pos_ref[:, rows] * inv_ref[...]
        cos, sin = jnp.cos(ang), jnp.sin(ang)
        reps = LANES // HEAD_DIM
        cn_ref[rows, :] = jnp.concatenate([cos, cos] * reps, axis=0).T
        sn_ref[rows, :] = jnp.concatenate([-sin, sin] * reps, axis=0).T

        def strided_rows(load, r):
            per = sub // r
            return jnp.concatenate([load(pl.ds(base + rho, per, stride=r)) for rho in range(r)], axis=0)

        def u_dilated(r):
            def load(idx):
                return jnp.concatenate([us_ref[j, idx, :] for j in range(nslab)], axis=1)
            return strided_rows(load, r).astype(BF16)

        def tab_dilated(ref, r):
            return strided_rows(lambda idx: ref[idx, :], r)

        def store_nat(c0, val):
            b1_ref[rows, c0:c0 + gw] = val

        def store_res(ref):
            nres = ref.shape[0]
            per = sub // nres
            off = base // nres

            def store(c0, val):
                for rho in range(nres):
                    ref[rho, off:off + per, c0:c0 + gw] = val[rho * per:(rho + 1) * per]
            return store

        groups = [(un, store_nat)] + [(u_dilated(B_PATTERNS[g][1]), store_res(ref))
                                      for g, ref in ((1, b2_ref), (2, b3_ref))]

        def values(g):
            u, store = groups[g]
            store(2 * gw, proj(u, cols["vb"] + g * gw, gw).astype(BF16))

        def queries_keys(g, ctab, stab):
            u, store = groups[g]
            store(0, _rope_cols(proj(u, cols["qb"] + g * gw, gw), ctab, stab, first_half, qscale))
            store(gw, _rope_cols(proj(u, cols["kb"] + g * gw, gw), ctab, stab, first_half, 1.0))

        values(0)
        cn, sn = cn_ref[rows, :], sn_ref[rows, :]
        for h in range(2):
            y = proj(un, cols["qa"] + h * half_q, half_q)
            qa_ref[rows, h * half_q:(h + 1) * half_q] = _rope_cols(y, cn, sn, first_half, qscale)
        y = proj(un, cols["ka"], 2 * kvw)
        for ref, val in ((ka_ref, _rope_cols(y[:, :kvw], cn, sn, first_half, 1.0, F32)), (va_ref, y[:, kvw:])):
            swp = pltpu.roll(val, HEAD_DIM, axis=1)
            ref[rows, :] = jnp.concatenate([jnp.where(lane_lo, val, swp), jnp.where(lane_lo, swp, val)],
                                           axis=1).astype(BF16)
        queries_keys(0, cn, sn)
        for g in (1, 2):
            r = B_PATTERNS[g][1]
            queries_keys(g, tab_dilated(cn_ref, r), tab_dilated(sn_ref, r))
            values(g)


def _inproj(x, mod3, positions, w_bf, cols):
    bsz, t, d = x.shape
    tm = TM_IN
    tiles = t // tm
    r4, r16 = B_PATTERNS[1][1], B_PATTERNS[2][1]

    def bq(i):
        return i // tiles, i % tiles

    nat = lambda width: pl.BlockSpec((None, tm, width), lambda i: (bq(i)[0], bq(i)[1], 0))
    res = lambda r, width: pl.BlockSpec((None, r, tm // r, width), lambda i: (bq(i)[0], 0, bq(i)[1], 0))
    in_specs = [
        nat(d),
        pl.BlockSpec((None, 1, d), lambda i: (bq(i)[0], 0, 1)),
        pl.BlockSpec((None, 1, d), lambda i: (bq(i)[0], 0, 0)),
        pl.BlockSpec((None, 1, tm), lambda i: (bq(i)[0], 0, bq(i)[1])),
        pl.BlockSpec((HALF, 1), lambda i: (0, 0)),
        _resident(w_bf.shape),
    ]
    inv = ROPE_THETA ** (-jnp.arange(HALF, dtype=F32) / HALF)
    gw3 = 3 * cols["bg_w"]
    kv_dup = 2 * cols["kva_w"]
    out_shape = [jax.ShapeDtypeStruct((bsz, t, cols["qa_w"]), BF16),
                 jax.ShapeDtypeStruct((bsz, t, kv_dup), BF16),
                 jax.ShapeDtypeStruct((bsz, t, kv_dup), BF16),
                 jax.ShapeDtypeStruct((bsz, t, gw3), BF16),
                 jax.ShapeDtypeStruct((bsz, r4, t // r4, gw3), BF16),
                 jax.ShapeDtypeStruct((bsz, r16, t // r16, gw3), BF16)]
    out_specs = [nat(cols["qa_w"]), nat(kv_dup), nat(kv_dup),
                 nat(gw3), res(r4, gw3), res(r16, gw3)]
    qa, ka, va, b1, b2, b3 = pl.pallas_call(
        functools.partial(_inproj_kernel, cols=cols),
        out_shape=out_shape,
        grid=(bsz * tiles,),
        in_specs=in_specs,
        out_specs=out_specs,
        scratch_shapes=[pltpu.VMEM((d // LANES, tm, LANES), F32),
                        pltpu.VMEM((tm, LANES), F32), pltpu.VMEM((tm, LANES), F32)],
        compiler_params=pltpu.CompilerParams(dimension_semantics=("arbitrary",),
                                             vmem_limit_bytes=VMEM_LIMIT),
        name="inproj",
    )(x, mod3, mod3, positions.astype(F32).reshape(bsz, 1, t), inv.reshape(HALF, 1), w_bf)
    return qa, ka, va, b1, b2.reshape(bsz, t, gw3), b3.reshape(bsz, t, gw3)


def _band_bias(keys, n_back, prev_valid):
    qi = lax.broadcasted_iota(jnp.int32, (BLOCK, keys), 0)
    c = lax.broadcasted_iota(jnp.int32, (BLOCK, keys), 1)
    dist = qi + (keys - BLOCK) - c
    ok = (dist >= 0) & (dist <= n_back)
    if not prev_valid:
        ok = ok & (c >= keys - BLOCK)
    return jnp.where(ok, 0.0, NEG_INF).astype(F32)


def _lane_consts():
    lane = lax.broadcasted_iota(jnp.int32, (1, LANES), 1)
    mlo = (lane < HEAD_DIM).astype(BF16)
    mhi = (lane >= HEAD_DIM).astype(BF16)
    lane_lo = lax.broadcasted_iota(jnp.int32, (BLOCK, LANES), 1) < HEAD_DIM
    return mlo, mhi, lane_lo


def _stack_scores(qs, kc, mlo, mhi):
    parts = []
    for q in qs:
        parts += [q * mlo, q * mhi]
    return lax.dot_general(jnp.concatenate(parts, axis=0), kc, _NT, preferred_element_type=F32)


def _stack_softmax(s, sinks2):
    heads = [s[h * BLOCK:(h + 1) * BLOCK] for h in range(s.shape[0] // BLOCK)]
    ms = [jnp.max(h, axis=-1, keepdims=True) for h in heads]
    if sinks2 is not None:
        ms = [jnp.maximum(m, sk) for m, sk in zip(ms, sinks2)]
    p = jnp.concatenate([jnp.exp2(h - m) for h, m in zip(heads, ms)], axis=0).astype(BF16)
    return p, ms


def _stack_values(p, vc, ms, lane_lo, sinks2, normalise):
    ones = jnp.ones(vc.shape, BF16)
    r = jnp.dot(p, jnp.concatenate([vc, ones], axis=1), preferred_element_type=F32)
    outs = []
    for j in range(len(ms) // 2):
        e, o = r[2 * j * BLOCK:(2 * j + 1) * BLOCK], r[(2 * j + 1) * BLOCK:(2 * j + 2) * BLOCK]
        me, mo = ms[2 * j], ms[2 * j + 1]
        num = jnp.where(lane_lo, e[:, :LANES], o[:, :LANES])
        den = jnp.where(lane_lo, e[:, LANES:], o[:, LANES:])
        if sinks2 is not None:
            den = den + jnp.where(lane_lo, jnp.exp2(sinks2[2 * j] - me), jnp.exp2(sinks2[2 * j + 1] - mo))
        outs.append(num / den if normalise else (num, den, jnp.where(lane_lo, me, mo)))
    return outs


def _pair_block(q, kc, vc, bias, mlo, mhi, lane_lo):
    p, ms = _stack_softmax(_stack_scores([q], kc, mlo, mhi) + bias, None)
    return _stack_values(p, vc, ms, lane_lo, None, False)[0]


def _window_block(qs, kc, vc, prev_bias, tri, tri_bf, ntri_bf, mlo, mhi, lane_lo, sinks2):
    s = _stack_scores(qs, kc, mlo, mhi)
    merged = jnp.where(tri, s[:, BLOCK:], s[:, :BLOCK] + prev_bias)
    p, ms = _stack_softmax(merged, sinks2)
    p2 = jnp.concatenate([p * ntri_bf, p * tri_bf], axis=1)
    return _stack_values(p2, vc, ms, lane_lo, sinks2, True)


def _attn_a_kernel(sink_ref, q_ref, k_ref, v_ref, *rest, casts):
    nw = len(casts)
    w_refs, o_ref, wbf_refs = rest[:nw], rest[nw], rest[nw + 1:]
    for w_ref, wbf_ref, (c0, width) in zip(w_refs, wbf_refs, casts):
        wbf_ref[...] = w_ref[:, c0:c0 + width].astype(BF16)
    t = q_ref.shape[0]
    nblk = t // BLOCK
    nslab = A_GROUP // 2
    assert A_WINDOW == BLOCK
    mlo, mhi, lane_lo = _lane_consts()
    stack = 2 * A_SLABS_PER_DOT * BLOCK
    qi = lax.broadcasted_iota(jnp.int32, (stack, BLOCK), 0) & (BLOCK - 1)
    tri = lax.broadcasted_iota(jnp.int32, (stack, BLOCK), 1) <= qi
    tri_bf = tri.astype(F32).astype(BF16)
    ntri_bf = 1.0 - tri_bf

    for i in range(nblk):
        r0 = i * BLOCK
        p0 = max(i - 1, 0) * BLOCK
        prev_bias = 0.0 if i > 0 else NEG_INF
        for hk in range(A_KV_HEADS):
            hc = slice(hk * LANES, (hk + 1) * LANES)
            kc = jnp.concatenate([k_ref[pl.ds(p0, BLOCK), hc], k_ref[pl.ds(r0, BLOCK), hc]], axis=0)
            vc = jnp.concatenate([v_ref[pl.ds(p0, BLOCK), hc], v_ref[pl.ds(r0, BLOCK), hc]], axis=0)
            for s0 in range(0, nslab, A_SLABS_PER_DOT):
                slabs = [hk * nslab + s0 + u for u in range(A_SLABS_PER_DOT)]
                cols = [slice(sl * LANES, (sl + 1) * LANES) for sl in slabs]
                sinks2 = [sink_ref[2 * sl + h] * LOG2E for sl in slabs for h in range(2)]
                outs = _window_block([q_ref[pl.ds(r0, BLOCK), c] for c in cols], kc, vc, prev_bias,
                                     tri, tri_bf, ntri_bf, mlo, mhi, lane_lo, sinks2)
                for c, out in zip(cols, outs):
                    o_ref[pl.ds(r0, BLOCK), c] = out.astype(BF16)


def _attn_a(sinks, qa, ka, va, weights):
    bsz, t, qw = qa.shape
    kw = ka.shape[2]
    seq = lambda width: pl.BlockSpec((None, t, width), lambda b: (b, 0, 0))
    chunk = lambda w, width: pl.BlockSpec((w.shape[0] // bsz, width), lambda b: (b, 0))
    outs = pl.pallas_call(
        functools.partial(_attn_a_kernel, casts=tuple((c0, width) for _, c0, width in weights)),
        out_shape=[jax.ShapeDtypeStruct((bsz, t, qw), BF16)]
                  + [jax.ShapeDtypeStruct((w.shape[0], width), BF16) for w, _, width in weights],
        grid=(bsz,),
        in_specs=[pl.BlockSpec(memory_space=pltpu.SMEM), seq(qw), seq(kw), seq(kw)]
                 + [chunk(w, w.shape[1]) for w, _, _ in weights],
        out_specs=[seq(qw)] + [chunk(w, width) for w, _, width in weights],
        compiler_params=pltpu.CompilerParams(dimension_semantics=("arbitrary",),
                                             vmem_limit_bytes=VMEM_LIMIT),
        name="attn_a",
    )(sinks, qa, ka, va, *[w for w, _, _ in weights])
    return outs[0], outs[1:]


def _attn_b_kernel(q1_ref, k1_ref, v1_ref, q2_ref, k2_ref, v2_ref, q3_ref, k3_ref, v3_ref,
                   o_ref, n1_s, d1_s, m1_s, n3_s, d3_s, m3_s, fin_s, bias_ref, bias1_ref):
    t = q1_ref.shape[0]
    mlo, mhi, lane_lo = _lane_consts()
    pair = functools.partial(_pair_block, mlo=mlo, mhi=mhi, lane_lo=lane_lo)
    (w1, _), (w4, r4), (w16, r16) = B_PATTERNS
    nb4 = t // r4 // BLOCK
    rows4 = t // r4
    assert w1 == w4 // r4 == w16 // r16 and t // r16 == BLOCK and r16 % r4 == 0

    twice = lambda b: jnp.concatenate([b, b], axis=0)
    bias_ref[...] = twice(_band_bias(2 * BLOCK, w1, True))
    bias1_ref[...] = twice(_band_bias(BLOCK, w1, True))

    def head_pair(pp):
        cs = slice(pp * LANES, (pp + 1) * LANES)
        g1_s, g3_s = (n1_s, d1_s, m1_s), (n3_s, d3_s, m3_s)

        def first_block(q_ref, k_ref, v_ref, r0):
            k = k_ref[pl.ds(r0, BLOCK), cs]
            s = _stack_scores([q_ref[pl.ds(r0, BLOCK), cs]], jnp.concatenate([k, k], axis=0), mlo, mhi)
            p, ms = _stack_softmax(s[:, :BLOCK] + bias1_ref[...], None)
            return _stack_values(p, v_ref[pl.ds(r0, BLOCK), cs], ms, lane_lo, None, False)[0]

        def banded(q_ref, k_ref, v_ref, r0):
            return pair(q_ref[pl.ds(r0, BLOCK), cs], k_ref[pl.ds(r0 - BLOCK, 2 * BLOCK), cs],
                        v_ref[pl.ds(r0 - BLOCK, 2 * BLOCK), cs], bias_ref[...])


        for rho in range(r16):
            stats = first_block(q3_ref, k3_ref, v3_ref, rho * BLOCK)
            dst = pl.ds((rho % r4) * rows4 + rho // r4, BLOCK, stride=r4)
            for ref, val in zip(g3_s, stats):
                ref[pp, dst, :] = val

        for i in range(t // BLOCK):
            r0 = i * BLOCK
            stats = first_block(q1_ref, k1_ref, v1_ref, r0) if i == 0 else banded(q1_ref, k1_ref, v1_ref, r0)
            for ref, val in zip(g1_s, stats):
                ref[pp, pl.ds(r0, BLOCK), :] = val

        for idx in range(r4 * nb4):
            rho, sb = idx // nb4, idx % nb4
            r0 = idx * BLOCK
            n2, d2, m2 = first_block(q2_ref, k2_ref, v2_ref, r0) if sb == 0 else banded(q2_ref, k2_ref, v2_ref, r0)
            nat = pl.ds(sb * (BLOCK * r4) + rho, BLOCK, stride=r4)
            n1, d1, m1 = (ref[pp, nat, :] for ref in g1_s)
            n3, d3, m3 = (ref[pp, pl.ds(r0, BLOCK), :] for ref in g3_s)
            mx = jnp.maximum(jnp.maximum(m1, m2), m3)
            a1, a2, a3 = jnp.exp2(m1 - mx), jnp.exp2(m2 - mx), jnp.exp2(m3 - mx)
            fin_s[pp, nat, :] = (a1 * n1 + a2 * n2 + a3 * n3) / (a1 * d1 + a2 * d2 + a3 * d3)

        o_ref[:, cs] = fin_s[pp].astype(BF16)

    for pp in range(o_ref.shape[1] // LANES):
        head_pair(pp)


def _attn_b(b1, b2, b3):
    bsz, t, w3 = b1.shape
    gw = w3 // 3
    wblk = B_PAIRS_PER_STEP * LANES
    nblk = gw // wblk

    def spec(part):
        return pl.BlockSpec((None, t, wblk), lambda b, j: (b, 0, part * nblk + j))

    return pl.pallas_call(
        _attn_b_kernel,
        out_shape=jax.ShapeDtypeStruct((bsz, t, gw), BF16),
        grid=(bsz, nblk),
        in_specs=[spec(0), spec(1), spec(2)] * 3,
        out_specs=pl.BlockSpec((None, t, wblk), lambda b, j: (b, 0, j)),
        scratch_shapes=[pltpu.VMEM((B_PAIRS_PER_STEP, t, LANES), F32)] * 7
                       + [pltpu.VMEM((2 * BLOCK, 2 * BLOCK), F32),
                          pltpu.VMEM((2 * BLOCK, BLOCK), F32)],
        compiler_params=pltpu.CompilerParams(dimension_semantics=("arbitrary", "arbitrary"),
                                             vmem_limit_bytes=VMEM_LIMIT),
        name="attn_b",
    )(b1, b1, b1, b2, b2, b2, b3, b3, b3)


def _layer_norm(h, g, b):
    mu = jnp.mean(h, axis=-1, keepdims=True)
    xc = h - mu
    var = jnp.mean(xc * xc, axis=-1, keepdims=True)
    return xc * lax.rsqrt(var + LN_EPS) * g + b


def _post_kernel(x_ref, oa_ref, ob_ref, scm_ref, shm_ref, gm_ref, shf_ref, scf_ref, gf_ref,
                 g1_ref, b1_ref, g2_ref, b2_ref,
                 wg_ref, wa_ref, wb_ref, wo_ref, wgu_ref, wd_ref, o_ref, *, alpha, d_ff):
    d = x_ref.shape[1]

    def mix(rows):
        x = x_ref[rows, :]
        u = (x * (1.0 + scm_ref[...]) + shm_ref[...]).astype(BF16)
        oa, ob = oa_ref[rows, :], ob_ref[rows, :]
        parts = []
        for c0 in range(0, d, FF_CHUNK):
            ga = jax.nn.sigmoid(jnp.dot(u, wg_ref[:, c0:c0 + FF_CHUNK], preferred_element_type=F32))
            ya = jnp.dot(oa, wa_ref[:, c0:c0 + FF_CHUNK], preferred_element_type=F32)
            gb = jax.nn.sigmoid(jnp.dot(u, wg_ref[:, d + c0:d + c0 + FF_CHUNK], preferred_element_type=F32))
            yb = jnp.dot(ob, wb_ref[:, c0:c0 + FF_CHUNK], preferred_element_type=F32)
            parts.append((ga * ya + gb * yb).astype(BF16))
        return jnp.dot(jnp.concatenate(parts, axis=1), wo_ref[...], preferred_element_type=F32)

    def norm1(rows, y):
        return _layer_norm(alpha * x_ref[rows, :] + (1.0 + gm_ref[...]) * y, g1_ref[...], b1_ref[...])

    def ffn(x1):
        u2 = (x1 * (1.0 + scf_ref[...]) + shf_ref[...]).astype(BF16)
        acts = []
        for c in range(d_ff // FF_CHUNK):
            c0 = c * FF_CHUNK
            hg = jnp.dot(u2, wgu_ref[:, c0:c0 + FF_CHUNK], preferred_element_type=F32)
            hu = jnp.dot(u2, wgu_ref[:, d_ff + c0:d_ff + c0 + FF_CHUNK], preferred_element_type=F32)
            acts.append((hg * jax.nn.sigmoid(hg) * hu).astype(BF16))
        return jnp.dot(jnp.concatenate(acts, axis=1), wd_ref[...], preferred_element_type=F32)

    def norm2(rows, x1, acc):
        o_ref[rows, :] = _layer_norm(alpha * x1 + (1.0 + gf_ref[...]) * acc, g2_ref[...], b2_ref[...])

    subs = [slice(r0, r0 + SUB_POST) for r0 in range(0, x_ref.shape[0], SUB_POST)]
    ys = [mix(rows) for rows in subs]
    x1s, accs = [], []
    for i, rows in enumerate(subs):
        x1s.append(norm1(rows, ys[i]))
        accs.append(ffn(x1s[i]))
        if i > 0:
            norm2(subs[i - 1], x1s[i - 1], accs[i - 1])
    norm2(subs[-1], x1s[-1], accs[-1])


def _post(x, oa, ob, mod3, ln1_g, ln1_b, ln2_g, ln2_b, wg, wa, wb, wo, wgu, wd, alpha):
    bsz, t, d = x.shape
    tm = TM_POST
    tiles = t // tm
    d_ff = wd.shape[0]

    def bq(i):
        return i // tiles, i % tiles

    nat = lambda width: pl.BlockSpec((None, tm, width), lambda i: (bq(i)[0], bq(i)[1], 0))
    modspec = lambda k: pl.BlockSpec((None, 1, d), lambda i: (bq(i)[0], 0, k))
    vec = pl.BlockSpec((1, d), lambda i: (0, 0))
    in_specs = [nat(d), nat(oa.shape[2]), nat(ob.shape[2]),
                modspec(1), modspec(0), modspec(2), modspec(3), modspec(4), modspec(5),
                vec, vec, vec, vec,
                _resident(wg.shape), _resident(wa.shape), _resident(wb.shape),
                _resident(wo.shape), _resident(wgu.shape), _resident(wd.shape)]
    row = lambda v: v.reshape(1, d)
    return pl.pallas_call(
        functools.partial(_post_kernel, alpha=alpha, d_ff=d_ff),
        out_shape=jax.ShapeDtypeStruct((bsz, t, d), F32),
        grid=(bsz * tiles,),
        in_specs=in_specs,
        out_specs=nat(d),
        compiler_params=pltpu.CompilerParams(dimension_semantics=("arbitrary",),
                                             vmem_limit_bytes=VMEM_LIMIT_POST),
        name="post",
    )(x, oa, ob, mod3, mod3, mod3, mod3, mod3, mod3,
      row(ln1_g), row(ln1_b), row(ln2_g), row(ln2_b), wg, wa, wb, wo, wgu, wd)


def kernel(x, c, positions, w_ada, b_ada, w_in, sinks, w_branch_a, w_branch_b, w_o,
           ln1_g, ln1_b, w_gate_up, w_down, ln2_g, ln2_b):
    depth = w_ada.shape[0]
    bsz, t, d = x.shape
    alpha = (2 * depth) ** 0.25
    a_q_w = A_Q_HEADS * HEAD_DIM
    a_kv_w = A_KV_HEADS * HEAD_DIM
    b_w = B_HEADS_PER_GROUP * len(B_PATTERNS) * HEAD_DIM
    cols = {"qa": 0, "qa_w": a_q_w, "ka": a_q_w, "kva_w": a_kv_w,
            "qb": a_q_w + 2 * a_kv_w, "kb": a_q_w + 2 * a_kv_w + b_w,
            "vb": a_q_w + 2 * a_kv_w + 2 * b_w, "bg_w": B_HEADS_PER_GROUP * HEAD_DIM}
    gate0 = a_q_w + 2 * a_kv_w + 3 * b_w

    for l in range(depth):
        mod, w_qkv = _ada(c, w_ada[l], b_ada[l], w_in[l], gate0)
        mod3 = mod.reshape(bsz, 1, 6 * d)
        qa, ka, va, b1, b2, b3 = _inproj(x, mod3, positions, w_qkv, cols)
        full = lambda w: (w, 0, w.shape[1])
        post_w = [(w_in[l], gate0, w_in.shape[2] - gate0), full(w_branch_a[l]), full(w_branch_b[l]),
                  full(w_o[l]), full(w_gate_up[l]), full(w_down[l])]
        oa, post_w_bf = _attn_a(sinks[l], qa, ka, va, post_w)
        ob = _attn_b(b1, b2, b3)
        x = _post(x, oa, ob, mod3, ln1_g[l], ln1_b[l], ln2_g[l], ln2_b[l], *post_w_bf, alpha)
    return x
```

```python
import functools

import jax
import jax.numpy as jnp
from jax import lax
from jax.experimental import pallas as pl
from jax.experimental.pallas import tpu as pltpu

F32 = jnp.float32
BF16 = jnp.bfloat16

HEAD_DIM = 64
HALF = HEAD_DIM // 2
BLOCK = 128
LANES = 128
A_Q_HEADS = 16
A_KV_HEADS = 2
A_GROUP = A_Q_HEADS // A_KV_HEADS
A_WINDOW = 128
B_PATTERNS = ((128, 1), (512, 4), (2048, 16))
B_HEADS_PER_GROUP = 8
ROPE_THETA = 10000.0
LN_EPS = 1e-5
NEG_INF = -1e30
LOG2E = 1.4426950408889634

TM_IN = 1024
SUB_IN = 512
TM_POST = 1024
SUB_POST = 512
FF_CHUNK = 256
ADA_STEPS = 8
B_PAIRS_PER_STEP = 2
A_SLABS_PER_DOT = 2
VMEM_BYTES = 64 * 1024 * 1024
VMEM_LIMIT = VMEM_BYTES - 8 * 1024 * 1024
VMEM_LIMIT_POST = VMEM_BYTES - 2 * 1024 * 1024

_NT = (((1,), (1,)), ((), ()))


def _resident(shape):
    nd = len(shape)
    return pl.BlockSpec(shape, lambda *_: (0,) * nd, pipeline_mode=pl.Buffered(1))


def _ada_kernel(c_ref, w_ref, b_ref, wqkv_ref, o_ref, wqkv_bf_ref):
    c = c_ref[...]
    act = (c * jax.nn.sigmoid(c)).astype(BF16)
    o_ref[...] = jnp.dot(act, w_ref[...].astype(BF16), preferred_element_type=F32) + b_ref[...]
    wqkv_bf_ref[...] = wqkv_ref[...].astype(BF16)


def _ada(c, w, b, w_in, qkv_cols):
    bsz, d = c.shape
    e = w.shape[1]
    steps = ADA_STEPS
    tn = e // steps
    rows = w_in.shape[0] // steps
    return pl.pallas_call(
        _ada_kernel,
        out_shape=[jax.ShapeDtypeStruct((bsz, e), F32),
                   jax.ShapeDtypeStruct((w_in.shape[0], qkv_cols), BF16)],
        grid=(steps,),
        in_specs=[pl.BlockSpec((bsz, d), lambda j: (0, 0)),
                  pl.BlockSpec((d, tn), lambda j: (0, j)),
                  pl.BlockSpec((1, tn), lambda j: (0, j)),
                  pl.BlockSpec((rows, qkv_cols), lambda j: (j, 0))],
        out_specs=[pl.BlockSpec((bsz, tn), lambda j: (0, j)),
                   pl.BlockSpec((rows, qkv_cols), lambda j: (j, 0))],
        compiler_params=pltpu.CompilerParams(dimension_semantics=("arbitrary",),
                                             vmem_limit_bytes=VMEM_LIMIT),
        name="ada",
    )(c, w, b.reshape(1, e), w_in)


def _rope_cols(y, ctab, stab, first_half, scale, out_dtype=BF16):
    outs = []
    for j in range(y.shape[1] // LANES):
        slab = y[:, j * LANES:(j + 1) * LANES]
        swapped = jnp.where(first_half,
                            pltpu.roll(slab, LANES - HALF, axis=1),
                            pltpu.roll(slab, HALF, axis=1))
        r = slab * ctab + swapped * stab
        if scale != 1.0:
            r = r * scale
        outs.append(r.astype(out_dtype))
    return outs[0] if len(outs) == 1 else jnp.concatenate(outs, axis=1)


def _inproj_kernel(x_ref, sc_ref, sh_ref, pos_ref, inv_ref, w_ref,
                   qa_ref, ka_ref, va_ref, b1_ref, b2_ref, b3_ref, us_ref, cn_ref, sn_ref, *, cols):
    tm, d = x_ref.shape
    nslab = d // LANES
    sub = SUB_IN
    lane = lax.broadcasted_iota(jnp.int32, (sub, LANES), 1)
    first_half = (lane & HALF) == 0
    lane_lo = lane < HEAD_DIM
    qscale = HEAD_DIM ** -0.5 * LOG2E
    scale1 = 1.0 + sc_ref[...]
    shift = sh_ref[...]
    half_q = cols["qa_w"] // 2
    kvw = cols["kva_w"]
    gw = cols["bg_w"]

    def proj(u, c0, width):
        return jnp.dot(u, w_ref[:, c0:c0 + width], preferred_element_type=F32)

    for base in range(0, tm, sub):
        rows = slice(base, base + sub)
        uf = x_ref[rows, :] * scale1 + shift
        un = uf.astype(BF16)
        for j in range(nslab):
            us_ref[j, rows, :] = uf[:, j * LANES:(j + 1) * LANES]
        ang = pos_ref[:, rows] * inv_ref[...]
        cos, sin = jnp.cos(ang), jnp.sin(ang)
        reps = LANES // HEAD_DIM
        cn_ref[rows, :] = jnp.concatenate([cos, cos] * reps, axis=0).T
        sn_ref[rows, :] = jnp.concatenate([-sin, sin] * reps, axis=0).T

        def strided_rows(load, r):
            per = sub // r
            return jnp.concatenate([load(pl.ds(base + rho, per, stride=r)) for rho in range(r)], axis=0)

        def u_dilated(r):
            def load(idx):
                return jnp.concatenate([us_ref[j, idx, :] for j in range(nslab)], axis=1)
            return strided_rows(load, r).astype(BF16)

        def tab_dilated(ref, r):
            return strided_rows(lambda idx: ref[idx, :], r)

        def store_nat(c0, val):
            b1_ref[rows, c0:c0 + gw] = val

        def store_res(ref):
            nres = ref.shape[0]
            per = sub // nres
            off = base // nres

            def store(c0, val):
                for rho in range(nres):
                    ref[rho, off:off + per, c0:c0 + gw] = val[rho * per:(rho + 1) * per]
            return store

        groups = [(un, store_nat)] + [(u_dilated(B_PATTERNS[g][1]), store_res(ref))
                                      for g, ref in ((1, b2_ref), (2, b3_ref))]

        def values(g):
            u, store = groups[g]
            store(2 * gw, proj(u, cols["vb"] + g * gw, gw).astype(BF16))

        def queries_keys(g, ctab, stab):
            u, store = groups[g]
            store(0, _rope_cols(proj(u, cols["qb"] + g * gw, gw), ctab, stab, first_half, qscale))
            store(gw, _rope_cols(proj(u, cols["kb"] + g * gw, gw), ctab, stab, first_half, 1.0))

        values(0)
        cn, sn = cn_ref[rows, :], sn_ref[rows, :]
        for h in range(2):
            y = proj(un, cols["qa"] + h * half_q, half_q)
            qa_ref[rows, h * half_q:(h + 1) * half_q] = _rope_cols(y, cn, sn, first_half, qscale)
        y = proj(un, cols["ka"], 2 * kvw)
        for ref, val in ((ka_ref, _rope_cols(y[:, :kvw], cn, sn, first_half, 1.0, F32)), (va_ref, y[:, kvw:])):
            swp = pltpu.roll(val, HEAD_DIM, axis=1)
            ref[rows, :] = jnp.concatenate([jnp.where(lane_lo, val, swp), jnp.where(lane_lo, swp, val)],
                                           axis=1).astype(BF16)
        queries_keys(0, cn, sn)
        for g in (1, 2):
            r = B_PATTERNS[g][1]
            queries_keys(g, tab_dilated(cn_ref, r), tab_dilated(sn_ref, r))
            values(g)


def _inproj(x, mod3, positions, w_bf, cols):
    bsz, t, d = x.shape
    tm = TM_IN
    tiles = t // tm
    r4, r16 = B_PATTERNS[1][1], B_PATTERNS[2][1]

    def bq(i):
        return i // tiles, i % tiles

    nat = lambda width: pl.BlockSpec((None, tm, width), lambda i: (bq(i)[0], bq(i)[1], 0))
    res = lambda r, width: pl.BlockSpec((None, r, tm // r, width), lambda i: (bq(i)[0], 0, bq(i)[1], 0))
    in_specs = [
        nat(d),
        pl.BlockSpec((None, 1, d), lambda i: (bq(i)[0], 0, 1)),
        pl.BlockSpec((None, 1, d), lambda i: (bq(i)[0], 0, 0)),
        pl.BlockSpec((None, 1, tm), lambda i: (bq(i)[0], 0, bq(i)[1])),
        pl.BlockSpec((HALF, 1), lambda i: (0, 0)),
        _resident(w_bf.shape),
    ]
    inv = ROPE_THETA ** (-jnp.arange(HALF, dtype=F32) / HALF)
    gw3 = 3 * cols["bg_w"]
    kv_dup = 2 * cols["kva_w"]
    out_shape = [jax.ShapeDtypeStruct((bsz, t, cols["qa_w"]), BF16),
                 jax.ShapeDtypeStruct((bsz, t, kv_dup), BF16),
                 jax.ShapeDtypeStruct((bsz, t, kv_dup), BF16),
                 jax.ShapeDtypeStruct((bsz, t, gw3), BF16),
                 jax.ShapeDtypeStruct((bsz, r4, t // r4, gw3), BF16),
                 jax.ShapeDtypeStruct((bsz, r16, t // r16, gw3), BF16)]
    out_specs = [nat(cols["qa_w"]), nat(kv_dup), nat(kv_dup),
                 nat(gw3), res(r4, gw3), res(r16, gw3)]
    qa, ka, va, b1, b2, b3 = pl.pallas_call(
        functools.partial(_inproj_kernel, cols=cols),
        out_shape=out_shape,
        grid=(bsz * tiles,),
        in_specs=in_specs,
        out_specs=out_specs,
        scratch_shapes=[pltpu.VMEM((d // LANES, tm, LANES), F32),
                        pltpu.VMEM((tm, LANES), F32), pltpu.VMEM((tm, LANES), F32)],
        compiler_params=pltpu.CompilerParams(dimension_semantics=("arbitrary",),
                                             vmem_limit_bytes=VMEM_LIMIT),
        name="inproj",
    )(x, mod3, mod3, positions.astype(F32).reshape(bsz, 1, t), inv.reshape(HALF, 1), w_bf)
    return qa, ka, va, b1, b2.reshape(bsz, t, gw3), b3.reshape(bsz, t, gw3)


def _band_bias(keys, n_back, prev_valid):
    qi = lax.broadcasted_iota(jnp.int32, (BLOCK, keys), 0)
    c = lax.broadcasted_iota(jnp.int32, (BLOCK, keys), 1)
    dist = qi + (keys - BLOCK) - c
    ok = (dist >= 0) & (dist <= n_back)
    if not prev_valid:
        ok = ok & (c >= keys - BLOCK)
    return jnp.where(ok, 0.0, NEG_INF).astype(F32)


def _lane_consts():
    lane = lax.broadcasted_iota(jnp.int32, (1, LANES), 1)
    mlo = (lane < HEAD_DIM).astype(BF16)
    mhi = (lane >= HEAD_DIM).astype(BF16)
    lane_lo = lax.broadcasted_iota(jnp.int32, (BLOCK, LANES), 1) < HEAD_DIM
    return mlo, mhi, lane_lo


def _stack_scores(qs, kc, mlo, mhi):
    parts = []
    for q in qs:
        parts += [q * mlo, q * mhi]
    return lax.dot_general(jnp.concatenate(parts, axis=0), kc, _NT, preferred_element_type=F32)


def _stack_softmax(s, sinks2):
    heads = [s[h * BLOCK:(h + 1) * BLOCK] for h in range(s.shape[0] // BLOCK)]
    ms = [jnp.max(h, axis=-1, keepdims=True) for h in heads]
    if sinks2 is not None:
        ms = [jnp.maximum(m, sk) for m, sk in zip(ms, sinks2)]
    p = jnp.concatenate([jnp.exp2(h - m) for h, m in zip(heads, ms)], axis=0).astype(BF16)
    return p, ms


def _stack_values(p, vc, ms, lane_lo, sinks2, normalise):
    ones = jnp.ones(vc.shape, BF16)
    r = jnp.dot(p, jnp.concatenate([vc, ones], axis=1), preferred_element_type=F32)
    outs = []
    for j in range(len(ms) // 2):
        e, o = r[2 * j * BLOCK:(2 * j + 1) * BLOCK], r[(2 * j + 1) * BLOCK:(2 * j + 2) * BLOCK]
        me, mo = ms[2 * j], ms[2 * j + 1]
        num = jnp.where(lane_lo, e[:, :LANES], o[:, :LANES])
        den = jnp.where(lane_lo, e[:, LANES:], o[:, LANES:])
        if sinks2 is not None:
            den = den + jnp.where(lane_lo, jnp.exp2(sinks2[2 * j] - me), jnp.exp2(sinks2[2 * j + 1] - mo))
        outs.append(num / den if normalise else (num, den, jnp.where(lane_lo, me, mo)))
    return outs


def _pair_block(q, kc, vc, bias, mlo, mhi, lane_lo):
    p, ms = _stack_softmax(_stack_scores([q], kc, mlo, mhi) + bias, None)
    return _stack_values(p, vc, ms, lane_lo, None, False)[0]


def _window_block(qs, kc, vc, prev_bias, tri, tri_bf, ntri_bf, mlo, mhi, lane_lo, sinks2):
    s = _stack_scores(qs, kc, mlo, mhi)
    merged = jnp.where(tri, s[:, BLOCK:], s[:, :BLOCK] + prev_bias)
    p, ms = _stack_softmax(merged, sinks2)
    p2 = jnp.concatenate([p * ntri_bf, p * tri_bf], axis=1)
    return _stack_values(p2, vc, ms, lane_lo, sinks2, True)


def _attn_a_kernel(sink_ref, q_ref, k_ref, v_ref, *rest, casts):
    nw = len(casts)
    w_refs, o_ref, wbf_refs = rest[:nw], rest[nw], rest[nw + 1:]
    for w_ref, wbf_ref, (c0, width) in zip(w_refs, wbf_refs, casts):
        wbf_ref[...] = w_ref[:, c0:c0 + width].astype(BF16)
    t = q_ref.shape[0]
    nblk = t // BLOCK
    nslab = A_GROUP // 2
    assert A_WINDOW == BLOCK
    mlo, mhi, lane_lo = _lane_consts()
    stack = 2 * A_SLABS_PER_DOT * BLOCK
    qi = lax.broadcasted_iota(jnp.int32, (stack, BLOCK), 0) & (BLOCK - 1)
    tri = lax.broadcasted_iota(jnp.int32, (stack, BLOCK), 1) <= qi
    tri_bf = tri.astype(F32).astype(BF16)
    ntri_bf = 1.0 - tri_bf

    for i in range(nblk):
        r0 = i * BLOCK
        p0 = max(i - 1, 0) * BLOCK
        prev_bias = 0.0 if i > 0 else NEG_INF
        for hk in range(A_KV_HEADS):
            hc = slice(hk * LANES, (hk + 1) * LANES)
            kc = jnp.concatenate([k_ref[pl.ds(p0, BLOCK), hc], k_ref[pl.ds(r0, BLOCK), hc]], axis=0)
            vc = jnp.concatenate([v_ref[pl.ds(p0, BLOCK), hc], v_ref[pl.ds(r0, BLOCK), hc]], axis=0)
            for s0 in range(0, nslab, A_SLABS_PER_DOT):
                slabs = [hk * nslab + s0 + u for u in range(A_SLABS_PER_DOT)]
                cols = [slice(sl * LANES, (sl + 1) * LANES) for sl in slabs]
                sinks2 = [sink_ref[2 * sl + h] * LOG2E for sl in slabs for h in range(2)]
                outs = _window_block([q_ref[pl.ds(r0, BLOCK), c] for c in cols], kc, vc, prev_bias,
                                     tri, tri_bf, ntri_bf, mlo, mhi, lane_lo, sinks2)
                for c, out in zip(cols, outs):
                    o_ref[pl.ds(r0, BLOCK), c] = out.astype(BF16)


def _attn_a(sinks, qa, ka, va, weights):
    bsz, t, qw = qa.shape
    kw = ka.shape[2]
    seq = lambda width: pl.BlockSpec((None, t, width), lambda b: (b, 0, 0))
    chunk = lambda w, width: pl.BlockSpec((w.shape[0] // bsz, width), lambda b: (b, 0))
    outs = pl.pallas_call(
        functools.partial(_attn_a_kernel, casts=tuple((c0, width) for _, c0, width in weights)),
        out_shape=[jax.ShapeDtypeStruct((bsz, t, qw), BF16)]
                  + [jax.ShapeDtypeStruct((w.shape[0], width), BF16) for w, _, width in weights],
        grid=(bsz,),
        in_specs=[pl.BlockSpec(memory_space=pltpu.SMEM), seq(qw), seq(kw), seq(kw)]
                 + [chunk(w, w.shape[1]) for w, _, _ in weights],
        out_specs=[seq(qw)] + [chunk(w, width) for w, _, width in weights],
        compiler_params=pltpu.CompilerParams(dimension_semantics=("arbitrary",),
                                             vmem_limit_bytes=VMEM_LIMIT),
        name="attn_a",
    )(sinks, qa, ka, va, *[w for w, _, _ in weights])
    return outs[0], outs[1:]


def _attn_b_kernel(q1_ref, k1_ref, v1_ref, q2_ref, k2_ref, v2_ref, q3_ref, k3_ref, v3_ref,
                   o_ref, n1_s, d1_s, m1_s, n3_s, d3_s, m3_s, fin_s, bias_ref, bias1_ref):
    t = q1_ref.shape[0]
    mlo, mhi, lane_lo = _lane_consts()
    pair = functools.partial(_pair_block, mlo=mlo, mhi=mhi, lane_lo=lane_lo)
    (w1, _), (w4, r4), (w16, r16) = B_PATTERNS
    nb4 = t // r4 // BLOCK
    rows4 = t // r4
    assert w1 == w4 // r4 == w16 // r16 and t // r16 == BLOCK and r16 % r4 == 0

    twice = lambda b: jnp.concatenate([b, b], axis=0)
    bias_ref[...] = twice(_band_bias(2 * BLOCK, w1, True))
    bias1_ref[...] = twice(_band_bias(BLOCK, w1, True))

    def head_pair(pp):
        cs = slice(pp * LANES, (pp + 1) * LANES)
        g1_s, g3_s = (n1_s, d1_s, m1_s), (n3_s, d3_s, m3_s)

        def first_block(q_ref, k_ref, v_ref, r0):
            k = k_ref[pl.ds(r0, BLOCK), cs]
            s = _stack_scores([q_ref[pl.ds(r0, BLOCK), cs]], jnp.concatenate([k, k], axis=0), mlo, mhi)
            p, ms = _stack_softmax(s[:, :BLOCK] + bias1_ref[...], None)
            return _stack_values(p, v_ref[pl.ds(r0, BLOCK), cs], ms, lane_lo, None, False)[0]

        def banded(q_ref, k_ref, v_ref, r0):
            return pair(q_ref[pl.ds(r0, BLOCK), cs], k_ref[pl.ds(r0 - BLOCK, 2 * BLOCK), cs],
                        v_ref[pl.ds(r0 - BLOCK, 2 * BLOCK), cs], bias_ref[...])


        for rho in range(r16):
            stats = first_block(q3_ref, k3_ref, v3_ref, rho * BLOCK)
            dst = pl.ds((rho % r4) * rows4 + rho // r4, BLOCK, stride=r4)
            for ref, val in zip(g3_s, stats):
                ref[pp, dst, :] = val

        for i in range(t // BLOCK):
            r0 = i * BLOCK
            stats = first_block(q1_ref, k1_ref, v1_ref, r0) if i == 0 else banded(q1_ref, k1_ref, v1_ref, r0)
            for ref, val in zip(g1_s, stats):
                ref[pp, pl.ds(r0, BLOCK), :] = val

        for idx in range(r4 * nb4):
            rho, sb = idx // nb4, idx % nb4
            r0 = idx * BLOCK
            n2, d2, m2 = first_block(q2_ref, k2_ref, v2_ref, r0) if sb == 0 else banded(q2_ref, k2_ref, v2_ref, r0)
            nat = pl.ds(sb * (BLOCK * r4) + rho, BLOCK, stride=r4)
            n1, d1, m1 = (ref[pp, nat, :] for ref in g1_s)
            n3, d3, m3 = (ref[pp, pl.ds(r0, BLOCK), :] for ref in g3_s)
            mx = jnp.maximum(jnp.maximum(m1, m2), m3)
            a1, a2, a3 = jnp.exp2(m1 - mx), jnp.exp2(m2 - mx), jnp.exp2(m3 - mx)
            fin_s[pp, nat, :] = (a1 * n1 + a2 * n2 + a3 * n3) / (a1 * d1 + a2 * d2 + a3 * d3)

        o_ref[:, cs] = fin_s[pp].astype(BF16)

    for pp in range(o_ref.shape[1] // LANES):
        head_pair(pp)


def _attn_b(b1, b2, b3):
    bsz, t, w3 = b1.shape
    gw = w3 // 3
    wblk = B_PAIRS_PER_STEP * LANES
    nblk = gw // wblk

    def spec(part):
        return pl.BlockSpec((None, t, wblk), lambda b, j: (b, 0, part * nblk + j))

    return pl.pallas_call(
        _attn_b_kernel,
        out_shape=jax.ShapeDtypeStruct((bsz, t, gw), BF16),
        grid=(bsz, nblk),
        in_specs=[spec(0), spec(1), spec(2)] * 3,
        out_specs=pl.BlockSpec((None, t, wblk), lambda b, j: (b, 0, j)),
        scratch_shapes=[pltpu.VMEM((B_PAIRS_PER_STEP, t, LANES), F32)] * 7
                       + [pltpu.VMEM((2 * BLOCK, 2 * BLOCK), F32),
                          pltpu.VMEM((2 * BLOCK, BLOCK), F32)],
        compiler_params=pltpu.CompilerParams(dimension_semantics=("arbitrary", "arbitrary"),
                                             vmem_limit_bytes=VMEM_LIMIT),
        name="attn_b",
    )(b1, b1, b1, b2, b2, b2, b3, b3, b3)


def _layer_norm(h, g, b):
    mu = jnp.mean(h, axis=-1, keepdims=True)
    xc = h - mu
    var = jnp.mean(xc * xc, axis=-1, keepdims=True)
    return xc * lax.rsqrt(var + LN_EPS) * g + b


def _post_kernel(x_ref, oa_ref, ob_ref, scm_ref, shm_ref, gm_ref, shf_ref, scf_ref, gf_ref,
                 g1_ref, b1_ref, g2_ref, b2_ref,
                 wg_ref, wa_ref, wb_ref, wo_ref, wgu_ref, wd_ref, o_ref, *, alpha, d_ff):
    d = x_ref.shape[1]

    def mix(rows):
        x = x_ref[rows, :]
        u = (x * (1.0 + scm_ref[...]) + shm_ref[...]).astype(BF16)
        oa, ob = oa_ref[rows, :], ob_ref[rows, :]
        parts = []
        for c0 in range(0, d, FF_CHUNK):
            ga = jax.nn.sigmoid(jnp.dot(u, wg_ref[:, c0:c0 + FF_CHUNK], preferred_element_type=F32))
            ya = jnp.dot(oa, wa_ref[:, c0:c0 + FF_CHUNK], preferred_element_type=F32)
            gb = jax.nn.sigmoid(jnp.dot(u, wg_ref[:, d + c0:d + c0 + FF_CHUNK], preferred_element_type=F32))
            yb = jnp.dot(ob, wb_ref[:, c0:c0 + FF_CHUNK], preferred_element_type=F32)
            parts.append((ga * ya + gb * yb).astype(BF16))
        return jnp.dot(jnp.concatenate(parts, axis=1), wo_ref[...], preferred_element_type=F32)

    def norm1(rows, y):
        return _layer_norm(alpha * x_ref[rows, :] + (1.0 + gm_ref[...]) * y, g1_ref[...], b1_ref[...])

    def ffn_act(x1):
        u2 = (x1 * (1.0 + scf_ref[...]) + shf_ref[...]).astype(BF16)
        acts = []
        for c in range(d_ff // FF_CHUNK):
            c0 = c * FF_CHUNK
            hg = jnp.dot(u2, wgu_ref[:, c0:c0 + FF_CHUNK], preferred_element_type=F32)
            hu = jnp.dot(u2, wgu_ref[:, d_ff + c0:d_ff + c0 + FF_CHUNK], preferred_element_type=F32)
            acts.append((hg * jax.nn.sigmoid(hg) * hu).astype(BF16))
        return jnp.concatenate(acts, axis=1)

    def down(act):
        return jnp.dot(act, wd_ref[...], preferred_element_type=F32)

    def norm2(rows, x1, acc):
        o_ref[rows, :] = _layer_norm(alpha * x1 + (1.0 + gf_ref[...]) * acc, g2_ref[...], b2_ref[...])

    subs = [slice(r0, r0 + SUB_POST) for r0 in range(0, x_ref.shape[0], SUB_POST)]
    ys = [mix(rows) for rows in subs]
    x1s, accs = [], []
    for i, rows in enumerate(subs[:-1]):
        x1s.append(norm1(rows, ys[i]))
        accs.append(down(ffn_act(x1s[i])))
        if i > 0:
            norm2(subs[i - 1], x1s[i - 1], accs[i - 1])
    last = subs[-1]
    x1 = norm1(last, ys[-1])
    act = ffn_act(x1)
    if len(subs) > 1:
        norm2(subs[-2], x1s[-1], accs[-1])
    half = SUB_POST // 2
    for h0 in (0, half):
        norm2(slice(last.start + h0, last.start + h0 + half), x1[h0:h0 + half], down(act[h0:h0 + half]))


def _post(x, oa, ob, mod3, ln1_g, ln1_b, ln2_g, ln2_b, wg, wa, wb, wo, wgu, wd, alpha):
    bsz, t, d = x.shape
    tm = TM_POST
    tiles = t // tm
    d_ff = wd.shape[0]

    def bq(i):
        return i // tiles, i % tiles

    nat = lambda width: pl.BlockSpec((None, tm, width), lambda i: (bq(i)[0], bq(i)[1], 0))
    modspec = lambda k: pl.BlockSpec((None, 1, d), lambda i: (bq(i)[0], 0, k))
    vec = pl.BlockSpec((1, d), lambda i: (0, 0))
    in_specs = [nat(d), nat(oa.shape[2]), nat(ob.shape[2]),
                modspec(1), modspec(0), modspec(2), modspec(3), modspec(4), modspec(5),
                vec, vec, vec, vec,
                _resident(wg.shape), _resident(wa.shape), _resident(wb.shape),
                _resident(wo.shape), _resident(wgu.shape), _resident(wd.shape)]
    row = lambda v: v.reshape(1, d)
    return pl.pallas_call(
        functools.partial(_post_kernel, alpha=alpha, d_ff=d_ff),
        out_shape=jax.ShapeDtypeStruct((bsz, t, d), F32),
        grid=(bsz * tiles,),
        in_specs=in_specs,
        out_specs=nat(d),
        compiler_params=pltpu.CompilerParams(dimension_semantics=("arbitrary",),
                                             vmem_limit_bytes=VMEM_LIMIT_POST),
        name="post",
    )(x, oa, ob, mod3, mod3, mod3, mod3, mod3, mod3,
      row(ln1_g), row(ln1_b), row(ln2_g), row(ln2_b), wg, wa, wb, wo, wgu, wd)


def kernel(x, c, positions, w_ada, b_ada, w_in, sinks, w_branch_a, w_branch_b, w_o,
           ln1_g, ln1_b, w_gate_up, w_down, ln2_g, ln2_b):
    depth = w_ada.shape[0]
    bsz, t, d = x.shape
    alpha = (2 * depth) ** 0.25
    a_q_w = A_Q_HEADS * HEAD_DIM
    a_kv_w = A_KV_HEADS * HEAD_DIM
    b_w = B_HEADS_PER_GROUP * len(B_PATTERNS) * HEAD_DIM
    cols = {"qa": 0, "qa_w": a_q_w, "ka": a_q_w, "kva_w": a_kv_w,
            "qb": a_q_w + 2 * a_kv_w, "kb": a_q_w + 2 * a_kv_w + b_w,
            "vb": a_q_w + 2 * a_kv_w + 2 * b_w, "bg_w": B_HEADS_PER_GROUP * HEAD_DIM}
    gate0 = a_q_w + 2 * a_kv_w + 3 * b_w

    for l in range(depth):
        mod, w_qkv = _ada(c, w_ada[l], b_ada[l], w_in[l], gate0)
        mod3 = mod.reshape(bsz, 1, 6 * d)
        qa, ka, va, b1, b2, b3 = _inproj(x, mod3, positions, w_qkv, cols)
        full = lambda w: (w, 0, w.shape[1])
        post_w = [(w_in[l], gate0, w_in.shape[2] - gate0), full(w_branch_a[l]), full(w_branch_b[l]),
                  full(w_o[l]), full(w_gate_up[l]), full(w_down[l])]
        oa, post_w_bf = _attn_a(sinks[l], qa, ka, va, post_w)
        ob = _attn_b(b1, b2, b3)
        x = _post(x, oa, ob, mod3, ln1_g[l], ln1_b[l], ln2_g[l], ln2_b[l], *post_w_bf, alpha)
    return x
```
